```python
import math
import jax, jax.numpy as jnp
from jax import lax
import numpy as np

D_MODEL = 1024
BATCH = 8
SEQ = 8192
DEPTH = 2
DEC_BATCH = 8
DEC_SEQ = 16
PAST_LEN = 2048

CHUNK = 64
PLE_DIM = 256
MIX_WIDTH = D_MODEL
D_FF = 4 * D_MODEL
ROPE_THETA = 10000.0
NORM_EPS = 1e-6
L2_EPS = 1e-6
Q_BLOCK = 128

A_HEADS = 4
A_DK = MIX_WIDTH // 4 // A_HEADS
A_DV = A_DK
A_QK = A_HEADS * A_DK
CONV_W = 4
A_CONV_CH = 2 * A_QK + A_HEADS * A_DV

B_HEADS = 4
B_DV = MIX_WIDTH // 2 // B_HEADS
B_DQK = B_DV // 2

C_HEADS = 4
C_HD = MIX_WIDTH // 4 // C_HEADS
C_WIDTH = C_HEADS * C_HD
C_W_RANK = 64
C_A_RANK = 64
C_G_RANK = 128
C_PROJ = 3 * C_WIDTH + C_W_RANK + C_A_RANK + C_G_RANK
C_LN_EPS = 64e-5
RWKV_DECAY_OFFSET = 0.5

PROJ_SIZES = (A_CONV_CH, A_HEADS * A_DV, A_HEADS, A_HEADS,
              B_HEADS * 2 * B_DQK, B_HEADS * 2 * B_DQK, B_HEADS * B_DV, C_PROJ)
IN_WIDTH = (A_CONV_CH + A_HEADS * A_DV + 2 * A_HEADS
            + 2 * B_HEADS * 2 * B_DQK + B_HEADS * B_DV + C_PROJ)

kernel_name = 'hybrid_streaming_encoder_step'

F32 = jnp.float32


def rmsnorm(x, g):
    xf = x.astype(F32)
    y = xf * lax.rsqrt(jnp.mean(xf * xf, axis=-1, keepdims=True) + NORM_EPS)
    return (y * g.astype(F32)).astype(x.dtype)


def l2norm(x):
    xf = x.astype(F32)
    return (xf * lax.rsqrt(jnp.sum(xf * xf, axis=-1, keepdims=True) + L2_EPS)).astype(x.dtype)


def rope(x, pos):
    d = x.shape[-1]
    half = d // 2
    inv = ROPE_THETA ** (-2.0 * jnp.arange(half, dtype=F32) / d)
    ang = pos.astype(F32)[:, None] * inv[None, :]
    shape = (1, pos.shape[0]) + (1,) * (x.ndim - 3) + (half,)
    cos = jnp.cos(ang).reshape(shape)
    sin = jnp.sin(ang).reshape(shape)
    xf = x.astype(F32)
    x1, x2 = xf[..., :half], xf[..., half:]
    return jnp.concatenate([x1 * cos - x2 * sin, x2 * cos + x1 * sin], axis=-1).astype(x.dtype)


def causal_conv(x, buf, w):
    L = x.shape[1]
    xp = jnp.concatenate([buf.astype(x.dtype), x], axis=1)
    y = xp[:, 0:L] * w[0]
    for j in range(1, CONV_W):
        y = y + xp[:, j:j + L] * w[j]
    return y, xp[:, L:]


def split_proj(proj):
    idx, acc = [], 0
    for s in PROJ_SIZES[:-1]:
        acc += s
        idx.append(acc)
    return jnp.split(proj, idx, axis=-1)


def gated_delta_chunked(q, k, v, g, beta, s0):
    bn, L, nh, dk = q.shape
    dv = v.shape[-1]
    C = min(CHUNK, L)
    N = L // C

    def blk(t):
        t = t.astype(F32).reshape((bn, N, C, nh) + t.shape[3:])
        return jnp.moveaxis(t, 3, 1)

    q, k, v, g, beta = blk(q), blk(k), blk(v), blk(g), blk(beta)
    q = q * (dk ** -0.5)
    g = jnp.cumsum(g, axis=-1)
    idx = jnp.arange(C)
    lower = idx[:, None] >= idx[None, :]
    strict = idx[:, None] > idx[None, :]
    diff = g[..., :, None] - g[..., None, :]
    decay = jnp.where(lower, jnp.exp(jnp.where(lower, diff, 0.0)), 0.0)
    kk = jnp.einsum('bhnid,bhnjd->bhnij', k, k)
    a_mat = jnp.where(strict, beta[..., :, None] * kk * decay, 0.0)
    eye = jnp.eye(C, dtype=F32)
    rhs = jnp.concatenate([v * beta[..., None], k * (beta * jnp.exp(g))[..., None]], axis=-1)
    sol = lax.linalg.triangular_solve(jnp.broadcast_to(eye, a_mat.shape) + a_mat, rhs,
                                      left_side=True, lower=True)
    u_val, w_k = sol[..., :dv], sol[..., dv:]
    qk = jnp.where(lower, jnp.einsum('bhnid,bhnjd->bhnij', q, k) * decay, 0.0)
    q_dec = q * jnp.exp(g)[..., None]
    k_tail = k * jnp.exp(g[..., -1:] - g)[..., None]
    g_last = jnp.exp(g[..., -1])

    def step(s, xs):
        u_c, w_c, qk_c, qd_c, kt_c, gl_c = xs
        u_new = u_c - jnp.einsum('bhcd,bhde->bhce', w_c, s)
        o = jnp.einsum('bhcd,bhde->bhce', qd_c, s) + jnp.einsum('bhij,bhje->bhie', qk_c, u_new)
        s = s * gl_c[..., None, None] + jnp.einsum('bhcd,bhce->bhde', kt_c, u_new)
        return s, o

    xs = (jnp.moveaxis(u_val, 2, 0), jnp.moveaxis(w_k, 2, 0), jnp.moveaxis(qk, 2, 0),
          jnp.moveaxis(q_dec, 2, 0), jnp.moveaxis(k_tail, 2, 0), jnp.moveaxis(g_last, 2, 0))
    s_fin, o = lax.scan(step, s0.astype(F32), xs)
    o = jnp.transpose(o, (1, 0, 3, 2, 4)).reshape(bn, L, nh, dv)
    return o, s_fin


def rwkv7_scan(r, w_log, k, v, kk, a, s0):
    def step(s, xs):
        r_t, w_t, k_t, v_t, kk_t, a_t = xs
        sa = jnp.einsum('bhvk,bhk->bhv', s, -kk_t)
        s = (s * jnp.exp(w_t)[:, :, None, :] + sa[..., :, None] * (kk_t * a_t)[..., None, :]
             + v_t[..., :, None] * k_t[..., None, :])
        return s, jnp.einsum('bhvk,bhk->bhv', s, r_t)

    xs = (jnp.moveaxis(r.astype(F32), 1, 0), jnp.moveaxis(w_log.astype(F32), 1, 0),
          jnp.moveaxis(k.astype(F32), 1, 0), jnp.moveaxis(v.astype(F32), 1, 0),
          jnp.moveaxis(kk.astype(F32), 1, 0), jnp.moveaxis(a.astype(F32), 1, 0))
    s_fin, y = lax.scan(step, s0.astype(F32), xs)
    return jnp.moveaxis(y, 0, 1), s_fin


def diff_attention(q, k, v, q_pos, k_pos, lam):
    bn, lq, nh, _, d = q.shape
    dv = v.shape[-1]
    blk = min(Q_BLOCK, lq)
    nb = lq // blk
    kf = k.astype(F32)
    vf = v.astype(F32)
    k_chunk = k_pos // CHUNK
    neg = jnp.finfo(F32).min

    def one_block(args):
        qb, qp = args
        s = jnp.einsum('bqhmd,bkhmd->bhmqk', qb.astype(F32) * (d ** -0.5), kf)
        visible = k_chunk[None, :] <= (qp // CHUNK)[:, None]
        p = jax.nn.softmax(jnp.where(visible, s, neg), axis=-1)
        pd = p[:, :, 0] - lam * p[:, :, 1]
        return jnp.einsum('bhqk,bkhe->bqhe', pd, vf)

    qb = jnp.moveaxis(q.reshape(bn, nb, blk, nh, 2, d), 1, 0)
    o = lax.map(one_block, (qb, q_pos.reshape(nb, blk)))
    return jnp.moveaxis(o, 0, 1).reshape(bn, lq, nh, dv)


def trunk_layer(h, p_l, past_len, cache_k, cache_v, conv_buf, delta_s, shift_prev, wkv_s,
                norm_mix, w_in, a_conv_w, a_A_log, a_dt_bias, a_norm,
                b_lam_q1, b_lam_k1, b_lam_q2, b_lam_k2, b_norm,
                c_mu, c_w0, c_w_up, c_a0, c_a_up, c_g_up, c_k_k, c_k_a, c_r_k, c_ln_w, c_ln_b,
                w_out, norm_ffn, w_ff1, w_ff2, norm_ple, w_ple_gate, w_ple_proj, layer_idx):
    dt = h.dtype
    bn, L, _ = h.shape
    x = rmsnorm(h, norm_mix)
    a_qkv, a_z, a_a, a_b, b_q, b_k, b_v, c_raw = split_proj(x @ w_in)

    a_c, new_conv = causal_conv(a_qkv, conv_buf, a_conv_w)
    a_c = jax.nn.silu(a_c)
    aq, ak, av = jnp.split(a_c, [A_QK, 2 * A_QK], axis=-1)
    aq = l2norm(aq.reshape(bn, L, A_HEADS, A_DK))
    ak = l2norm(ak.reshape(bn, L, A_HEADS, A_DK))
    av = av.reshape(bn, L, A_HEADS, A_DV)
    a_logdecay = -jnp.exp(a_A_log.astype(F32)) * jax.nn.softplus(a_a.astype(F32) + a_dt_bias)
    a_beta = jax.nn.sigmoid(a_b.astype(F32))
    ao, new_delta = gated_delta_chunked(aq, ak, av, a_logdecay, a_beta, delta_s)
    ao = rmsnorm(ao, a_norm) * jax.nn.silu(a_z.astype(F32)).reshape(bn, L, A_HEADS, A_DV)

    q_pos = past_len + jnp.arange(L, dtype=jnp.int32)
    k_pos = jnp.arange(past_len + L, dtype=jnp.int32)
    bq = rope(b_q.reshape(bn, L, B_HEADS, 2, B_DQK), q_pos)
    bk = rope(b_k.reshape(bn, L, B_HEADS, 2, B_DQK), q_pos)
    bv = b_v.reshape(bn, L, B_HEADS, B_DV)
    new_k = bk.reshape(bn, L, B_HEADS, 2 * B_DQK)
    k_all = jnp.concatenate([cache_k.astype(dt), new_k], axis=1).reshape(bn, past_len + L, B_HEADS, 2, B_DQK)
    v_all = jnp.concatenate([cache_v.astype(dt), bv], axis=1)
    lam_init = 0.8 - 0.6 * math.exp(-0.3 * layer_idx)
    lam = (jnp.exp(jnp.sum(b_lam_q1.astype(F32) * b_lam_k1.astype(F32)))
           - jnp.exp(jnp.sum(b_lam_q2.astype(F32) * b_lam_k2.astype(F32))) + lam_init)
    bo = diff_attention(bq, k_all, v_all, q_pos, k_pos, lam)
    bo = rmsnorm(bo, b_norm) * (1.0 - lam_init)

    prev = jnp.concatenate([shift_prev[:, None, :].astype(dt), c_raw[:, :-1]], axis=1)
    c = c_raw + (prev - c_raw) * c_mu
    new_shift = c_raw[:, -1]
    cr, ck, cv, cwd, cad, cgd = jnp.split(
        c, [C_WIDTH, 2 * C_WIDTH, 3 * C_WIDTH, 3 * C_WIDTH + C_W_RANK,
            3 * C_WIDTH + C_W_RANK + C_A_RANK], axis=-1)
    hd = (bn, L, C_HEADS, C_HD)
    w_log = -jnp.exp(-jax.nn.softplus(-(c_w0 + jnp.tanh(cwd.astype(F32)) @ c_w_up)) - RWKV_DECAY_OFFSET)
    ca = jax.nn.sigmoid(c_a0 + cad.astype(F32) @ c_a_up).reshape(hd)
    cg = jax.nn.sigmoid(cgd.astype(F32)) @ c_g_up
    cr = cr.astype(F32).reshape(hd)
    cv = cv.astype(F32).reshape(hd)
    ck = ck.astype(F32).reshape(hd)
    kk = l2norm(ck * c_k_k.reshape(C_HEADS, C_HD))
    ck = ck * (1.0 + (ca - 1.0) * c_k_a.reshape(C_HEADS, C_HD))
    cy, new_wkv = rwkv7_scan(cr, w_log.reshape(hd), ck, cv, kk, ca, wkv_s)
    mu = jnp.mean(cy, axis=-1, keepdims=True)
    var = jnp.mean(jnp.square(cy - mu), axis=-1, keepdims=True)
    cy = ((cy - mu) * lax.rsqrt(var + C_LN_EPS)).reshape(bn, L, C_WIDTH) * c_ln_w + c_ln_b
    bonus = jnp.sum(cr * ck * c_r_k, axis=-1, keepdims=True) * cv
    co = (cy + bonus.reshape(bn, L, C_WIDTH)) * cg

    mix = jnp.concatenate([ao.reshape(bn, L, -1).astype(dt), bo.reshape(bn, L, -1).astype(dt),
                           co.astype(dt)], axis=-1)
    h = h + mix @ w_out

    u = jax.nn.relu(rmsnorm(h, norm_ffn) @ w_ff1)
    h = h + (u * u) @ w_ff2

    gate = jax.nn.sigmoid(rmsnorm(h, norm_ple) @ w_ple_gate)
    h = h + gate * (p_l @ w_ple_proj)

    return h, (new_conv, new_delta.astype(dt), new_k, bv, new_shift, new_wkv.astype(dt))


def run_trunk(x, p, cache_k, cache_v, conv_buf, delta_s, shift_prev, wkv_s, layer_params, norm_final):
    past_len = cache_k.shape[2]
    h = x
    states = []
    for i in range(DEPTH):
        lp = [t[i] for t in layer_params]
        h, st = trunk_layer(h, p[i], past_len, cache_k[i], cache_v[i], conv_buf[i], delta_s[i],
                            shift_prev[i], wkv_s[i], *lp, layer_idx=i)
        states.append(st)
    new_state = [jnp.stack([st[j] for st in states], axis=0) for j in range(6)]
    return rmsnorm(h, norm_final), new_state


def setup_inputs(seed: int = 0) -> dict:
    key = jax.random.key(seed)
    ks = iter(jax.random.split(key, 48))

    def nrm(shape, s):
        return jax.random.normal(next(ks), shape, F32) * s

    def gain(shape):
        return 1.0 + nrm(shape, 0.01)

    dt0 = jnp.exp(jax.random.uniform(next(ks), (DEPTH, A_HEADS), F32,
                                     minval=math.log(1e-3), maxval=math.log(1e-1)))
    return {
        'x_prompt': nrm((BATCH, SEQ, D_MODEL), 1.0),
        'x_sample': nrm((DEC_BATCH, DEC_SEQ, D_MODEL), 1.0),
        'cache_b_k': nrm((DEPTH, DEC_BATCH, PAST_LEN, B_HEADS, 2 * B_DQK), 1.0),
        'cache_b_v': nrm((DEPTH, DEC_BATCH, PAST_LEN, B_HEADS, B_DV), 1.0),
        'state_a_conv': nrm((DEPTH, DEC_BATCH, CONV_W - 1, A_CONV_CH), 1.0),
        'state_a_delta': nrm((DEPTH, DEC_BATCH, A_HEADS, A_DK, A_DV), 0.1),
        'state_c_shift': nrm((DEPTH, DEC_BATCH, C_PROJ), 1.0),
        'state_c_wkv': nrm((DEPTH, DEC_BATCH, C_HEADS, C_HD, C_HD), 0.5),
        'p_prompt': nrm((DEPTH, BATCH, SEQ, PLE_DIM), 1.0),
        'p_sample': nrm((DEPTH, DEC_BATCH, DEC_SEQ, PLE_DIM), 1.0),
        'norm_mix': gain((DEPTH, D_MODEL)),
        'w_in': nrm((DEPTH, D_MODEL, IN_WIDTH), D_MODEL ** -0.5),
        'a_conv_w': nrm((DEPTH, CONV_W, A_CONV_CH), CONV_W ** -0.5),
        'a_A_log': jnp.log(jax.random.uniform(next(ks), (DEPTH, A_HEADS), F32, minval=1.0, maxval=16.0)),
        'a_dt_bias': dt0 + jnp.log(-jnp.expm1(-dt0)),
        'a_norm': gain((DEPTH, A_DV)),
        'b_lam_q1': nrm((DEPTH, B_DQK), 0.1),
        'b_lam_k1': nrm((DEPTH, B_DQK), 0.1),
        'b_lam_q2': nrm((DEPTH, B_DQK), 0.1),
        'b_lam_k2': nrm((DEPTH, B_DQK), 0.1),
        'b_norm': gain((DEPTH, B_DV)),
        'c_mu': jax.random.uniform(next(ks), (DEPTH, C_PROJ), F32),
        'c_w0': nrm((DEPTH, C_WIDTH), 0.5) - 0.5,
        'c_w_up': nrm((DEPTH, C_W_RANK, C_WIDTH), 0.5 * C_W_RANK ** -0.5),
        'c_a0': nrm((DEPTH, C_WIDTH), 0.1),
        'c_a_up': nrm((DEPTH, C_A_RANK, C_WIDTH), 0.5 * C_A_RANK ** -0.5),
        'c_g_up': nrm((DEPTH, C_G_RANK, C_WIDTH), C_G_RANK ** -0.5),
        'c_k_k': 0.85 + nrm((DEPTH, C_WIDTH), 0.02),
        'c_k_a': 1.0 + nrm((DEPTH, C_WIDTH), 0.02),
        'c_r_k': nrm((DEPTH, C_HEADS, C_HD), 0.1),
        'c_ln_w': gain((DEPTH, C_WIDTH)),
        'c_ln_b': nrm((DEPTH, C_WIDTH), 0.01),
        'w_out': nrm((DEPTH, MIX_WIDTH, D_MODEL), MIX_WIDTH ** -0.5),
        'norm_ffn': gain((DEPTH, D_MODEL)),
        'w_ff1': nrm((DEPTH, D_MODEL, D_FF), D_MODEL ** -0.5),
        'w_ff2': nrm((DEPTH, D_FF, D_MODEL), D_FF ** -0.5),
        'norm_ple': gain((DEPTH, D_MODEL)),
        'w_ple_gate': nrm((DEPTH, D_MODEL, D_MODEL), D_MODEL ** -0.5),
        'w_ple_proj': nrm((DEPTH, PLE_DIM, D_MODEL), PLE_DIM ** -0.5),
        'norm_final': gain((D_MODEL,)),
    }


def reference(x_prompt, x_sample, cache_b_k, cache_b_v, state_a_conv, state_a_delta,
              state_c_shift, state_c_wkv, p_prompt, p_sample,
              norm_mix, w_in, a_conv_w, a_A_log, a_dt_bias, a_norm,
              b_lam_q1, b_lam_k1, b_lam_q2, b_lam_k2, b_norm,
              c_mu, c_w0, c_w_up, c_a0, c_a_up, c_g_up, c_k_k, c_k_a, c_r_k, c_ln_w, c_ln_b,
              w_out, norm_ffn, w_ff1, w_ff2, norm_ple, w_ple_gate, w_ple_proj, norm_final):
    layer_params = (norm_mix, w_in, a_conv_w, a_A_log, a_dt_bias, a_norm,
                    b_lam_q1, b_lam_k1, b_lam_q2, b_lam_k2, b_norm,
                    c_mu, c_w0, c_w_up, c_a0, c_a_up, c_g_up, c_k_k, c_k_a, c_r_k, c_ln_w, c_ln_b,
                    w_out, norm_ffn, w_ff1, w_ff2, norm_ple, w_ple_gate, w_ple_proj)

    bp = x_prompt.shape[0]
    dt = x_prompt.dtype
    zk = jnp.zeros((DEPTH, bp, 0, B_HEADS, 2 * B_DQK), dt)
    zv = jnp.zeros((DEPTH, bp, 0, B_HEADS, B_DV), dt)
    zc = jnp.zeros((DEPTH, bp, CONV_W - 1, A_CONV_CH), dt)
    zd = jnp.zeros((DEPTH, bp, A_HEADS, A_DK, A_DV), dt)
    zs = jnp.zeros((DEPTH, bp, C_PROJ), dt)
    zw = jnp.zeros((DEPTH, bp, C_HEADS, C_HD, C_HD), dt)
    y_prompt, st_p = run_trunk(x_prompt, p_prompt, zk, zv, zc, zd, zs, zw, layer_params, norm_final)
    a_conv_p, a_delta_p, b_k_p, b_v_p, c_shift_p, c_wkv_p = st_p

    y_sample, st_s = run_trunk(x_sample, p_sample, cache_b_k, cache_b_v, state_a_conv, state_a_delta,
                               state_c_shift, state_c_wkv, layer_params, norm_final)
    a_conv_s, a_delta_s, b_k_s, b_v_s, c_shift_s, c_wkv_s = st_s

    return (y_prompt, y_sample,
            a_conv_p, a_delta_p, b_k_p, b_v_p, c_shift_p, c_wkv_p,
            a_conv_s, a_delta_s, b_k_s, b_v_s, c_shift_s, c_wkv_s)
```

```python
import functools
import math

import numpy as np
import jax
import jax.numpy as jnp
from jax import lax
from jax.experimental import pallas as pl
from jax.experimental.pallas import tpu as pltpu

F32 = jnp.float32
BF16 = jnp.bfloat16

CHUNK = 64
ROPE_THETA = 10000.0
NORM_EPS = 1e-6
L2_EPS = 1e-6
C_LN_EPS = 64e-5
RWKV_DECAY_OFFSET = 0.5
MASK_VALUE = float(np.finfo(np.float32).min)

LANES = 128
HEAD = 64
VMEM_LIMIT = 56 * 1024 * 1024
ATTN_TQ = 512
ATTN_TK = 512


def _iota(shape, axis):
    return lax.broadcasted_iota(jnp.int32, shape, axis)


_NN = (((1,), (0,)), ((), ()))
_NT = (((1,), (1,)), ((), ()))
_TN = (((0,), (0,)), ((), ()))


def _mm(a, b, dims=_NN):
    return lax.dot_general(a.astype(BF16), b.astype(BF16), dims, preferred_element_type=F32)


def _mm3(a, b, dims=_NN):
    ah = a.astype(BF16)
    al = (a - ah.astype(F32)).astype(BF16)
    bh = b.astype(BF16)
    bl = (b - bh.astype(F32)).astype(BF16)
    d = functools.partial(lax.dot_general, dimension_numbers=dims, preferred_element_type=F32)
    return d(ah, bh) + d(ah, bl) + d(al, bh)


def _split3(x):
    h = x.astype(BF16)
    r = x - h.astype(F32)
    m = r.astype(BF16)
    lo = (r - m.astype(F32)).astype(BF16)
    return h, m, lo


def _mm_exact_rhs(x, ones_bf16, dims=_NN):
    h, m, lo = _split3(x)
    d = functools.partial(lax.dot_general, dimension_numbers=dims, preferred_element_type=F32)
    return d(h, ones_bf16) + d(m, ones_bf16) + d(lo, ones_bf16)


def _mm_exact_lhs(ones_bf16, x, dims=_NN):
    h, m, lo = _split3(x)
    d = functools.partial(lax.dot_general, dimension_numbers=dims, preferred_element_type=F32)
    return d(ones_bf16, h) + d(ones_bf16, m) + d(ones_bf16, lo)


def _segment_matrix(width):
    r = jnp.right_shift(_iota((width, width), 0), 6)
    c = jnp.right_shift(_iota((width, width), 1), 6)
    return (r == c).astype(BF16)


def _lower_ones(c):
    return (_iota((c, c), 0) >= _iota((c, c), 1)).astype(BF16)


def _rmsnorm_rows(x, g):
    return x * lax.rsqrt(jnp.mean(x * x, axis=-1, keepdims=True) + NORM_EPS) * g


def _unit_lower_inverse(a_strict, c):
    row = _iota((c, c), 0)
    col = _iota((c, c), 1)

    def same_block(shift):
        return jnp.right_shift(row, shift) == jnp.right_shift(col, shift)

    eye = (row == col).astype(F32)
    n = jnp.where(same_block(3), -a_strict, 0.0)
    t = eye + n
    n2 = _mm3(n, n)
    t = t + _mm3(t, n2)
    n4 = _mm3(n2, n2)
    t = t + _mm3(t, n4)
    shift = 3
    while (1 << shift) < c:
        off = jnp.where(same_block(shift + 1) & jnp.logical_not(same_block(shift)), a_strict, 0.0)
        t = t - _mm3(_mm3(t, off), t)
        shift += 1
    return t


_PROJ_A_QKV = 0
_PROJ_A_Z = 768
_PROJ_B_Q = 1024
_PROJ_B_K = 1536
_PROJ_B_V = 2048
_PROJ_C = 2560
_PROJ_A_AB = 3584
_PROJ_WIDTH = 3712


def _proj_kernel(x_ref, g_ref, w_ref, cos_ref, sa_ref, sb_ref,
                 aqkv_ref, az_ref, aab_ref, q_ref, k_ref, v_ref, kb_ref, vb_ref, c_ref):
    xn = _rmsnorm_rows(x_ref[...], g_ref[...]).astype(BF16)

    def proj(c0, c1):
        return jnp.dot(xn, w_ref[:, c0:c1], preferred_element_type=F32)

    aqkv_ref[...] = proj(_PROJ_A_QKV, _PROJ_A_Z)
    az_ref[...] = proj(_PROJ_A_Z, _PROJ_B_Q)
    cos = cos_ref[...]
    sa = sa_ref[...]
    sb = sb_ref[...]

    def rope(x):
        return x * cos + pltpu.roll(x, LANES - 32, 1) * sa + pltpu.roll(x, 32, 1) * sb

    for h in range(4):
        lo, hi = h * LANES, (h + 1) * LANES
        q = rope(proj(_PROJ_B_Q + lo, _PROJ_B_Q + hi))
        q_ref[:, lo:hi] = (q * (HEAD ** -0.5)).astype(BF16)
        k = rope(proj(_PROJ_B_K + lo, _PROJ_B_K + hi))
        k_ref[:, lo:hi] = k
        kb_ref[:, lo:hi] = k.astype(BF16)
    v = proj(_PROJ_B_V, _PROJ_C)
    v_ref[...] = v
    vb_ref[...] = v.astype(BF16)
    c_ref[...] = proj(_PROJ_C, _PROJ_A_AB)
    aab_ref[...] = proj(_PROJ_A_AB, _PROJ_WIDTH)


def _input_projection(x2d, g_row, w_bf16, rope_tabs, seq_len, tm):
    t, d = x2d.shape
    nt = t // tm
    cos_t, sa_t, sb_t = rope_tabs
    if tm >= seq_len:
        reps = tm // seq_len
        cos_t, sa_t, sb_t = (jnp.tile(a, (reps, 1)) for a in (cos_t, sa_t, sb_t))
        tab_map = lambda i: (0, 0)
    else:
        per_seq = seq_len // tm
        tab_map = lambda i: (i % per_seq, 0)
    row = lambda i: (i, 0)
    fixed = lambda i: (0, 0)
    tab_spec = pl.BlockSpec((tm, LANES), tab_map)
    widths = [(768, F32), (256, F32), (128, F32), (512, BF16), (512, F32), (512, F32),
              (512, BF16), (512, BF16), (1024, F32)]
    return pl.pallas_call(
        _proj_kernel,
        grid=(nt,),
        in_specs=[pl.BlockSpec((tm, d), row), pl.BlockSpec((1, d), fixed),
                  pl.BlockSpec((d, _PROJ_WIDTH), fixed), tab_spec, tab_spec, tab_spec],
        out_specs=[pl.BlockSpec((tm, w), row) for w, _ in widths],
        out_shape=[jax.ShapeDtypeStruct((t, w), dt) for w, dt in widths],
        compiler_params=pltpu.CompilerParams(dimension_semantics=("parallel",),
                                             vmem_limit_bytes=VMEM_LIMIT),
        name="input_projection",
    )(x2d, g_row, w_bf16, cos_t, sa_t, sb_t)


def _delta_kernel(qkv_ref, z_ref, ab_ref, conv0_ref, s0_ref, convw_ref, alog_ref, dtb_ref, anorm_ref,
                  mix_ref, convn_ref, sn_ref, xp_ref, s_ref, o_ref, *, c, nl):
    l = pl.program_id(1)

    @pl.when(l == 0)
    def _():
        xp_ref[5:8, :] = conv0_ref[0]
        s_ref[...] = s0_ref[0]

    x = qkv_ref[0]
    xp_ref[8:8 + c, :] = x
    w = convw_ref[...]
    y = xp_ref[5:5 + c, :] * w[0:1]
    y = y + xp_ref[6:6 + c, :] * w[1:2]
    y = y + xp_ref[7:7 + c, :] * w[2:3]
    y = y + x * w[3:4]
    tail = xp_ref[5 + c:8 + c, :]
    xp_ref[5:8, :] = tail

    @pl.when(l == nl - 1)
    def _():
        convn_ref[0] = tail

    act = y * jax.nn.sigmoid(y)
    seg = _segment_matrix(4 * HEAD)

    def l2n(t):
        return t * lax.rsqrt(_mm_exact_rhs(t * t, seg) + L2_EPS)

    q = l2n(act[:, 0:256]) * (HEAD ** -0.5)
    k = l2n(act[:, 256:512])
    v = act[:, 512:768]

    ab = ab_ref[0]
    gl = -jnp.exp(alog_ref[...]) * jax.nn.softplus(ab + dtb_ref[...])
    beta_all = jax.nn.sigmoid(ab)
    g_all = _mm_exact_lhs(_lower_ones(c), gl)
    g_rows = _mm_exact_rhs(gl, jnp.transpose(_lower_ones(c)), _TN)

    row = _iota((c, c), 0)
    col = _iota((c, c), 1)
    lower = row >= col
    strict = row > col

    for h in range(4):
        sl = slice(h * HEAD, (h + 1) * HEAD)
        g_col = g_all[:, h:h + 1]
        g_row = g_rows[h:h + 1, :]
        beta = beta_all[:, 4 + h:5 + h]
        decay = jnp.where(lower, jnp.exp(jnp.where(lower, g_col - g_row, 0.0)), 0.0)
        kh, qh, vh = k[:, sl], q[:, sl], v[:, sl]
        kk = _mm(kh, kh, _NT)
        a_mat = jnp.where(strict, beta * kk * decay, 0.0)
        t_inv = _unit_lower_inverse(a_mat, c)
        eg = jnp.exp(g_col)
        rhs = jnp.concatenate([vh * beta, kh * (beta * eg)], axis=1)
        sol = _mm3(t_inv, rhs)
        u_val, w_k = sol[:, :HEAD], sol[:, HEAD:]
        qk = jnp.where(lower, _mm(qh, kh, _NT) * decay, 0.0)
        g_last = g_col[c - 1:c, :]
        q_dec = qh * eg
        k_tail = kh * jnp.exp(g_last - g_col)
        s = s_ref[h]
        u_new = u_val - _mm(w_k, s)
        o_ref[:, sl] = _mm(q_dec, s) + _mm(qk, u_new)
        s_ref[h] = s * jnp.exp(g_last) + _mm(k_tail, u_new, _TN)

    o = o_ref[...]
    ms = _mm_exact_rhs(o * o, seg) * (1.0 / HEAD)
    z = z_ref[0]
    ao = (o * lax.rsqrt(ms + NORM_EPS) * anorm_ref[...]) * (z * jax.nn.sigmoid(z))
    mix_ref[0] = ao.astype(BF16)

    @pl.when(l == nl - 1)
    def _():
        sn_ref[0] = s_ref[...]


def _gated_delta(a_qkv, a_z, a_ab, conv0, s0, conv_w, alog_row, dtb_row, anorm_row):
    b, seq, _ = a_qkv.shape
    c = min(CHUNK, seq)
    nl = seq // c
    tile = lambda i, l: (i, l, 0)
    per_b3 = lambda i, l: (i, 0, 0)
    per_b4 = lambda i, l: (i, 0, 0, 0)
    fixed = lambda i, l: (0, 0)
    return pl.pallas_call(
        functools.partial(_delta_kernel, c=c, nl=nl),
        grid=(b, nl),
        in_specs=[pl.BlockSpec((1, c, 768), tile), pl.BlockSpec((1, c, 256), tile),
                  pl.BlockSpec((1, c, LANES), tile), pl.BlockSpec((1, 3, 768), per_b3),
                  pl.BlockSpec((1, 4, HEAD, HEAD), per_b4), pl.BlockSpec((4, 768), fixed),
                  pl.BlockSpec((1, LANES), fixed), pl.BlockSpec((1, LANES), fixed),
                  pl.BlockSpec((1, 256), fixed)],
        out_specs=[pl.BlockSpec((1, c, 256), tile), pl.BlockSpec((1, 3, 768), per_b3),
                   pl.BlockSpec((1, 4, HEAD, HEAD), per_b4)],
        out_shape=[jax.ShapeDtypeStruct((b, seq, 256), BF16),
                   jax.ShapeDtypeStruct((b, 3, 768), F32),
                   jax.ShapeDtypeStruct((b, 4, HEAD, HEAD), F32)],
        scratch_shapes=[pltpu.VMEM((c + 8, 768), F32), pltpu.VMEM((4, HEAD, HEAD), F32),
                        pltpu.VMEM((c, 256), F32)],
        compiler_params=pltpu.CompilerParams(dimension_semantics=("parallel", "arbitrary"),
                                             vmem_limit_bytes=VMEM_LIMIT),
        name="gated_delta",
    )(a_qkv, a_z, a_ab, conv0, s0, conv_w, alog_row, dtb_row, anorm_row)


def _rwkv_kernel(c_ref, shift0_ref, s0_ref, mu_ref, w0_ref, wup_ref, a0_ref, aup_ref, gup_ref,
                 kk_ref, ka_ref, rk_ref, lnw_ref, lnb_ref,
                 mix_ref, shiftn_ref, sn_ref, xp_ref, s_ref, y_ref, *, c, nl):
    l = pl.program_id(1)

    @pl.when(l == 0)
    def _():
        xp_ref[7:8, :] = shift0_ref[0]
        s_ref[...] = s0_ref[0]

    raw = c_ref[0]
    xp_ref[8:8 + c, :] = raw
    prev = xp_ref[7:7 + c, :]
    last = raw[c - 1:c, :]
    xp_ref[7:8, :] = last

    @pl.when(l == nl - 1)
    def _():
        shiftn_ref[0] = last

    x = raw + (prev - raw) * mu_ref[...]
    cr, ck, cv = x[:, 0:256], x[:, 256:512], x[:, 512:768]
    c_wa = x[:, 768:896]
    c_g = x[:, 896:1024]
    w_log = -jnp.exp(-jax.nn.softplus(-(w0_ref[...] + _mm(jnp.tanh(c_wa), wup_ref[...])))
                     - RWKV_DECAY_OFFSET)
    ca = jax.nn.sigmoid(a0_ref[...] + _mm(c_wa, aup_ref[...]))
    cg = _mm(jax.nn.sigmoid(c_g), gup_ref[...])
    seg = _segment_matrix(4 * HEAD)
    kkv = ck * kk_ref[...]
    kk = kkv * lax.rsqrt(_mm_exact_rhs(kkv * kkv, seg) + L2_EPS)
    ck = ck * (1.0 + (ca - 1.0) * ka_ref[...])

    g_cum = _mm_exact_lhs(_lower_ones(c), w_log)
    e_pos = jnp.exp(g_cum)
    e_neg = jnp.exp(-g_cum)
    a_t = -kk * jnp.exp(g_cum - w_log)
    b_t = (kk * ca) * e_neg
    k_t = ck * e_neg
    r_t = cr * e_pos
    g_end = e_pos[c - 1:c, :]

    row = _iota((c, c), 0)
    col = _iota((c, c), 1)
    lower = row >= col
    strict = row > col

    for h in range(4):
        sl = slice(h * HEAD, (h + 1) * HEAD)
        ah, bh, kh, rh, vh = a_t[:, sl], b_t[:, sl], k_t[:, sl], r_t[:, sl], cv[:, sl]
        m_ab = jnp.where(strict, _mm3(ah, bh, _NT), 0.0)
        m_ak = jnp.where(strict, _mm3(ah, kh, _NT), 0.0)
        n_rb = jnp.where(lower, _mm3(rh, bh, _NT), 0.0)
        n_rk = jnp.where(lower, _mm3(rh, kh, _NT), 0.0)
        t_inv = _unit_lower_inverse(-m_ab, c)
        s = s_ref[h]
        u = _mm3(t_inv, _mm3(ah, s, _NT) + _mm3(m_ak, vh))
        y_ref[:, sl] = _mm3(rh, s, _NT) + _mm3(n_rb, u) + _mm3(n_rk, vh)
        s_ref[h] = (s + _mm3(u, bh, _TN) + _mm3(vh, kh, _TN)) * g_end[:, sl]

    y = y_ref[...]
    mean = _mm_exact_rhs(y, seg) * (1.0 / HEAD)
    yc = y - mean
    var = _mm_exact_rhs(yc * yc, seg) * (1.0 / HEAD)
    cy = (yc * lax.rsqrt(var + C_LN_EPS)) * lnw_ref[...] + lnb_ref[...]
    bonus = _mm_exact_rhs(cr * ck * rk_ref[...], seg) * cv
    mix_ref[0] = ((cy + bonus) * cg).astype(BF16)

    @pl.when(l == nl - 1)
    def _():
        sn_ref[0] = s_ref[...]


def _rwkv7(c_raw, shift0, s0, params):
    b, seq, width = c_raw.shape
    c = min(CHUNK, seq)
    nl = seq // c
    tile = lambda i, l: (i, l, 0)
    per_b3 = lambda i, l: (i, 0, 0)
    per_b4 = lambda i, l: (i, 0, 0, 0)
    fixed = lambda i, l: (0, 0)
    param_specs = [pl.BlockSpec(p.shape, fixed) for p in params]
    return pl.pallas_call(
        functools.partial(_rwkv_kernel, c=c, nl=nl),
        grid=(b, nl),
        in_specs=[pl.BlockSpec((1, c, width), tile), pl.BlockSpec((1, 1, width), per_b3),
                  pl.BlockSpec((1, 4, HEAD, HEAD), per_b4)] + param_specs,
        out_specs=[pl.BlockSpec((1, c, 256), tile), pl.BlockSpec((1, 1, width), per_b3),
                   pl.BlockSpec((1, 4, HEAD, HEAD), per_b4)],
        out_shape=[jax.ShapeDtypeStruct((b, seq, 256), BF16),
                   jax.ShapeDtypeStruct((b, 1, width), F32),
                   jax.ShapeDtypeStruct((b, 4, HEAD, HEAD), F32)],
        scratch_shapes=[pltpu.VMEM((c + 8, width), F32), pltpu.VMEM((4, HEAD, HEAD), F32),
                        pltpu.VMEM((c, 256), F32)],
        compiler_params=pltpu.CompilerParams(dimension_semantics=("parallel", "arbitrary"),
                                             vmem_limit_bytes=VMEM_LIMIT),
        name="rwkv7",
    )(c_raw, shift0, s0, *params)


def _attn_kernel(i_tab, j_tab, last_tab, q_ref, k_ref, v_ref, lam_ref, bn_ref, o_ref,
                 m_ref, l_ref, acc_ref, *, tq, tk, past, lk_true, lam_init):
    p = pl.program_id(1)
    i = i_tab[p]
    j = j_tab[p]

    @pl.when(j == 0)
    def _():
        m_ref[...] = jnp.full(m_ref.shape, MASK_VALUE, F32)
        l_ref[...] = jnp.zeros(l_ref.shape, F32)
        acc_ref[...] = jnp.zeros(acc_ref.shape, F32)

    q_pos = past + i * tq + _iota((tq, tk), 0)
    k_pos = j * tk + _iota((tq, tk), 1)
    visible = (jnp.right_shift(k_pos, 6) <= jnp.right_shift(q_pos, 6)) & (k_pos < lk_true)
    q = q_ref[0]
    k = k_ref[0]
    v = v_ref[0]
    for hm in range(8):
        h = hm // 2
        sl = slice(hm * HEAD, (hm + 1) * HEAD)
        s = lax.dot_general(q[:, sl], k[:, sl], _NT, preferred_element_type=F32)
        s = jnp.where(visible, s, MASK_VALUE)
        m_prev = m_ref[hm]
        m_new = jnp.maximum(m_prev, jnp.max(s, axis=1, keepdims=True))
        alpha = jnp.exp(m_prev - m_new)
        e = jnp.exp(s - m_new)
        l_ref[hm] = alpha * l_ref[hm] + jnp.sum(e, axis=1, keepdims=True)
        acc_ref[hm] = alpha * acc_ref[hm] + jnp.dot(
            e.astype(BF16), v[:, h * LANES:(h + 1) * LANES], preferred_element_type=F32)
        m_ref[hm] = m_new

    @pl.when(last_tab[p] == 1)
    def _():
        lp = lam_ref[...]
        lam = (jnp.exp(jnp.sum(lp[0:1] * lp[1:2], axis=1, keepdims=True))
               - jnp.exp(jnp.sum(lp[2:3] * lp[3:4], axis=1, keepdims=True)) + lam_init)
        for h in range(4):
            o = acc_ref[2 * h] / l_ref[2 * h] - lam * (acc_ref[2 * h + 1] / l_ref[2 * h + 1])
            o = _rmsnorm_rows(o, bn_ref[...]) * (1.0 - lam_init)
            o_ref[0, :, h * LANES:(h + 1) * LANES] = o.astype(BF16)


def _diff_attention(q, k, v, lam_params, bnorm_row, past, lam_init, tq, tk):
    b, lq, width = q.shape
    lk_true = k.shape[1]
    nk = -(-lk_true // tk)
    if nk * tk != lk_true:
        pad = ((0, 0), (0, nk * tk - lk_true), (0, 0))
        k = jnp.pad(k, pad)
        v = jnp.pad(v, pad)
    nq = lq // tq
    i_list, j_list, last_list = [], [], []
    for i in range(nq):
        last_pos = past + (i + 1) * tq - 1
        j_max = min(((last_pos // CHUNK + 1) * CHUNK - 1) // tk, nk - 1)
        for j in range(j_max + 1):
            i_list.append(i)
            j_list.append(j)
            last_list.append(1 if j == j_max else 0)
    tabs = [jnp.asarray(np.asarray(t, np.int32)) for t in (i_list, j_list, last_list)]
    grid_spec = pltpu.PrefetchScalarGridSpec(
        num_scalar_prefetch=3,
        grid=(b, len(i_list)),
        in_specs=[pl.BlockSpec((1, tq, width), lambda bi, p, it, jt, lt: (bi, it[p], 0)),
                  pl.BlockSpec((1, tk, width), lambda bi, p, it, jt, lt: (bi, jt[p], 0)),
                  pl.BlockSpec((1, tk, width), lambda bi, p, it, jt, lt: (bi, jt[p], 0)),
                  pl.BlockSpec((4, HEAD), lambda bi, p, it, jt, lt: (0, 0)),
                  pl.BlockSpec((1, LANES), lambda bi, p, it, jt, lt: (0, 0))],
        out_specs=pl.BlockSpec((1, tq, width), lambda bi, p, it, jt, lt: (bi, it[p], 0)),
        scratch_shapes=[pltpu.VMEM((8, tq, 1), F32), pltpu.VMEM((8, tq, 1), F32),
                        pltpu.VMEM((8, tq, LANES), F32)],
    )
    return pl.pallas_call(
        functools.partial(_attn_kernel, tq=tq, tk=tk, past=past, lk_true=lk_true, lam_init=lam_init),
        grid_spec=grid_spec,
        out_shape=jax.ShapeDtypeStruct((b, lq, width), BF16),
        compiler_params=pltpu.CompilerParams(dimension_semantics=("parallel", "arbitrary"),
                                             vmem_limit_bytes=VMEM_LIMIT),
        name="diff_attention",
    )(*tabs, q, k, v, lam_params, bnorm_row)


def _mlp_kernel(h_ref, ma_ref, mb_ref, mc_ref, wo_ref, nf_ref, w1_ref, w2_ref, np_ref, wg_ref,
                wp_ref, p_ref, nfin_ref, o_ref, h1_ref, xn_ref, acc_ref, *, nf, final_norm):
    kf = pl.program_id(1)

    @pl.when(kf == 0)
    def _():
        mixed = (jnp.dot(ma_ref[...], wo_ref[0:256, :], preferred_element_type=F32)
                 + jnp.dot(mb_ref[...], wo_ref[256:768, :], preferred_element_type=F32)
                 + jnp.dot(mc_ref[...], wo_ref[768:1024, :], preferred_element_type=F32))
        h1 = h_ref[...] + mixed
        h1_ref[...] = h1
        xn_ref[...] = _rmsnorm_rows(h1, nf_ref[...]).astype(BF16)
        acc_ref[...] = jnp.zeros(acc_ref.shape, F32)

    u = jnp.maximum(jnp.dot(xn_ref[...], w1_ref[...], preferred_element_type=F32), 0.0)
    acc_ref[...] += jnp.dot((u * u).astype(BF16), w2_ref[...], preferred_element_type=F32)

    @pl.when(kf == nf - 1)
    def _():
        h2 = h1_ref[...] + acc_ref[...]
        gate = jax.nn.sigmoid(jnp.dot(_rmsnorm_rows(h2, np_ref[...]).astype(BF16), wg_ref[...],
                                      preferred_element_type=F32))
        h3 = h2 + gate * jnp.dot(p_ref[...].astype(BF16), wp_ref[...], preferred_element_type=F32)
        if final_norm:
            h3 = _rmsnorm_rows(h3, nfin_ref[...])
        o_ref[...] = h3


def _mix_mlp(h2d, mix_a, mix_b, mix_c, w_out, norm_ffn, w_ff1, w_ff2, norm_ple, w_gate, w_proj,
             p2d, norm_final, final_norm, tm, tf):
    t, d = h2d.shape
    dff = w_ff1.shape[1]
    nt, nf = t // tm, dff // tf
    row = lambda i, kf: (i, 0)
    fixed = lambda i, kf: (0, 0)
    return pl.pallas_call(
        functools.partial(_mlp_kernel, nf=nf, final_norm=final_norm),
        grid=(nt, nf),
        in_specs=[pl.BlockSpec((tm, d), row), pl.BlockSpec((tm, 256), row),
                  pl.BlockSpec((tm, 512), row), pl.BlockSpec((tm, 256), row),
                  pl.BlockSpec((d, d), fixed), pl.BlockSpec((1, d), fixed),
                  pl.BlockSpec((d, tf), lambda i, kf: (0, kf)),
                  pl.BlockSpec((tf, d), lambda i, kf: (kf, 0)),
                  pl.BlockSpec((1, d), fixed), pl.BlockSpec((d, d), fixed),
                  pl.BlockSpec((p2d.shape[1], d), fixed), pl.BlockSpec((tm, p2d.shape[1]), row),
                  pl.BlockSpec((1, d), fixed)],
        out_specs=pl.BlockSpec((tm, d), row),
        out_shape=jax.ShapeDtypeStruct((t, d), F32),
        scratch_shapes=[pltpu.VMEM((tm, d), F32), pltpu.VMEM((tm, d), BF16), pltpu.VMEM((tm, d), F32)],
        compiler_params=pltpu.CompilerParams(dimension_semantics=("parallel", "arbitrary"),
                                             vmem_limit_bytes=VMEM_LIMIT),
        name="mix_mlp",
    )(h2d, mix_a, mix_b, mix_c, w_out, norm_ffn, w_ff1, w_ff2, norm_ple, w_gate, w_proj, p2d,
      norm_final)


def _rope_tables(past, seq_len):
    half = HEAD // 2
    inv = ROPE_THETA ** (-2.0 * jnp.arange(half, dtype=F32) / HEAD)
    pos = past + jnp.arange(seq_len, dtype=jnp.int32)
    ang = pos.astype(F32)[:, None] * inv[None, :]
    cos, sin, zero = jnp.cos(ang), jnp.sin(ang), jnp.zeros_like(ang)
    rep = LANES // HEAD
    cos_t = jnp.tile(jnp.concatenate([cos, cos], axis=1), (1, rep))
    sa_t = jnp.tile(jnp.concatenate([-sin, zero], axis=1), (1, rep))
    sb_t = jnp.tile(jnp.concatenate([zero, sin], axis=1), (1, rep))
    return cos_t, sa_t, sb_t


def _pad_lanes(v, width=LANES):
    return jnp.pad(v, (0, width - v.shape[0]))[None, :]


def _prep_layer(i, norm_mix, w_in, a_conv_w, a_A_log, a_dt_bias, a_norm,
                b_lam_q1, b_lam_k1, b_lam_q2, b_lam_k2, b_norm,
                c_mu, c_w0, c_w_up, c_a0, c_a_up, c_g_up, c_k_k, c_k_a, c_r_k, c_ln_w, c_ln_b,
                w_out, norm_ffn, w_ff1, w_ff2, norm_ple, w_ple_gate, w_ple_proj):
    w = w_in[i]
    d = w.shape[0]
    w_perm = jnp.concatenate(
        [w[:, 0:1024], w[:, 1032:3592], w[:, 1024:1032],
         jnp.zeros((d, _PROJ_WIDTH - 3592), w.dtype)], axis=1).astype(BF16)
    zeros_rank = jnp.zeros_like(c_w_up[i])
    rwkv = (c_mu[i][None, :], c_w0[i][None, :],
            jnp.concatenate([c_w_up[i], zeros_rank], axis=0).astype(BF16), c_a0[i][None, :],
            jnp.concatenate([jnp.zeros_like(c_a_up[i]), c_a_up[i]], axis=0).astype(BF16),
            c_g_up[i].astype(BF16), c_k_k[i][None, :], c_k_a[i][None, :],
            c_r_k[i].reshape(1, -1), c_ln_w[i][None, :], c_ln_b[i][None, :])
    return dict(
        norm_mix=norm_mix[i][None, :], w_in=w_perm, conv_w=a_conv_w[i],
        alog=_pad_lanes(a_A_log[i]), dtb=_pad_lanes(a_dt_bias[i]),
        anorm=jnp.tile(a_norm[i], 4)[None, :],
        lam=jnp.stack([b_lam_q1[i], b_lam_k1[i], b_lam_q2[i], b_lam_k2[i]], axis=0),
        bnorm=b_norm[i][None, :], rwkv=rwkv,
        w_out=w_out[i].astype(BF16), norm_ffn=norm_ffn[i][None, :],
        w_ff1=w_ff1[i].astype(BF16), w_ff2=w_ff2[i].astype(BF16),
        norm_ple=norm_ple[i][None, :], w_gate=w_ple_gate[i].astype(BF16),
        w_proj=w_ple_proj[i].astype(BF16))


def _pick_tile(n, target):
    t = min(n, target)
    while n % t:
        t //= 2
    return t


def _trunk(x, p, cache_k, cache_v, conv_buf, delta_s, shift_prev, wkv_s, layers, norm_final):
    b, seq, d = x.shape
    depth = len(layers)
    past = cache_k.shape[2]
    t = b * seq
    tm = _pick_tile(t, 512)
    rope_tabs = _rope_tables(past, seq)
    h = x.reshape(t, d)
    states = []
    for i, lp in enumerate(layers):
        a_qkv, a_z, a_ab, q_b, k_f, v_f, k_b, v_b, c_raw = _input_projection(
            h, lp["norm_mix"], lp["w_in"], rope_tabs, seq, tm)
        mix_a, conv_n, delta_n = _gated_delta(
            a_qkv.reshape(b, seq, -1), a_z.reshape(b, seq, -1), a_ab.reshape(b, seq, -1),
            conv_buf[i], delta_s[i], lp["conv_w"], lp["alog"], lp["dtb"], lp["anorm"])
        k_all = k_b.reshape(b, seq, -1)
        v_all = v_b.reshape(b, seq, -1)
        if past:
            k_all = jnp.concatenate([cache_k[i].reshape(b, past, -1).astype(BF16), k_all], axis=1)
            v_all = jnp.concatenate([cache_v[i].reshape(b, past, -1).astype(BF16), v_all], axis=1)
        lam_init = 0.8 - 0.6 * math.exp(-0.3 * i)
        mix_b = _diff_attention(q_b.reshape(b, seq, -1), k_all, v_all, lp["lam"], lp["bnorm"],
                                past, lam_init, _pick_tile(seq, ATTN_TQ),
                                ATTN_TK if past + seq >= 8 * ATTN_TK else ATTN_TK // 2)
        mix_c, shift_n, wkv_n = _rwkv7(c_raw.reshape(b, seq, -1), shift_prev[i][:, None, :],
                                       wkv_s[i], lp["rwkv"])
        h = _mix_mlp(h, mix_a.reshape(t, -1), mix_b.reshape(t, -1), mix_c.reshape(t, -1),
                     lp["w_out"], lp["norm_ffn"], lp["w_ff1"], lp["w_ff2"], lp["norm_ple"],
                     lp["w_gate"], lp["w_proj"], p[i].reshape(t, -1), norm_final[None, :],
                     i == depth - 1, tm, 1024)
        states.append((conv_n, delta_n, k_f.reshape(b, seq, 4, LANES), v_f.reshape(b, seq, 4, LANES),
                       shift_n[:, 0, :], wkv_n))
    new_state = [jnp.stack([st[j] for st in states], axis=0) for j in range(6)]
    return h.reshape(b, seq, d), new_state


def kernel(x_prompt, x_sample, cache_b_k, cache_b_v, state_a_conv, state_a_delta, state_c_shift, state_c_wkv, p_prompt, p_sample, norm_mix, w_in, a_conv_w, a_A_log, a_dt_bias, a_norm, b_lam_q1, b_lam_k1, b_lam_q2, b_lam_k2, b_norm, c_mu, c_w0, c_w_up, c_a0, c_a_up, c_g_up, c_k_k, c_k_a, c_r_k, c_ln_w, c_ln_b, w_out, norm_ffn, w_ff1, w_ff2, norm_ple, w_ple_gate, w_ple_proj, norm_final):
    depth = w_in.shape[0]
    layers = [_prep_layer(i, norm_mix, w_in, a_conv_w, a_A_log, a_dt_bias, a_norm,
                          b_lam_q1, b_lam_k1, b_lam_q2, b_lam_k2, b_norm,
                          c_mu, c_w0, c_w_up, c_a0, c_a_up, c_g_up, c_k_k, c_k_a, c_r_k, c_ln_w, c_ln_b,
                          w_out, norm_ffn, w_ff1, w_ff2, norm_ple, w_ple_gate, w_ple_proj)
              for i in range(depth)]
    bp = x_prompt.shape[0]
    dt = x_prompt.dtype
    zeros = lambda ref: jnp.zeros((depth, bp) + ref.shape[2:], dt)
    empty_k = jnp.zeros((depth, bp, 0) + cache_b_k.shape[3:], dt)
    empty_v = jnp.zeros((depth, bp, 0) + cache_b_v.shape[3:], dt)
    y_prompt, st_p = _trunk(x_prompt, p_prompt, empty_k, empty_v, zeros(state_a_conv),
                            zeros(state_a_delta), zeros(state_c_shift), zeros(state_c_wkv),
                            layers, norm_final)
    y_sample, st_s = _trunk(x_sample, p_sample, cache_b_k, cache_b_v, state_a_conv, state_a_delta,
                            state_c_shift, state_c_wkv, layers, norm_final)
    return (y_prompt, y_sample, *st_p, *st_s)
```

```python
import functools
import math

import numpy as np
import jax
import jax.numpy as jnp
from jax import lax
from jax.experimental import pallas as pl
from jax.experimental.pallas import tpu as pltpu

F32 = jnp.float32
BF16 = jnp.bfloat16

CHUNK = 64
ROPE_THETA = 10000.0
NORM_EPS = 1e-6
L2_EPS = 1e-6
C_LN_EPS = 64e-5
RWKV_DECAY_OFFSET = 0.5
MASK_VALUE = float(np.finfo(np.float32).min)

LANES = 128
HEAD = 64
VMEM_LIMIT = 56 * 1024 * 1024
CHUNKS_PER_STEP = 4
ATTN_TQ = 512
ATTN_TK = 512
ATTN_Q_SCALE = (64 ** -0.5) * math.log2(math.e)


def _iota(shape, axis):
    return lax.broadcasted_iota(jnp.int32, shape, axis)


_NN = (((1,), (0,)), ((), ()))
_NT = (((1,), (1,)), ((), ()))
_TN = (((0,), (0,)), ((), ()))


def _mm(a, b, dims=_NN):
    return lax.dot_general(a.astype(BF16), b.astype(BF16), dims, preferred_element_type=F32)


def _split3(x):
    h = x.astype(BF16)
    r = x - h.astype(F32)
    m = r.astype(BF16)
    lo = (r - m.astype(F32)).astype(BF16)
    return h, m, lo


def _mm_exact_rhs(x, ones_bf16, dims=_NN):
    h, m, lo = _split3(x)
    d = functools.partial(lax.dot_general, dimension_numbers=dims, preferred_element_type=F32)
    return d(h, ones_bf16) + d(m, ones_bf16) + d(lo, ones_bf16)


def _mm_exact_lhs(ones_bf16, x, dims=_NN):
    h, m, lo = _split3(x)
    d = functools.partial(lax.dot_general, dimension_numbers=dims, preferred_element_type=F32)
    return d(ones_bf16, h) + d(ones_bf16, m) + d(ones_bf16, lo)


def _segment_matrix(width):
    r = jnp.right_shift(_iota((width, width), 0), 6)
    c = jnp.right_shift(_iota((width, width), 1), 6)
    return (r == c).astype(BF16)


def _chunk_tri_ones(n, c, upper):
    row = _iota((n, n), 0)
    col = _iota((n, n), 1)
    shift = c.bit_length() - 1
    same = jnp.right_shift(row, shift) == jnp.right_shift(col, shift)
    tri = (row <= col) if upper else (row >= col)
    return (same & tri).astype(BF16)


def _chunking(seq):
    c = min(CHUNK, seq)
    assert c & (c - 1) == 0 and seq % c == 0, seq
    g = min(CHUNKS_PER_STEP, seq // c)
    while (seq // c) % g:
        g -= 1
    return c, g


def _rmsnorm_rows(x, g):
    return x * lax.rsqrt(jnp.mean(x * x, axis=-1, keepdims=True) + NORM_EPS) * g


def _unit_lower_inverses(a_list, c):
    row = _iota((c, c), 0)
    col = _iota((c, c), 1)

    def same_block(shift):
        return jnp.right_shift(row, shift) == jnp.right_shift(col, shift)

    eye = (row == col).astype(F32)
    leaf = same_block(3)
    ns = [jnp.where(leaf, -a, 0.0) for a in a_list]
    ts = [eye + n for n in ns]
    n2s = [_mm(n, n) for n in ns]
    ts = [t + _mm(t, n2) for t, n2 in zip(ts, n2s)]
    n4s = [_mm(n2, n2) for n2 in n2s]
    ts = [t + _mm(t, n4) for t, n4 in zip(ts, n4s)]
    shift = 3
    while (1 << shift) < c:
        off_mask = same_block(shift + 1) & jnp.logical_not(same_block(shift))
        tos = [_mm(t, jnp.where(off_mask, a, 0.0)) for t, a in zip(ts, a_list)]
        ts = [t - _mm(to, t) for t, to in zip(ts, tos)]
        shift += 1
    return ts


_PROJ_A_QKV = 0
_PROJ_A_Z = 768
_PROJ_B_Q = 1024
_PROJ_B_K = 1536
_PROJ_B_V = 2048
_PROJ_C = 2560
_PROJ_A_AB = 3584
_PROJ_WIDTH = 3712


def _proj_kernel(x_ref, g_ref, w_ref, cos_ref, sa_ref, sb_ref,
                 aqkv_ref, az_ref, aab_ref, q_ref, k_ref, v_ref, kb_ref, vb_ref, c_ref):
    xn = _rmsnorm_rows(x_ref[...], g_ref[...]).astype(BF16)

    def proj(c0, c1):
        return jnp.dot(xn, w_ref[:, c0:c1], preferred_element_type=F32)

    aqkv_ref[...] = proj(_PROJ_A_QKV, _PROJ_A_Z)
    az_ref[...] = proj(_PROJ_A_Z, _PROJ_B_Q)
    cos = cos_ref[...]
    sa = sa_ref[...]
    sb = sb_ref[...]

    def rope(x):
        return x * cos + pltpu.roll(x, LANES - 32, 1) * sa + pltpu.roll(x, 32, 1) * sb

    for h in range(4):
        lo, hi = h * LANES, (h + 1) * LANES
        q = rope(proj(_PROJ_B_Q + lo, _PROJ_B_Q + hi))
        q_ref[:, lo:hi] = (q * ATTN_Q_SCALE).astype(BF16)
        k = rope(proj(_PROJ_B_K + lo, _PROJ_B_K + hi))
        k_ref[:, lo:hi] = k
        kb_ref[:, lo:hi] = k.astype(BF16)
    v = proj(_PROJ_B_V, _PROJ_C)
    v_ref[...] = v
    vb_ref[...] = v.astype(BF16)
    c_ref[...] = proj(_PROJ_C, _PROJ_A_AB)
    aab_ref[...] = proj(_PROJ_A_AB, _PROJ_WIDTH)


def _input_projection(x2d, g_row, w_bf16, rope_tabs, seq_len, tm):
    t, d = x2d.shape
    nt = t // tm
    cos_t, sa_t, sb_t = rope_tabs
    if tm >= seq_len:
        reps = tm // seq_len
        cos_t, sa_t, sb_t = (jnp.tile(a, (reps, 1)) for a in (cos_t, sa_t, sb_t))
        tab_map = lambda i: (0, 0)
    else:
        per_seq = seq_len // tm
        tab_map = lambda i: (i % per_seq, 0)
    row = lambda i: (i, 0)
    fixed = lambda i: (0, 0)
    tab_spec = pl.BlockSpec((tm, LANES), tab_map)
    widths = [(768, F32), (256, F32), (128, F32), (512, BF16), (512, F32), (512, F32),
              (512, BF16), (512, BF16), (1024, F32)]
    return pl.pallas_call(
        _proj_kernel,
        grid=(nt,),
        in_specs=[pl.BlockSpec((tm, d), row), pl.BlockSpec((1, d), fixed),
                  pl.BlockSpec((d, _PROJ_WIDTH), fixed), tab_spec, tab_spec, tab_spec],
        out_specs=[pl.BlockSpec((tm, w), row) for w, _ in widths],
        out_shape=[jax.ShapeDtypeStruct((t, w), dt) for w, dt in widths],
        compiler_params=pltpu.CompilerParams(dimension_semantics=("parallel",),
                                             vmem_limit_bytes=VMEM_LIMIT),
        name="input_projection",
    )(x2d, g_row, w_bf16, cos_t, sa_t, sb_t)


def _delta_kernel(qkv_ref, z_ref, ab_ref, conv0_ref, s0_ref, convw_ref, alog_ref, dtb_ref, anorm_ref,
                  mix_ref, convn_ref, sn_ref, xp_ref, s_ref, o_ref, *, c, g, nl):
    l = pl.program_id(1)
    tl = c * g

    @pl.when(l == 0)
    def _():
        xp_ref[5:8, :] = conv0_ref[0]
        s_ref[...] = s0_ref[0]

    x = qkv_ref[0]
    xp_ref[8:8 + tl, :] = x
    w = convw_ref[...]
    y = xp_ref[5:5 + tl, :] * w[0:1]
    y = y + xp_ref[6:6 + tl, :] * w[1:2]
    y = y + xp_ref[7:7 + tl, :] * w[2:3]
    y = y + x * w[3:4]
    tail = xp_ref[5 + tl:8 + tl, :]
    xp_ref[5:8, :] = tail

    @pl.when(l == nl - 1)
    def _():
        convn_ref[0] = tail

    act = y * jax.nn.sigmoid(y)
    seg = _segment_matrix(4 * HEAD)

    def l2n(t):
        return t * lax.rsqrt(_mm_exact_rhs(t * t, seg) + L2_EPS)

    q = l2n(act[:, 0:256]) * (HEAD ** -0.5)
    k = l2n(act[:, 256:512])
    v = act[:, 512:768]

    ab = ab_ref[0]
    gl = -jnp.exp(alog_ref[...]) * jax.nn.softplus(ab + dtb_ref[...])
    beta_all = jax.nn.sigmoid(ab)
    g_all = _mm_exact_lhs(_chunk_tri_ones(tl, c, False), gl)
    g_rows = _mm_exact_rhs(gl, _chunk_tri_ones(tl, c, True), _TN)

    row = _iota((c, c), 0)
    col = _iota((c, c), 1)
    lower = row >= col
    strict = row > col

    probs = [(slice(ci * c, (ci + 1) * c), h) for ci in range(g) for h in range(4)]
    heads = lambda h: slice(h * HEAD, (h + 1) * HEAD)
    g_col = [g_all[rows, h:h + 1] for rows, h in probs]
    beta = [beta_all[rows, 4 + h:5 + h] for rows, h in probs]
    decay = [jnp.where(lower, jnp.exp(jnp.where(lower, gc - g_rows[h:h + 1, rows], 0.0)), 0.0)
             for gc, (rows, h) in zip(g_col, probs)]
    kh = [k[rows, heads(h)] for rows, h in probs]
    qh = [q[rows, heads(h)] for rows, h in probs]
    vh = [v[rows, heads(h)] for rows, h in probs]
    kq = [_mm(jnp.concatenate([kk_, qq_], axis=0), kk_, _NT) for kk_, qq_ in zip(kh, qh)]
    t_inv = _unit_lower_inverses(
        [jnp.where(strict, b * x[:c] * d, 0.0) for b, x, d in zip(beta, kq, decay)], c)
    eg = [jnp.exp(gc) for gc in g_col]
    sol = [_mm(t, jnp.concatenate([vv * b, kk_ * (b * e)], axis=1))
           for t, vv, kk_, b, e in zip(t_inv, vh, kh, beta, eg)]
    qk = [jnp.where(lower, x[c:] * d, 0.0) for x, d in zip(kq, decay)]
    g_last = [gc[c - 1:c, :] for gc in g_col]
    wq = [jnp.concatenate([so[:, HEAD:], qq_ * e], axis=0) for so, qq_, e in zip(sol, qh, eg)]
    k_tail = [kk_ * jnp.exp(gl_ - gc) for kk_, gl_, gc in zip(kh, g_last, g_col)]
    decay_end = [jnp.exp(gl_) for gl_ in g_last]

    states = [s_ref[h] for h in range(4)]
    for ci in range(g):
        rows = slice(ci * c, (ci + 1) * c)
        idx = [ci * 4 + h for h in range(4)]
        ws = [_mm(wq[n], states[h]) for h, n in enumerate(idx)]
        u_new = [sol[n][:, :HEAD] - ws[h][:c] for h, n in enumerate(idx)]
        o_new = [ws[h][c:] + _mm(qk[n], u_new[h]) for h, n in enumerate(idx)]
        states = [states[h] * decay_end[n] + _mm(k_tail[n], u_new[h], _TN) for h, n in enumerate(idx)]
        for h in range(4):
            o_ref[rows, heads(h)] = o_new[h]
    for h in range(4):
        s_ref[h] = states[h]

    o = o_ref[...]
    ms = _mm_exact_rhs(o * o, seg) * (1.0 / HEAD)
    z = z_ref[0]
    ao = (o * lax.rsqrt(ms + NORM_EPS) * anorm_ref[...]) * (z * jax.nn.sigmoid(z))
    mix_ref[0] = ao.astype(BF16)

    @pl.when(l == nl - 1)
    def _():
        sn_ref[0] = s_ref[...]


def _gated_delta(a_qkv, a_z, a_ab, conv0, s0, conv_w, alog_row, dtb_row, anorm_row):
    b, seq, _ = a_qkv.shape
    c, g = _chunking(seq)
    tl = c * g
    nl = seq // tl
    tile = lambda i, l: (i, l, 0)
    per_b3 = lambda i, l: (i, 0, 0)
    per_b4 = lambda i, l: (i, 0, 0, 0)
    fixed = lambda i, l: (0, 0)
    return pl.pallas_call(
        functools.partial(_delta_kernel, c=c, g=g, nl=nl),
        grid=(b, nl),
        in_specs=[pl.BlockSpec((1, tl, 768), tile), pl.BlockSpec((1, tl, 256), tile),
                  pl.BlockSpec((1, tl, LANES), tile), pl.BlockSpec((1, 3, 768), per_b3),
                  pl.BlockSpec((1, 4, HEAD, HEAD), per_b4), pl.BlockSpec((4, 768), fixed),
                  pl.BlockSpec((1, LANES), fixed), pl.BlockSpec((1, LANES), fixed),
                  pl.BlockSpec((1, 256), fixed)],
        out_specs=[pl.BlockSpec((1, tl, 256), tile), pl.BlockSpec((1, 3, 768), per_b3),
                   pl.BlockSpec((1, 4, HEAD, HEAD), per_b4)],
        out_shape=[jax.ShapeDtypeStruct((b, seq, 256), BF16),
                   jax.ShapeDtypeStruct((b, 3, 768), F32),
                   jax.ShapeDtypeStruct((b, 4, HEAD, HEAD), F32)],
        scratch_shapes=[pltpu.VMEM((tl + 8, 768), F32), pltpu.VMEM((4, HEAD, HEAD), F32),
                        pltpu.VMEM((tl, 256), F32)],
        compiler_params=pltpu.CompilerParams(dimension_semantics=("parallel", "arbitrary"),
                                             vmem_limit_bytes=VMEM_LIMIT),
        name="gated_delta",
    )(a_qkv, a_z, a_ab, conv0, s0, conv_w, alog_row, dtb_row, anorm_row)


def _rwkv_kernel(c_ref, shift0_ref, s0_ref, mu_ref, w0_ref, wup_ref, a0_ref, aup_ref, gup_ref,
                 kk_ref, ka_ref, rk_ref, lnw_ref, lnb_ref,
                 mix_ref, shiftn_ref, sn_ref, xp_ref, s_ref, y_ref, *, c, g, nl):
    l = pl.program_id(1)
    tl = c * g

    @pl.when(l == 0)
    def _():
        xp_ref[7:8, :] = shift0_ref[0]
        s_ref[...] = s0_ref[0]

    raw = c_ref[0]
    xp_ref[8:8 + tl, :] = raw
    prev = xp_ref[7:7 + tl, :]
    last = raw[tl - 1:tl, :]
    xp_ref[7:8, :] = last

    @pl.when(l == nl - 1)
    def _():
        shiftn_ref[0] = last

    x = raw + (prev - raw) * mu_ref[...]
    cr, ck, cv = x[:, 0:256], x[:, 256:512], x[:, 512:768]
    c_wa = x[:, 768:896]
    c_g = x[:, 896:1024]
    w_log = -jnp.exp(-jax.nn.softplus(-(w0_ref[...] + _mm(jnp.tanh(c_wa), wup_ref[...])))
                     - RWKV_DECAY_OFFSET)
    ca = jax.nn.sigmoid(a0_ref[...] + _mm(c_wa, aup_ref[...]))
    cg = _mm(jax.nn.sigmoid(c_g), gup_ref[...])
    seg = _segment_matrix(4 * HEAD)
    kkv = ck * kk_ref[...]
    kk = kkv * lax.rsqrt(_mm_exact_rhs(kkv * kkv, seg) + L2_EPS)
    ck = ck * (1.0 + (ca - 1.0) * ka_ref[...])

    g_cum = _mm_exact_lhs(_chunk_tri_ones(tl, c, False), w_log)
    e_pos = jnp.exp(g_cum)
    e_neg = jnp.exp(-g_cum)
    a_t = -kk * jnp.exp(g_cum - w_log)
    b_t = (kk * ca) * e_neg
    k_t = ck * e_neg
    r_t = cr * e_pos

    row = _iota((c, c), 0)
    col = _iota((c, c), 1)
    lower = row >= col
    strict = row > col

    probs = [(slice(ci * c, (ci + 1) * c), h) for ci in range(g) for h in range(4)]
    heads = lambda h: slice(h * HEAD, (h + 1) * HEAD)
    bh = [b_t[rows, heads(h)] for rows, h in probs]
    kh = [k_t[rows, heads(h)] for rows, h in probs]
    vh = [cv[rows, heads(h)] for rows, h in probs]
    ar = [jnp.concatenate([a_t[rows, heads(h)], r_t[rows, heads(h)]], axis=0) for rows, h in probs]
    pb = [_mm(x, y_, _NT) for x, y_ in zip(ar, bh)]
    pk = [_mm(x, y_, _NT) for x, y_ in zip(ar, kh)]
    t_inv = _unit_lower_inverses([-jnp.where(strict, x[:c], 0.0) for x in pb], c)
    n_rb = [jnp.where(lower, x[c:], 0.0) for x in pb]
    mn = [_mm(jnp.concatenate([jnp.where(strict, x[:c], 0.0), jnp.where(lower, x[c:], 0.0)], axis=0), vv)
          for x, vv in zip(pk, vh)]
    bk = [jnp.concatenate([x, y_], axis=0) for x, y_ in zip(bh, kh)]
    g_end = [e_pos[rows.stop - 1:rows.stop, heads(h)] for rows, h in probs]

    states = [s_ref[h] for h in range(4)]
    for ci in range(g):
        rows = slice(ci * c, (ci + 1) * c)
        idx = [ci * 4 + h for h in range(4)]
        ars = [_mm(ar[n], states[h], _NT) for h, n in enumerate(idx)]
        u = [_mm(t_inv[n], ars[h][:c] + mn[n][:c]) for h, n in enumerate(idx)]
        y_new = [ars[h][c:] + _mm(n_rb[n], u[h]) + mn[n][c:] for h, n in enumerate(idx)]
        states = [(states[h] + _mm(jnp.concatenate([u[h], vh[n]], axis=0), bk[n], _TN)) * g_end[n]
                  for h, n in enumerate(idx)]
        for h in range(4):
            y_ref[rows, heads(h)] = y_new[h]
    for h in range(4):
        s_ref[h] = states[h]

    y = y_ref[...]
    mean = _mm_exact_rhs(y, seg) * (1.0 / HEAD)
    yc = y - mean
    var = _mm_exact_rhs(yc * yc, seg) * (1.0 / HEAD)
    cy = (yc * lax.rsqrt(var + C_LN_EPS)) * lnw_ref[...] + lnb_ref[...]
    bonus = _mm_exact_rhs(cr * ck * rk_ref[...], seg) * cv
    mix_ref[0] = ((cy + bonus) * cg).astype(BF16)

    @pl.when(l == nl - 1)
    def _():
        sn_ref[0] = s_ref[...]


def _rwkv7(c_raw, shift0, s0, params):
    b, seq, width = c_raw.shape
    c, g = _chunking(seq)
    tl = c * g
    nl = seq // tl
    tile = lambda i, l: (i, l, 0)
    per_b3 = lambda i, l: (i, 0, 0)
    per_b4 = lambda i, l: (i, 0, 0, 0)
    fixed = lambda i, l: (0, 0)
    param_specs = [pl.BlockSpec(p.shape, fixed) for p in params]
    return pl.pallas_call(
        functools.partial(_rwkv_kernel, c=c, g=g, nl=nl),
        grid=(b, nl),
        in_specs=[pl.BlockSpec((1, tl, width), tile), pl.BlockSpec((1, 1, width), per_b3),
                  pl.BlockSpec((1, 4, HEAD, HEAD), per_b4)] + param_specs,
        out_specs=[pl.BlockSpec((1, tl, 256), tile), pl.BlockSpec((1, 1, width), per_b3),
                   pl.BlockSpec((1, 4, HEAD, HEAD), per_b4)],
        out_shape=[jax.ShapeDtypeStruct((b, seq, 256), BF16),
                   jax.ShapeDtypeStruct((b, 1, width), F32),
                   jax.ShapeDtypeStruct((b, 4, HEAD, HEAD), F32)],
        scratch_shapes=[pltpu.VMEM((tl + 8, width), F32), pltpu.VMEM((4, HEAD, HEAD), F32),
                        pltpu.VMEM((tl, 256), F32)],
        compiler_params=pltpu.CompilerParams(dimension_semantics=("parallel", "arbitrary"),
                                             vmem_limit_bytes=VMEM_LIMIT),
        name="rwkv7",
    )(c_raw, shift0, s0, *params)


def _attn_kernel(i_tab, j_tab, flag_tab, q_ref, k_ref, v_ref, lam_ref, bn_ref, o_ref,
                 m_ref, l_ref, acc_ref, *, tq, tk, past, lk_true, lam_init):
    p = pl.program_id(1)
    i = i_tab[p]
    j = j_tab[p]

    @pl.when(j == 0)
    def _():
        m_ref[...] = jnp.full(m_ref.shape, MASK_VALUE, F32)
        l_ref[...] = jnp.zeros(l_ref.shape, F32)
        acc_ref[...] = jnp.zeros(acc_ref.shape, F32)

    nslab = tk // LANES

    def step(masked):
        if masked:
            q_pos = past + i * tq + _iota((tq, tk), 0)
            k_pos = j * tk + _iota((tq, tk), 1)
            visible = (jnp.right_shift(k_pos, 6) <= jnp.right_shift(q_pos, 6)) & (k_pos < lk_true)
        q = q_ref[0]
        k = k_ref[0]
        v = v_ref[0]
        for hm in range(8):
            h = hm // 2
            sl = slice(hm * HEAD, (hm + 1) * HEAD)
            s = lax.dot_general(q[:, sl], k[:, sl], _NT, preferred_element_type=F32)
            if masked:
                s = jnp.where(visible, s, MASK_VALUE)
            slabs = [s[:, n * LANES:(n + 1) * LANES] for n in range(nslab)]
            m_cur = functools.reduce(jnp.maximum, slabs)
            m_prev = m_ref[hm]
            m_new = jnp.maximum(m_prev, jnp.max(m_cur, axis=1, keepdims=True))
            alpha = jnp.exp2(m_prev - m_new)
            es = [jnp.exp2(sb - m_new) for sb in slabs]
            l_ref[hm] = alpha * l_ref[hm] + functools.reduce(lambda a, b: a + b, es)
            e = jnp.concatenate([x.astype(BF16) for x in es], axis=1)
            acc_ref[hm] = alpha * acc_ref[hm] + jnp.dot(
                e, v[:, h * LANES:(h + 1) * LANES], preferred_element_type=F32)
            m_ref[hm] = m_new

    needs_mask = jnp.bitwise_and(flag_tab[p], 2) != 0

    @pl.when(needs_mask)
    def _():
        step(True)

    @pl.when(jnp.logical_not(needs_mask))
    def _():
        step(False)

    @pl.when(jnp.bitwise_and(flag_tab[p], 1) != 0)
    def _():
        lp = lam_ref[...]
        lam = (jnp.exp(jnp.sum(lp[0:1] * lp[1:2], axis=1, keepdims=True))
               - jnp.exp(jnp.sum(lp[2:3] * lp[3:4], axis=1, keepdims=True)) + lam_init)
        for h in range(4):
            l1 = jnp.sum(l_ref[2 * h], axis=1, keepdims=True)
            l2 = jnp.sum(l_ref[2 * h + 1], axis=1, keepdims=True)
            o = acc_ref[2 * h] / l1 - lam * (acc_ref[2 * h + 1] / l2)
            o = _rmsnorm_rows(o, bn_ref[...]) * (1.0 - lam_init)
            o_ref[0, :, h * LANES:(h + 1) * LANES] = o.astype(BF16)


def _diff_attention(q, k, v, lam_params, bnorm_row, past, lam_init, tq, tk):
    b, lq, width = q.shape
    lk_true = k.shape[1]
    nk = -(-lk_true // tk)
    if nk * tk != lk_true:
        pad = ((0, 0), (0, nk * tk - lk_true), (0, 0))
        k = jnp.pad(k, pad)
        v = jnp.pad(v, pad)
    nq = lq // tq
    i_list, j_list, flag_list = [], [], []
    for i in range(nq):
        first_pos = past + i * tq
        last_pos = first_pos + tq - 1
        j_max = min(((last_pos // CHUNK + 1) * CHUNK - 1) // tk, nk - 1)
        for j in range(j_max + 1):
            key_end = (j + 1) * tk
            all_visible = (key_end - 1) // CHUNK <= first_pos // CHUNK and key_end <= lk_true
            i_list.append(i)
            j_list.append(j)
            flag_list.append((1 if j == j_max else 0) + (0 if all_visible else 2))
    tabs = [jnp.asarray(np.asarray(t, np.int32)) for t in (i_list, j_list, flag_list)]
    grid_spec = pltpu.PrefetchScalarGridSpec(
        num_scalar_prefetch=3,
        grid=(b, len(i_list)),
        in_specs=[pl.BlockSpec((1, tq, width), lambda bi, p, it, jt, lt: (bi, it[p], 0)),
                  pl.BlockSpec((1, tk, width), lambda bi, p, it, jt, lt: (bi, jt[p], 0)),
                  pl.BlockSpec((1, tk, width), lambda bi, p, it, jt, lt: (bi, jt[p], 0)),
                  pl.BlockSpec((4, HEAD), lambda bi, p, it, jt, lt: (0, 0)),
                  pl.BlockSpec((1, LANES), lambda bi, p, it, jt, lt: (0, 0))],
        out_specs=pl.BlockSpec((1, tq, width), lambda bi, p, it, jt, lt: (bi, it[p], 0)),
        scratch_shapes=[pltpu.VMEM((8, tq, LANES), F32), pltpu.VMEM((8, tq, LANES), F32),
                        pltpu.VMEM((8, tq, LANES), F32)],
    )
    return pl.pallas_call(
        functools.partial(_attn_kernel, tq=tq, tk=tk, past=past, lk_true=lk_true, lam_init=lam_init),
        grid_spec=grid_spec,
        out_shape=jax.ShapeDtypeStruct((b, lq, width), BF16),
        compiler_params=pltpu.CompilerParams(dimension_semantics=("parallel", "arbitrary"),
                                             vmem_limit_bytes=VMEM_LIMIT),
        name="diff_attention",
    )(*tabs, q, k, v, lam_params, bnorm_row)


def _mlp_kernel(h_ref, ma_ref, mb_ref, mc_ref, wo_ref, nf_ref, w1_ref, w2_ref, np_ref, wg_ref,
                wp_ref, p_ref, nfin_ref, o_ref, h1_ref, xn_ref, acc_ref, *, nf, final_norm):
    kf = pl.program_id(1)

    @pl.when(kf == 0)
    def _():
        mixed = (jnp.dot(ma_ref[...], wo_ref[0:256, :], preferred_element_type=F32)
                 + jnp.dot(mb_ref[...], wo_ref[256:768, :], preferred_element_type=F32)
                 + jnp.dot(mc_ref[...], wo_ref[768:1024, :], preferred_element_type=F32))
        h1 = h_ref[...] + mixed
        h1_ref[...] = h1
        xn_ref[...] = _rmsnorm_rows(h1, nf_ref[...]).astype(BF16)
        acc_ref[...] = jnp.zeros(acc_ref.shape, F32)

    u = jnp.maximum(jnp.dot(xn_ref[...], w1_ref[...], preferred_element_type=F32), 0.0)
    acc_ref[...] += jnp.dot((u * u).astype(BF16), w2_ref[...], preferred_element_type=F32)

    @pl.when(kf == nf - 1)
    def _():
        h2 = h1_ref[...] + acc_ref[...]
        gate = jax.nn.sigmoid(jnp.dot(_rmsnorm_rows(h2, np_ref[...]).astype(BF16), wg_ref[...],
                                      preferred_element_type=F32))
        h3 = h2 + gate * jnp.dot(p_ref[...].astype(BF16), wp_ref[...], preferred_element_type=F32)
        if final_norm:
            h3 = _rmsnorm_rows(h3, nfin_ref[...])
        o_ref[...] = h3


def _mix_mlp(h2d, mix_a, mix_b, mix_c, w_out, norm_ffn, w_ff1, w_ff2, norm_ple, w_gate, w_proj,
             p2d, norm_final, final_norm, tm, tf):
    t, d = h2d.shape
    dff = w_ff1.shape[1]
    nt, nf = t // tm, dff // tf
    row = lambda i, kf: (i, 0)
    fixed = lambda i, kf: (0, 0)
    return pl.pallas_call(
        functools.partial(_mlp_kernel, nf=nf, final_norm=final_norm),
        grid=(nt, nf),
        in_specs=[pl.BlockSpec((tm, d), row), pl.BlockSpec((tm, 256), row),
                  pl.BlockSpec((tm, 512), row), pl.BlockSpec((tm, 256), row),
                  pl.BlockSpec((d, d), fixed), pl.BlockSpec((1, d), fixed),
                  pl.BlockSpec((d, tf), lambda i, kf: (0, kf)),
                  pl.BlockSpec((tf, d), lambda i, kf: (kf, 0)),
                  pl.BlockSpec((1, d), fixed), pl.BlockSpec((d, d), fixed),
                  pl.BlockSpec((p2d.shape[1], d), fixed), pl.BlockSpec((tm, p2d.shape[1]), row),
                  pl.BlockSpec((1, d), fixed)],
        out_specs=pl.BlockSpec((tm, d), row),
        out_shape=jax.ShapeDtypeStruct((t, d), F32),
        scratch_shapes=[pltpu.VMEM((tm, d), F32), pltpu.VMEM((tm, d), BF16), pltpu.VMEM((tm, d), F32)],
        compiler_params=pltpu.CompilerParams(dimension_semantics=("parallel", "arbitrary"),
                                             vmem_limit_bytes=VMEM_LIMIT),
        name="mix_mlp",
    )(h2d, mix_a, mix_b, mix_c, w_out, norm_ffn, w_ff1, w_ff2, norm_ple, w_gate, w_proj, p2d,
      norm_final)


def _rope_tables(past, seq_len):
    half = HEAD // 2
    inv = ROPE_THETA ** (-2.0 * jnp.arange(half, dtype=F32) / HEAD)
    pos = past + jnp.arange(seq_len, dtype=jnp.int32)
    ang = pos.astype(F32)[:, None] * inv[None, :]
    cos, sin, zero = jnp.cos(ang), jnp.sin(ang), jnp.zeros_like(ang)
    rep = LANES // HEAD
    cos_t = jnp.tile(jnp.concatenate([cos, cos], axis=1), (1, rep))
    sa_t = jnp.tile(jnp.concatenate([-sin, zero], axis=1), (1, rep))
    sb_t = jnp.tile(jnp.concatenate([zero, sin], axis=1), (1, rep))
    return cos_t, sa_t, sb_t


def _pad_lanes(v, width=LANES):
    return jnp.pad(v, (0, width - v.shape[0]))[None, :]


def _prep_layer(i, norm_mix, w_in, a_conv_w, a_A_log, a_dt_bias, a_norm,
                b_lam_q1, b_lam_k1, b_lam_q2, b_lam_k2, b_norm,
                c_mu, c_w0, c_w_up, c_a0, c_a_up, c_g_up, c_k_k, c_k_a, c_r_k, c_ln_w, c_ln_b,
                w_out, norm_ffn, w_ff1, w_ff2, norm_ple, w_ple_gate, w_ple_proj):
    w = w_in[i]
    d = w.shape[0]
    w_perm = jnp.concatenate(
        [w[:, 0:1024], w[:, 1032:3592], w[:, 1024:1032],
         jnp.zeros((d, _PROJ_WIDTH - 3592), w.dtype)], axis=1).astype(BF16)
    zeros_rank = jnp.zeros_like(c_w_up[i])
    rwkv = (c_mu[i][None, :], c_w0[i][None, :],
            jnp.concatenate([c_w_up[i], zeros_rank], axis=0).astype(BF16), c_a0[i][None, :],
            jnp.concatenate([jnp.zeros_like(c_a_up[i]), c_a_up[i]], axis=0).astype(BF16),
            c_g_up[i].astype(BF16), c_k_k[i][None, :], c_k_a[i][None, :],
            c_r_k[i].reshape(1, -1), c_ln_w[i][None, :], c_ln_b[i][None, :])
    return dict(
        norm_mix=norm_mix[i][None, :], w_in=w_perm, conv_w=a_conv_w[i],
        alog=_pad_lanes(a_A_log[i]), dtb=_pad_lanes(a_dt_bias[i]),
        anorm=jnp.tile(a_norm[i], 4)[None, :],
        lam=jnp.stack([b_lam_q1[i], b_lam_k1[i], b_lam_q2[i], b_lam_k2[i]], axis=0),
        bnorm=b_norm[i][None, :], rwkv=rwkv,
        w_out=w_out[i].astype(BF16), norm_ffn=norm_ffn[i][None, :],
        w_ff1=w_ff1[i].astype(BF16), w_ff2=w_ff2[i].astype(BF16),
        norm_ple=norm_ple[i][None, :], w_gate=w_ple_gate[i].astype(BF16),
        w_proj=w_ple_proj[i].astype(BF16))


def _pick_tile(n, target):
    t = min(n, target)
    while n % t:
        t //= 2
    return t


def _trunk(x, p, cache_k, cache_v, conv_buf, delta_s, shift_prev, wkv_s, layers, norm_final):
    b, seq, d = x.shape
    depth = len(layers)
    past = cache_k.shape[2]
    t = b * seq
    tm = _pick_tile(t, 512)
    rope_tabs = _rope_tables(past, seq)
    h = x.reshape(t, d)
    states = []
    for i, lp in enumerate(layers):
        a_qkv, a_z, a_ab, q_b, k_f, v_f, k_b, v_b, c_raw = _input_projection(
            h, lp["norm_mix"], lp["w_in"], rope_tabs, seq, tm)
        mix_a, conv_n, delta_n = _gated_delta(
            a_qkv.reshape(b, seq, -1), a_z.reshape(b, seq, -1), a_ab.reshape(b, seq, -1),
            conv_buf[i], delta_s[i], lp["conv_w"], lp["alog"], lp["dtb"], lp["anorm"])
        k_all = k_b.reshape(b, seq, -1)
        v_all = v_b.reshape(b, seq, -1)
        if past:
            k_all = jnp.concatenate([cache_k[i].reshape(b, past, -1).astype(BF16), k_all], axis=1)
            v_all = jnp.concatenate([cache_v[i].reshape(b, past, -1).astype(BF16), v_all], axis=1)
        lam_init = 0.8 - 0.6 * math.exp(-0.3 * i)
        mix_b = _diff_attention(q_b.reshape(b, seq, -1), k_all, v_all, lp["lam"], lp["bnorm"],
                                past, lam_init, _pick_tile(seq, ATTN_TQ),
                                ATTN_TK if past + seq >= 8 * ATTN_TK else ATTN_TK // 2)
        mix_c, shift_n, wkv_n = _rwkv7(c_raw.reshape(b, seq, -1), shift_prev[i][:, None, :],
                                       wkv_s[i], lp["rwkv"])
        h = _mix_mlp(h, mix_a.reshape(t, -1), mix_b.reshape(t, -1), mix_c.reshape(t, -1),
                     lp["w_out"], lp["norm_ffn"], lp["w_ff1"], lp["w_ff2"], lp["norm_ple"],
                     lp["w_gate"], lp["w_proj"], p[i].reshape(t, -1), norm_final[None, :],
                     i == depth - 1, tm, 1024)
        states.append((conv_n, delta_n, k_f.reshape(b, seq, 4, LANES), v_f.reshape(b, seq, 4, LANES),
                       shift_n[:, 0, :], wkv_n))
    new_state = [jnp.stack([st[j] for st in states], axis=0) for j in range(6)]
    return h.reshape(b, seq, d), new_state


def kernel(x_prompt, x_sample, cache_b_k, cache_b_v, state_a_conv, state_a_delta, state_c_shift, state_c_wkv, p_prompt, p_sample, norm_mix, w_in, a_conv_w, a_A_log, a_dt_bias, a_norm, b_lam_q1, b_lam_k1, b_lam_q2, b_lam_k2, b_norm, c_mu, c_w0, c_w_up, c_a0, c_a_up, c_g_up, c_k_k, c_k_a, c_r_k, c_ln_w, c_ln_b, w_out, norm_ffn, w_ff1, w_ff2, norm_ple, w_ple_gate, w_ple_proj, norm_final):
    depth = w_in.shape[0]
    layers = [_prep_layer(i, norm_mix, w_in, a_conv_w, a_A_log, a_dt_bias, a_norm,
                          b_lam_q1, b_lam_k1, b_lam_q2, b_lam_k2, b_norm,
                          c_mu, c_w0, c_w_up, c_a0, c_a_up, c_g_up, c_k_k, c_k_a, c_r_k, c_ln_w, c_ln_b,
                          w_out, norm_ffn, w_ff1, w_ff2, norm_ple, w_ple_gate, w_ple_proj)
              for i in range(depth)]
    bp = x_prompt.shape[0]
    dt = x_prompt.dtype
    zeros = lambda ref: jnp.zeros((depth, bp) + ref.shape[2:], dt)
    empty_k = jnp.zeros((depth, bp, 0) + cache_b_k.shape[3:], dt)
    empty_v = jnp.zeros((depth, bp, 0) + cache_b_v.shape[3:], dt)
    y_prompt, st_p = _trunk(x_prompt, p_prompt, empty_k, empty_v, zeros(state_a_conv),
                            zeros(state_a_delta), zeros(state_c_shift), zeros(state_c_wkv),
                            layers, norm_final)
    y_sample, st_s = _trunk(x_sample, p_sample, cache_b_k, cache_b_v, state_a_conv, state_a_delta,
                            state_c_shift, state_c_wkv, layers, norm_final)
    return (y_prompt, y_sample, *st_p, *st_s)
```

```python
import functools
import math

import numpy as np
import jax
import jax.numpy as jnp
from jax import lax
from jax.experimental import pallas as pl
from jax.experimental.pallas import tpu as pltpu

F32 = jnp.float32
BF16 = jnp.bfloat16

CHUNK = 64
ROPE_THETA = 10000.0
NORM_EPS = 1e-6
L2_EPS = 1e-6
C_LN_EPS = 64e-5
RWKV_DECAY_OFFSET = 0.5
MASK_VALUE = float(np.finfo(np.float32).min)

LANES = 128
HEAD = 64
VMEM_LIMIT = 56 * 1024 * 1024
PROJ_TM = 512
MLP_TM = 1024
MLP_TF = 512
CHUNKS_PER_STEP = 4
ATTN_TQ = 512
ATTN_TK = 1024
ATTN_Q_SCALE = (64 ** -0.5) * math.log2(math.e)


def _iota(shape, axis):
    return lax.broadcasted_iota(jnp.int32, shape, axis)


_NN = (((1,), (0,)), ((), ()))
_NT = (((1,), (1,)), ((), ()))
_TN = (((0,), (0,)), ((), ()))


def _mm(a, b, dims=_NN):
    return lax.dot_general(a.astype(BF16), b.astype(BF16), dims, preferred_element_type=F32)


def _split3(x):
    h = x.astype(BF16)
    r = x - h.astype(F32)
    m = r.astype(BF16)
    lo = (r - m.astype(F32)).astype(BF16)
    return h, m, lo


def _mm_exact_rhs(x, ones_bf16, dims=_NN):
    h, m, lo = _split3(x)
    d = functools.partial(lax.dot_general, dimension_numbers=dims, preferred_element_type=F32)
    return d(h, ones_bf16) + d(m, ones_bf16) + d(lo, ones_bf16)


def _mm_exact_lhs(ones_bf16, x, dims=_NN):
    h, m, lo = _split3(x)
    d = functools.partial(lax.dot_general, dimension_numbers=dims, preferred_element_type=F32)
    return d(ones_bf16, h) + d(ones_bf16, m) + d(ones_bf16, lo)


def _segment_matrix(width):
    r = jnp.right_shift(_iota((width, width), 0), 6)
    c = jnp.right_shift(_iota((width, width), 1), 6)
    return (r == c).astype(BF16)


def _chunk_tri_ones(n, c, upper):
    row = _iota((n, n), 0)
    col = _iota((n, n), 1)
    shift = c.bit_length() - 1
    same = jnp.right_shift(row, shift) == jnp.right_shift(col, shift)
    tri = (row <= col) if upper else (row >= col)
    return (same & tri).astype(BF16)


def _chunking(seq):
    c = min(CHUNK, seq)
    assert c & (c - 1) == 0 and seq % c == 0, seq
    g = min(CHUNKS_PER_STEP, seq // c)
    while (seq // c) % g:
        g -= 1
    return c, g


def _rmsnorm_rows(x, g):
    return x * lax.rsqrt(jnp.mean(x * x, axis=-1, keepdims=True) + NORM_EPS) * g


def _unit_lower_inverses(a_list, c):
    row = _iota((c, c), 0)
    col = _iota((c, c), 1)

    def same_block(shift):
        return jnp.right_shift(row, shift) == jnp.right_shift(col, shift)

    eye = (row == col).astype(F32)
    leaf = same_block(3)
    ns = [jnp.where(leaf, -a, 0.0) for a in a_list]
    ts = [eye + n for n in ns]
    n2s = [_mm(n, n) for n in ns]
    ts = [t + _mm(t, n2) for t, n2 in zip(ts, n2s)]
    n4s = [_mm(n2, n2) for n2 in n2s]
    ts = [t + _mm(t, n4) for t, n4 in zip(ts, n4s)]
    shift = 3
    while (1 << shift) < c:
        off_mask = same_block(shift + 1) & jnp.logical_not(same_block(shift))
        tos = [_mm(t, jnp.where(off_mask, a, 0.0)) for t, a in zip(ts, a_list)]
        ts = [t - _mm(to, t) for t, to in zip(ts, tos)]
        shift += 1
    return ts


_PROJ_A_QKV = 0
_PROJ_A_Z = 768
_PROJ_B_Q = 1024
_PROJ_B_K = 1536
_PROJ_B_V = 2048
_PROJ_C = 2560
_PROJ_A_AB = 3584
_PROJ_WIDTH = 3712


def _proj_kernel(*refs, n_alias):
    x_ref, g_ref, w_ref, cos_ref, sa_ref, sb_ref = refs[:6]
    aqkv_ref, az_ref, aab_ref, q_ref, k_ref, v_ref, kb_ref, vb_ref, c_ref = refs[6 + n_alias:]
    xn = _rmsnorm_rows(x_ref[...], g_ref[...]).astype(BF16)

    def proj(c0, c1):
        return jnp.dot(xn, w_ref[:, c0:c1], preferred_element_type=F32)

    aqkv_ref[...] = proj(_PROJ_A_QKV, _PROJ_A_Z)
    az_ref[...] = proj(_PROJ_A_Z, _PROJ_B_Q)
    cos = cos_ref[...]
    sa = sa_ref[...]
    sb = sb_ref[...]

    def rope(x):
        return x * cos + pltpu.roll(x, LANES - 32, 1) * sa + pltpu.roll(x, 32, 1) * sb

    for h in range(4):
        lo, hi = h * LANES, (h + 1) * LANES
        q = rope(proj(_PROJ_B_Q + lo, _PROJ_B_Q + hi))
        q_ref[:, lo:hi] = (q * ATTN_Q_SCALE).astype(BF16)
        k = rope(proj(_PROJ_B_K + lo, _PROJ_B_K + hi))
        k_ref[:, lo:hi] = k
        kb_ref[:, lo:hi] = k.astype(BF16)
    v = proj(_PROJ_B_V, _PROJ_C)
    v_ref[...] = v
    vb_ref[...] = v.astype(BF16)
    c_ref[...] = proj(_PROJ_C, _PROJ_A_AB)
    aab_ref[...] = proj(_PROJ_A_AB, _PROJ_WIDTH)


def _input_projection(x2d, g_row, w_bf16, rope_tabs, seq_len, tm, layer, depth, kv_stacked):
    t, d = x2d.shape
    nt = t // tm
    cos_t, sa_t, sb_t = rope_tabs
    if tm >= seq_len:
        reps = tm // seq_len
        cos_t, sa_t, sb_t = (jnp.tile(a, (reps, 1)) for a in (cos_t, sa_t, sb_t))
        tab_map = lambda i: (0, 0)
    else:
        per_seq = seq_len // tm
        tab_map = lambda i: (i % per_seq, 0)
    row = lambda i: (i, 0)
    fixed = lambda i: (0, 0)
    tab_spec = pl.BlockSpec((tm, LANES), tab_map)
    widths = [(768, F32), (256, F32), (128, F32), (512, BF16), (512, F32), (512, F32),
              (512, BF16), (512, BF16), (1024, F32)]
    stacked = (4, 5)
    out_specs = [pl.BlockSpec((None, tm, w), lambda i: (layer, i, 0)) if n in stacked
                 else pl.BlockSpec((tm, w), row) for n, (w, _) in enumerate(widths)]
    out_shape = [jax.ShapeDtypeStruct((depth, t, w) if n in stacked else (t, w), dt)
                 for n, (w, dt) in enumerate(widths)]
    in_specs = [pl.BlockSpec((tm, d), row), pl.BlockSpec((1, d), fixed),
                pl.BlockSpec((d, _PROJ_WIDTH), fixed), tab_spec, tab_spec, tab_spec]
    operands = [x2d, g_row, w_bf16, cos_t, sa_t, sb_t]
    aliases = {}
    if kv_stacked is not None:
        for out_idx, arr in zip(stacked, kv_stacked):
            aliases[len(operands)] = out_idx
            in_specs.append(pl.BlockSpec(memory_space=pl.ANY))
            operands.append(arr)
    return pl.pallas_call(
        functools.partial(_proj_kernel, n_alias=len(aliases)),
        grid=(nt,),
        in_specs=in_specs,
        out_specs=out_specs,
        out_shape=out_shape,
        input_output_aliases=aliases,
        compiler_params=pltpu.CompilerParams(dimension_semantics=("parallel",),
                                             vmem_limit_bytes=VMEM_LIMIT),
        name="input_projection",
    )(*operands)


def _delta_kernel(qkv_ref, z_ref, ab_ref, conv0_ref, s0_ref, convw_ref, alog_ref, dtb_ref, anorm_ref,
                  mix_ref, convn_ref, sn_ref, xp_ref, s_ref, o_ref, *, c, g, nl):
    l = pl.program_id(1)
    tl = c * g

    @pl.when(l == 0)
    def _():
        xp_ref[5:8, :] = conv0_ref[0]
        s_ref[...] = s0_ref[0]

    x = qkv_ref[0]
    xp_ref[8:8 + tl, :] = x
    w = convw_ref[...]
    y = xp_ref[5:5 + tl, :] * w[0:1]
    y = y + xp_ref[6:6 + tl, :] * w[1:2]
    y = y + xp_ref[7:7 + tl, :] * w[2:3]
    y = y + x * w[3:4]
    tail = xp_ref[5 + tl:8 + tl, :]
    xp_ref[5:8, :] = tail

    @pl.when(l == nl - 1)
    def _():
        convn_ref[0] = tail

    act = y * jax.nn.sigmoid(y)
    seg = _segment_matrix(4 * HEAD)

    def l2n(t):
        return t * lax.rsqrt(_mm_exact_rhs(t * t, seg) + L2_EPS)

    q = l2n(act[:, 0:256]) * (HEAD ** -0.5)
    k = l2n(act[:, 256:512])
    v = act[:, 512:768]

    ab = ab_ref[0]
    gl = -jnp.exp(alog_ref[...]) * jax.nn.softplus(ab + dtb_ref[...])
    beta_all = jax.nn.sigmoid(ab)
    g_all = _mm_exact_lhs(_chunk_tri_ones(tl, c, False), gl)
    g_rows = _mm_exact_rhs(gl, _chunk_tri_ones(tl, c, True), _TN)

    row = _iota((c, c), 0)
    col = _iota((c, c), 1)
    lower = row >= col
    strict = row > col

    probs = [(slice(ci * c, (ci + 1) * c), h) for ci in range(g) for h in range(4)]
    heads = lambda h: slice(h * HEAD, (h + 1) * HEAD)
    g_col = [g_all[rows, h:h + 1] for rows, h in probs]
    beta = [beta_all[rows, 4 + h:5 + h] for rows, h in probs]
    decay = [jnp.where(lower, jnp.exp(jnp.where(lower, gc - g_rows[h:h + 1, rows], 0.0)), 0.0)
             for gc, (rows, h) in zip(g_col, probs)]
    kh = [k[rows, heads(h)] for rows, h in probs]
    qh = [q[rows, heads(h)] for rows, h in probs]
    vh = [v[rows, heads(h)] for rows, h in probs]
    kq = [_mm(jnp.concatenate([kk_, qq_], axis=0), kk_, _NT) for kk_, qq_ in zip(kh, qh)]
    t_inv = _unit_lower_inverses(
        [jnp.where(strict, b * x[:c] * d, 0.0) for b, x, d in zip(beta, kq, decay)], c)
    eg = [jnp.exp(gc) for gc in g_col]
    sol = [_mm(t, jnp.concatenate([vv * b, kk_ * (b * e)], axis=1))
           for t, vv, kk_, b, e in zip(t_inv, vh, kh, beta, eg)]
    qk = [jnp.where(lower, x[c:] * d, 0.0) for x, d in zip(kq, decay)]
    g_last = [gc[c - 1:c, :] for gc in g_col]
    wq = [jnp.concatenate([so[:, HEAD:], qq_ * e], axis=0) for so, qq_, e in zip(sol, qh, eg)]
    k_tail = [kk_ * jnp.exp(gl_ - gc) for kk_, gl_, gc in zip(kh, g_last, g_col)]
    decay_end = [jnp.exp(gl_) for gl_ in g_last]

    states = [s_ref[h] for h in range(4)]
    for ci in range(g):
        rows = slice(ci * c, (ci + 1) * c)
        idx = [ci * 4 + h for h in range(4)]
        ws = [_mm(wq[n], states[h]) for h, n in enumerate(idx)]
        u_new = [sol[n][:, :HEAD] - ws[h][:c] for h, n in enumerate(idx)]
        o_new = [ws[h][c:] + _mm(qk[n], u_new[h]) for h, n in enumerate(idx)]
        states = [states[h] * decay_end[n] + _mm(k_tail[n], u_new[h], _TN) for h, n in enumerate(idx)]
        for h in range(4):
            o_ref[rows, heads(h)] = o_new[h]
    for h in range(4):
        s_ref[h] = states[h]

    o = o_ref[...]
    ms = _mm_exact_rhs(o * o, seg) * (1.0 / HEAD)
    z = z_ref[0]
    ao = (o * lax.rsqrt(ms + NORM_EPS) * anorm_ref[...]) * (z * jax.nn.sigmoid(z))
    mix_ref[0] = ao.astype(BF16)

    @pl.when(l == nl - 1)
    def _():
        sn_ref[0] = s_ref[...]


def _gated_delta(a_qkv, a_z, a_ab, conv0, s0, conv_w, alog_row, dtb_row, anorm_row):
    b, seq, _ = a_qkv.shape
    c, g = _chunking(seq)
    tl = c * g
    nl = seq // tl
    tile = lambda i, l: (i, l, 0)
    per_b3 = lambda i, l: (i, 0, 0)
    per_b4 = lambda i, l: (i, 0, 0, 0)
    fixed = lambda i, l: (0, 0)
    return pl.pallas_call(
        functools.partial(_delta_kernel, c=c, g=g, nl=nl),
        grid=(b, nl),
        in_specs=[pl.BlockSpec((1, tl, 768), tile), pl.BlockSpec((1, tl, 256), tile),
                  pl.BlockSpec((1, tl, LANES), tile), pl.BlockSpec((1, 3, 768), per_b3),
                  pl.BlockSpec((1, 4, HEAD, HEAD), per_b4), pl.BlockSpec((4, 768), fixed),
                  pl.BlockSpec((1, LANES), fixed), pl.BlockSpec((1, LANES), fixed),
                  pl.BlockSpec((1, 256), fixed)],
        out_specs=[pl.BlockSpec((1, tl, 256), tile), pl.BlockSpec((1, 3, 768), per_b3),
                   pl.BlockSpec((1, 4, HEAD, HEAD), per_b4)],
        out_shape=[jax.ShapeDtypeStruct((b, seq, 256), BF16),
                   jax.ShapeDtypeStruct((b, 3, 768), F32),
                   jax.ShapeDtypeStruct((b, 4, HEAD, HEAD), F32)],
        scratch_shapes=[pltpu.VMEM((tl + 8, 768), F32), pltpu.VMEM((4, HEAD, HEAD), F32),
                        pltpu.VMEM((tl, 256), F32)],
        compiler_params=pltpu.CompilerParams(dimension_semantics=("parallel", "arbitrary"),
                                             vmem_limit_bytes=VMEM_LIMIT),
        name="gated_delta",
    )(a_qkv, a_z, a_ab, conv0, s0, conv_w, alog_row, dtb_row, anorm_row)


def _rwkv_kernel(c_ref, shift0_ref, s0_ref, mu_ref, w0_ref, wup_ref, a0_ref, aup_ref, gup_ref,
                 kk_ref, ka_ref, rk_ref, lnw_ref, lnb_ref,
                 mix_ref, shiftn_ref, sn_ref, xp_ref, s_ref, y_ref, *, c, g, nl):
    l = pl.program_id(1)
    tl = c * g

    @pl.when(l == 0)
    def _():
        xp_ref[7:8, :] = shift0_ref[0]
        s_ref[...] = s0_ref[0]

    raw = c_ref[0]
    xp_ref[8:8 + tl, :] = raw
    prev = xp_ref[7:7 + tl, :]
    last = raw[tl - 1:tl, :]
    xp_ref[7:8, :] = last

    @pl.when(l == nl - 1)
    def _():
        shiftn_ref[0] = last

    x = raw + (prev - raw) * mu_ref[...]
    cr, ck, cv = x[:, 0:256], x[:, 256:512], x[:, 512:768]
    c_wa = x[:, 768:896]
    c_g = x[:, 896:1024]
    w_log = -jnp.exp(-jax.nn.softplus(-(w0_ref[...] + _mm(jnp.tanh(c_wa), wup_ref[...])))
                     - RWKV_DECAY_OFFSET)
    ca = jax.nn.sigmoid(a0_ref[...] + _mm(c_wa, aup_ref[...]))
    cg = _mm(jax.nn.sigmoid(c_g), gup_ref[...])
    seg = _segment_matrix(4 * HEAD)
    kkv = ck * kk_ref[...]
    kk = kkv * lax.rsqrt(_mm_exact_rhs(kkv * kkv, seg) + L2_EPS)
    ck = ck * (1.0 + (ca - 1.0) * ka_ref[...])

    g_cum = _mm_exact_lhs(_chunk_tri_ones(tl, c, False), w_log)
    e_pos = jnp.exp(g_cum)
    e_neg = jnp.exp(-g_cum)
    a_t = -kk * jnp.exp(g_cum - w_log)
    b_t = (kk * ca) * e_neg
    k_t = ck * e_neg
    r_t = cr * e_pos

    row = _iota((c, c), 0)
    col = _iota((c, c), 1)
    lower = row >= col
    strict = row > col

    probs = [(slice(ci * c, (ci + 1) * c), h) for ci in range(g) for h in range(4)]
    heads = lambda h: slice(h * HEAD, (h + 1) * HEAD)
    bh = [b_t[rows, heads(h)] for rows, h in probs]
    kh = [k_t[rows, heads(h)] for rows, h in probs]
    vh = [cv[rows, heads(h)] for rows, h in probs]
    ar = [jnp.concatenate([a_t[rows, heads(h)], r_t[rows, heads(h)]], axis=0) for rows, h in probs]
    pb = [_mm(x, y_, _NT) for x, y_ in zip(ar, bh)]
    pk = [_mm(x, y_, _NT) for x, y_ in zip(ar, kh)]
    t_inv = _unit_lower_inverses([-jnp.where(strict, x[:c], 0.0) for x in pb], c)
    n_rb = [jnp.where(lower, x[c:], 0.0) for x in pb]
    mn = [_mm(jnp.concatenate([jnp.where(strict, x[:c], 0.0), jnp.where(lower, x[c:], 0.0)], axis=0), vv)
          for x, vv in zip(pk, vh)]
    bk = [jnp.concatenate([x, y_], axis=0) for x, y_ in zip(bh, kh)]
    g_end = [e_pos[rows.stop - 1:rows.stop, heads(h)] for rows, h in probs]

    states = [s_ref[h] for h in range(4)]
    for ci in range(g):
        rows = slice(ci * c, (ci + 1) * c)
        idx = [ci * 4 + h for h in range(4)]
        ars = [_mm(ar[n], states[h], _NT) for h, n in enumerate(idx)]
        u = [_mm(t_inv[n], ars[h][:c] + mn[n][:c]) for h, n in enumerate(idx)]
        y_new = [ars[h][c:] + _mm(n_rb[n], u[h]) + mn[n][c:] for h, n in enumerate(idx)]
        states = [(states[h] + _mm(jnp.concatenate([u[h], vh[n]], axis=0), bk[n], _TN)) * g_end[n]
                  for h, n in enumerate(idx)]
        for h in range(4):
            y_ref[rows, heads(h)] = y_new[h]
    for h in range(4):
        s_ref[h] = states[h]

    y = y_ref[...]
    mean = _mm_exact_rhs(y, seg) * (1.0 / HEAD)
    yc = y - mean
    var = _mm_exact_rhs(yc * yc, seg) * (1.0 / HEAD)
    cy = (yc * lax.rsqrt(var + C_LN_EPS)) * lnw_ref[...] + lnb_ref[...]
    bonus = _mm_exact_rhs(cr * ck * rk_ref[...], seg) * cv
    mix_ref[0] = ((cy + bonus) * cg).astype(BF16)

    @pl.when(l == nl - 1)
    def _():
        sn_ref[0] = s_ref[...]


def _rwkv7(c_raw, shift0, s0, params):
    b, seq, width = c_raw.shape
    c, g = _chunking(seq)
    tl = c * g
    nl = seq // tl
    tile = lambda i, l: (i, l, 0)
    per_b3 = lambda i, l: (i, 0, 0)
    per_b4 = lambda i, l: (i, 0, 0, 0)
    fixed = lambda i, l: (0, 0)
    param_specs = [pl.BlockSpec(p.shape, fixed) for p in params]
    return pl.pallas_call(
        functools.partial(_rwkv_kernel, c=c, g=g, nl=nl),
        grid=(b, nl),
        in_specs=[pl.BlockSpec((1, tl, width), tile), pl.BlockSpec((1, 1, width), per_b3),
                  pl.BlockSpec((1, 4, HEAD, HEAD), per_b4)] + param_specs,
        out_specs=[pl.BlockSpec((1, tl, 256), tile), pl.BlockSpec((1, 1, width), per_b3),
                   pl.BlockSpec((1, 4, HEAD, HEAD), per_b4)],
        out_shape=[jax.ShapeDtypeStruct((b, seq, 256), BF16),
                   jax.ShapeDtypeStruct((b, 1, width), F32),
                   jax.ShapeDtypeStruct((b, 4, HEAD, HEAD), F32)],
        scratch_shapes=[pltpu.VMEM((tl + 8, width), F32), pltpu.VMEM((4, HEAD, HEAD), F32),
                        pltpu.VMEM((tl, 256), F32)],
        compiler_params=pltpu.CompilerParams(dimension_semantics=("parallel", "arbitrary"),
                                             vmem_limit_bytes=VMEM_LIMIT),
        name="rwkv7",
    )(c_raw, shift0, s0, *params)


def _attn_kernel(i_tab, j_tab, flag_tab, q_ref, k_ref, v_ref, lam_ref, bn_ref, o_ref,
                 m_ref, l_ref, acc_ref, *, tq, tk, past, lk_true, lam_init):
    p = pl.program_id(1)
    i = i_tab[p]
    j = j_tab[p]

    @pl.when(j == 0)
    def _():
        m_ref[...] = jnp.full(m_ref.shape, MASK_VALUE, F32)
        l_ref[...] = jnp.zeros(l_ref.shape, F32)
        acc_ref[...] = jnp.zeros(acc_ref.shape, F32)

    nslab = tk // LANES

    def step(masked):
        if masked:
            q_pos = past + i * tq + _iota((tq, tk), 0)
            k_pos = j * tk + _iota((tq, tk), 1)
            visible = (jnp.right_shift(k_pos, 6) <= jnp.right_shift(q_pos, 6)) & (k_pos < lk_true)
        q = q_ref[0]
        k = k_ref[0]
        v = v_ref[0]

        def scores(hm):
            sl = slice(hm * HEAD, (hm + 1) * HEAD)
            return lax.dot_general(q[:, sl], k[:, sl], _NT, preferred_element_type=F32)

        s_next = scores(0)
        for hm in range(8):
            h = hm // 2
            s = s_next
            if hm + 1 < 8:
                s_next = scores(hm + 1)
            if masked:
                s = jnp.where(visible, s, MASK_VALUE)
            slabs = [s[:, n * LANES:(n + 1) * LANES] for n in range(nslab)]
            m_cur = functools.reduce(jnp.maximum, slabs)
            m_prev = m_ref[hm]
            m_new = jnp.maximum(m_prev, jnp.max(m_cur, axis=1, keepdims=True))
            alpha = jnp.exp2(m_prev - m_new)
            es = [jnp.exp2(sb - m_new) for sb in slabs]
            l_ref[hm] = alpha * l_ref[hm] + functools.reduce(lambda a, b: a + b, es)
            e = jnp.concatenate([x.astype(BF16) for x in es], axis=1)
            acc_ref[hm] = alpha * acc_ref[hm] + jnp.dot(
                e, v[:, h * LANES:(h + 1) * LANES], preferred_element_type=F32)
            m_ref[hm] = m_new

    needs_mask = jnp.bitwise_and(flag_tab[p], 2) != 0

    @pl.when(needs_mask)
    def _():
        step(True)

    @pl.when(jnp.logical_not(needs_mask))
    def _():
        step(False)

    @pl.when(jnp.bitwise_and(flag_tab[p], 1) != 0)
    def _():
        lp = lam_ref[...]
        lam = (jnp.exp(jnp.sum(lp[0:1] * lp[1:2], axis=1, keepdims=True))
               - jnp.exp(jnp.sum(lp[2:3] * lp[3:4], axis=1, keepdims=True)) + lam_init)
        for h in range(4):
            l1 = jnp.sum(l_ref[2 * h], axis=1, keepdims=True)
            l2 = jnp.sum(l_ref[2 * h + 1], axis=1, keepdims=True)
            o = acc_ref[2 * h] / l1 - lam * (acc_ref[2 * h + 1] / l2)
            o = _rmsnorm_rows(o, bn_ref[...]) * (1.0 - lam_init)
            o_ref[0, :, h * LANES:(h + 1) * LANES] = o.astype(BF16)


def _diff_attention(q, k, v, lam_params, bnorm_row, past, lam_init, tq, tk):
    b, lq, width = q.shape
    lk_true = k.shape[1]
    nk = -(-lk_true // tk)
    if nk * tk != lk_true:
        pad = ((0, 0), (0, nk * tk - lk_true), (0, 0))
        k = jnp.pad(k, pad)
        v = jnp.pad(v, pad)
    nq = lq // tq
    i_list, j_list, flag_list = [], [], []
    for i in range(nq):
        first_pos = past + i * tq
        last_pos = first_pos + tq - 1
        j_max = min(((last_pos // CHUNK + 1) * CHUNK - 1) // tk, nk - 1)
        for j in range(j_max + 1):
            key_end = (j + 1) * tk
            all_visible = (key_end - 1) // CHUNK <= first_pos // CHUNK and key_end <= lk_true
            i_list.append(i)
            j_list.append(j)
            flag_list.append((1 if j == j_max else 0) + (0 if all_visible else 2))
    tabs = [jnp.asarray(np.asarray(t, np.int32)) for t in (i_list, j_list, flag_list)]
    grid_spec = pltpu.PrefetchScalarGridSpec(
        num_scalar_prefetch=3,
        grid=(b, len(i_list)),
        in_specs=[pl.BlockSpec((1, tq, width), lambda bi, p, it, jt, lt: (bi, it[p], 0)),
                  pl.BlockSpec((1, tk, width), lambda bi, p, it, jt, lt: (bi, jt[p], 0)),
                  pl.BlockSpec((1, tk, width), lambda bi, p, it, jt, lt: (bi, jt[p], 0)),
                  pl.BlockSpec((4, HEAD), lambda bi, p, it, jt, lt: (0, 0)),
                  pl.BlockSpec((1, LANES), lambda bi, p, it, jt, lt: (0, 0))],
        out_specs=pl.BlockSpec((1, tq, width), lambda bi, p, it, jt, lt: (bi, it[p], 0)),
        scratch_shapes=[pltpu.VMEM((8, tq, LANES), F32), pltpu.VMEM((8, tq, LANES), F32),
                        pltpu.VMEM((8, tq, LANES), F32)],
    )
    return pl.pallas_call(
        functools.partial(_attn_kernel, tq=tq, tk=tk, past=past, lk_true=lk_true, lam_init=lam_init),
        grid_spec=grid_spec,
        out_shape=jax.ShapeDtypeStruct((b, lq, width), BF16),
        compiler_params=pltpu.CompilerParams(dimension_semantics=("parallel", "arbitrary"),
                                             vmem_limit_bytes=VMEM_LIMIT),
        name="diff_attention",
    )(*tabs, q, k, v, lam_params, bnorm_row)


def _mlp_kernel(h_ref, ma_ref, mb_ref, mc_ref, wo_ref, nf_ref, w1_ref, w2_ref, np_ref, wg_ref,
                wp_ref, p_ref, nfin_ref, o_ref, h1_ref, xn_ref, acc_ref, *, nf, final_norm):
    kf = pl.program_id(1)

    @pl.when(kf == 0)
    def _():
        mixed = (jnp.dot(ma_ref[...], wo_ref[0:256, :], preferred_element_type=F32)
                 + jnp.dot(mb_ref[...], wo_ref[256:768, :], preferred_element_type=F32)
                 + jnp.dot(mc_ref[...], wo_ref[768:1024, :], preferred_element_type=F32))
        h1 = h_ref[...] + mixed
        h1_ref[...] = h1
        xn_ref[...] = _rmsnorm_rows(h1, nf_ref[...]).astype(BF16)
        acc_ref[...] = jnp.zeros(acc_ref.shape, F32)

    u = jnp.maximum(jnp.dot(xn_ref[...], w1_ref[...], preferred_element_type=F32), 0.0)
    acc_ref[...] += jnp.dot((u * u).astype(BF16), w2_ref[...], preferred_element_type=F32)

    @pl.when(kf == nf - 1)
    def _():
        h2 = h1_ref[...] + acc_ref[...]
        gate = jax.nn.sigmoid(jnp.dot(_rmsnorm_rows(h2, np_ref[...]).astype(BF16), wg_ref[...],
                                      preferred_element_type=F32))
        h3 = h2 + gate * jnp.dot(p_ref[...].astype(BF16), wp_ref[...], preferred_element_type=F32)
        if final_norm:
            h3 = _rmsnorm_rows(h3, nfin_ref[...])
        o_ref[...] = h3


def _mix_mlp(h2d, mix_a, mix_b, mix_c, w_out, norm_ffn, w_ff1, w_ff2, norm_ple, w_gate, w_proj,
             p2d, norm_final, final_norm, tm, tf):
    t, d = h2d.shape
    dff = w_ff1.shape[1]
    nt, nf = t // tm, dff // tf
    row = lambda i, kf: (i, 0)
    fixed = lambda i, kf: (0, 0)
    return pl.pallas_call(
        functools.partial(_mlp_kernel, nf=nf, final_norm=final_norm),
        grid=(nt, nf),
        in_specs=[pl.BlockSpec((tm, d), row), pl.BlockSpec((tm, 256), row),
                  pl.BlockSpec((tm, 512), row), pl.BlockSpec((tm, 256), row),
                  pl.BlockSpec((d, d), fixed), pl.BlockSpec((1, d), fixed),
                  pl.BlockSpec((d, tf), lambda i, kf: (0, kf)),
                  pl.BlockSpec((tf, d), lambda i, kf: (kf, 0)),
                  pl.BlockSpec((1, d), fixed), pl.BlockSpec((d, d), fixed),
                  pl.BlockSpec((p2d.shape[1], d), fixed), pl.BlockSpec((tm, p2d.shape[1]), row),
                  pl.BlockSpec((1, d), fixed)],
        out_specs=pl.BlockSpec((tm, d), row),
        out_shape=jax.ShapeDtypeStruct((t, d), F32),
        scratch_shapes=[pltpu.VMEM((tm, d), F32), pltpu.VMEM((tm, d), BF16), pltpu.VMEM((tm, d), F32)],
        compiler_params=pltpu.CompilerParams(dimension_semantics=("parallel", "arbitrary"),
                                             vmem_limit_bytes=VMEM_LIMIT),
        name="mix_mlp",
    )(h2d, mix_a, mix_b, mix_c, w_out, norm_ffn, w_ff1, w_ff2, norm_ple, w_gate, w_proj, p2d,
      norm_final)


def _rope_tables(past, seq_len):
    half = HEAD // 2
    inv = ROPE_THETA ** (-2.0 * jnp.arange(half, dtype=F32) / HEAD)
    pos = past + jnp.arange(seq_len, dtype=jnp.int32)
    ang = pos.astype(F32)[:, None] * inv[None, :]
    cos, sin, zero = jnp.cos(ang), jnp.sin(ang), jnp.zeros_like(ang)
    rep = LANES // HEAD
    cos_t = jnp.tile(jnp.concatenate([cos, cos], axis=1), (1, rep))
    sa_t = jnp.tile(jnp.concatenate([-sin, zero], axis=1), (1, rep))
    sb_t = jnp.tile(jnp.concatenate([zero, sin], axis=1), (1, rep))
    return cos_t, sa_t, sb_t


def _pad_lanes(v, width=LANES):
    return jnp.pad(v, (0, width - v.shape[0]))[None, :]


def _prep_layer(i, norm_mix, w_in, a_conv_w, a_A_log, a_dt_bias, a_norm,
                b_lam_q1, b_lam_k1, b_lam_q2, b_lam_k2, b_norm,
                c_mu, c_w0, c_w_up, c_a0, c_a_up, c_g_up, c_k_k, c_k_a, c_r_k, c_ln_w, c_ln_b,
                w_out, norm_ffn, w_ff1, w_ff2, norm_ple, w_ple_gate, w_ple_proj):
    w = w_in[i]
    d = w.shape[0]
    w_perm = jnp.concatenate(
        [w[:, 0:1024], w[:, 1032:3592], w[:, 1024:1032],
         jnp.zeros((d, _PROJ_WIDTH - 3592), w.dtype)], axis=1).astype(BF16)
    zeros_rank = jnp.zeros_like(c_w_up[i])
    rwkv = (c_mu[i][None, :], c_w0[i][None, :],
            jnp.concatenate([c_w_up[i], zeros_rank], axis=0).astype(BF16), c_a0[i][None, :],
            jnp.concatenate([jnp.zeros_like(c_a_up[i]), c_a_up[i]], axis=0).astype(BF16),
            c_g_up[i].astype(BF16), c_k_k[i][None, :], c_k_a[i][None, :],
            c_r_k[i].reshape(1, -1), c_ln_w[i][None, :], c_ln_b[i][None, :])
    return dict(
        norm_mix=norm_mix[i][None, :], w_in=w_perm, conv_w=a_conv_w[i],
        alog=_pad_lanes(a_A_log[i]), dtb=_pad_lanes(a_dt_bias[i]),
        anorm=jnp.tile(a_norm[i], 4)[None, :],
        lam=jnp.stack([b_lam_q1[i], b_lam_k1[i], b_lam_q2[i], b_lam_k2[i]], axis=0),
        bnorm=b_norm[i][None, :], rwkv=rwkv,
        w_out=w_out[i].astype(BF16), norm_ffn=norm_ffn[i][None, :],
        w_ff1=w_ff1[i].astype(BF16), w_ff2=w_ff2[i].astype(BF16),
        norm_ple=norm_ple[i][None, :], w_gate=w_ple_gate[i].astype(BF16),
        w_proj=w_ple_proj[i].astype(BF16))


def _pick_tile(n, target):
    t = min(n, target)
    while n % t:
        t //= 2
    return t


def _trunk(x, p, cache_k, cache_v, conv_buf, delta_s, shift_prev, wkv_s, layers, norm_final):
    b, seq, d = x.shape
    depth = len(layers)
    past = cache_k.shape[2]
    t = b * seq
    tm = _pick_tile(t, PROJ_TM)
    rope_tabs = _rope_tables(past, seq)
    h = x.reshape(t, d)
    states = []
    kv_stacked = None
    for i, lp in enumerate(layers):
        a_qkv, a_z, a_ab, q_b, k_f, v_f, k_b, v_b, c_raw = _input_projection(
            h, lp["norm_mix"], lp["w_in"], rope_tabs, seq, tm, i, depth, kv_stacked)
        kv_stacked = (k_f, v_f)
        mix_a, conv_n, delta_n = _gated_delta(
            a_qkv.reshape(b, seq, -1), a_z.reshape(b, seq, -1), a_ab.reshape(b, seq, -1),
            conv_buf[i], delta_s[i], lp["conv_w"], lp["alog"], lp["dtb"], lp["anorm"])
        k_all = k_b.reshape(b, seq, -1)
        v_all = v_b.reshape(b, seq, -1)
        if past:
            k_all = jnp.concatenate([cache_k[i].reshape(b, past, -1).astype(BF16), k_all], axis=1)
            v_all = jnp.concatenate([cache_v[i].reshape(b, past, -1).astype(BF16), v_all], axis=1)
        lam_init = 0.8 - 0.6 * math.exp(-0.3 * i)
        mix_b = _diff_attention(q_b.reshape(b, seq, -1), k_all, v_all, lp["lam"], lp["bnorm"],
                                past, lam_init, _pick_tile(seq, ATTN_TQ),
                                ATTN_TK if past + seq >= 8 * ATTN_TK else ATTN_TK // 2)
        mix_c, shift_n, wkv_n = _rwkv7(c_raw.reshape(b, seq, -1), shift_prev[i][:, None, :],
                                       wkv_s[i], lp["rwkv"])
        h = _mix_mlp(h, mix_a.reshape(t, -1), mix_b.reshape(t, -1), mix_c.reshape(t, -1),
                     lp["w_out"], lp["norm_ffn"], lp["w_ff1"], lp["w_ff2"], lp["norm_ple"],
                     lp["w_gate"], lp["w_proj"], p[i].reshape(t, -1), norm_final[None, :],
                     i == depth - 1, _pick_tile(t, MLP_TM), MLP_TF)
        states.append((conv_n, delta_n, shift_n[:, 0, :], wkv_n))
    conv_all, delta_all, shift_all, wkv_all = (
        jnp.stack([st[j] for st in states], axis=0) for j in range(4))
    k_all_layers, v_all_layers = (a.reshape(depth, b, seq, 4, LANES) for a in kv_stacked)
    return h.reshape(b, seq, d), [conv_all, delta_all, k_all_layers, v_all_layers, shift_all, wkv_all]


def kernel(x_prompt, x_sample, cache_b_k, cache_b_v, state_a_conv, state_a_delta, state_c_shift, state_c_wkv, p_prompt, p_sample, norm_mix, w_in, a_conv_w, a_A_log, a_dt_bias, a_norm, b_lam_q1, b_lam_k1, b_lam_q2, b_lam_k2, b_norm, c_mu, c_w0, c_w_up, c_a0, c_a_up, c_g_up, c_k_k, c_k_a, c_r_k, c_ln_w, c_ln_b, w_out, norm_ffn, w_ff1, w_ff2, norm_ple, w_ple_gate, w_ple_proj, norm_final):
    depth = w_in.shape[0]
    layers = [_prep_layer(i, norm_mix, w_in, a_conv_w, a_A_log, a_dt_bias, a_norm,
                          b_lam_q1, b_lam_k1, b_lam_q2, b_lam_k2, b_norm,
                          c_mu, c_w0, c_w_up, c_a0, c_a_up, c_g_up, c_k_k, c_k_a, c_r_k, c_ln_w, c_ln_b,
                          w_out, norm_ffn, w_ff1, w_ff2, norm_ple, w_ple_gate, w_ple_proj)
              for i in range(depth)]
    bp = x_prompt.shape[0]
    dt = x_prompt.dtype
    zeros = lambda ref: jnp.zeros((depth, bp) + ref.shape[2:], dt)
    empty_k = jnp.zeros((depth, bp, 0) + cache_b_k.shape[3:], dt)
    empty_v = jnp.zeros((depth, bp, 0) + cache_b_v.shape[3:], dt)
    y_prompt, st_p = _trunk(x_prompt, p_prompt, empty_k, empty_v, zeros(state_a_conv),
                            zeros(state_a_delta), zeros(state_c_shift), zeros(state_c_wkv),
                            layers, norm_final)
    y_sample, st_s = _trunk(x_sample, p_sample, cache_b_k, cache_b_v, state_a_conv, state_a_delta,
                            state_c_shift, state_c_wkv, layers, norm_final)
    return (y_prompt, y_sample, *st_p, *st_s)
```

```python
import functools
import math

import numpy as np
import jax
import jax.numpy as jnp
from jax import lax
from jax.experimental import pallas as pl
from jax.experimental.pallas import tpu as pltpu

F32 = jnp.float32
BF16 = jnp.bfloat16

CHUNK = 64
ROPE_THETA = 10000.0
NORM_EPS = 1e-6
L2_EPS = 1e-6
C_LN_EPS = 64e-5
RWKV_DECAY_OFFSET = 0.5
MASK_VALUE = float(np.finfo(np.float32).min)

LANES = 128
HEAD = 64
VMEM_LIMIT = 56 * 1024 * 1024
PROJ_TM = 512
MLP_TM = 1024
MLP_TF = 512
CHUNKS_PER_STEP = 8
ATTN_TQ = 512
ATTN_TK = 1024
ATTN_Q_SCALE = (64 ** -0.5) * math.log2(math.e)


def _iota(shape, axis):
    return lax.broadcasted_iota(jnp.int32, shape, axis)


_NN = (((1,), (0,)), ((), ()))
_NT = (((1,), (1,)), ((), ()))
_TN = (((0,), (0,)), ((), ()))


def _mm(a, b, dims=_NN):
    return lax.dot_general(a.astype(BF16), b.astype(BF16), dims, preferred_element_type=F32)


def _split3(x):
    h = x.astype(BF16)
    r = x - h.astype(F32)
    m = r.astype(BF16)
    lo = (r - m.astype(F32)).astype(BF16)
    return h, m, lo


def _mm_exact_rhs(x, ones_bf16, dims=_NN):
    h, m, lo = _split3(x)
    d = functools.partial(lax.dot_general, dimension_numbers=dims, preferred_element_type=F32)
    return d(h, ones_bf16) + d(m, ones_bf16) + d(lo, ones_bf16)


def _mm_exact_lhs(ones_bf16, x, dims=_NN):
    h, m, lo = _split3(x)
    d = functools.partial(lax.dot_general, dimension_numbers=dims, preferred_element_type=F32)
    return d(ones_bf16, h) + d(ones_bf16, m) + d(ones_bf16, lo)


def _selector_matrices(tl, c):
    tok = np.arange(tl)
    same = (tok[:, None] // c) == (tok[None, :] // c)
    lane = np.arange(4 * HEAD)
    src = np.arange(LANES)
    mats = dict(
        seg=(lane[:, None] // HEAD) == (lane[None, :] // HEAD),
        tril=same & (tok[:, None] >= tok[None, :]),
        triu=same & (tok[:, None] <= tok[None, :]),
        last=tok[None, :] == (tok[:, None] // c) * c + c - 1,
        expg=src[:, None] == lane[None, :] // HEAD,
        expb=src[:, None] == lane[None, :] // HEAD + 4)
    return {name: jnp.asarray(m.astype(np.float32), BF16) for name, m in mats.items()}


def _chunking(seq):
    c = min(CHUNK, seq)
    assert c & (c - 1) == 0 and seq % c == 0, seq
    g = min(CHUNKS_PER_STEP, seq // c)
    while (seq // c) % g:
        g -= 1
    return c, g


def _rmsnorm_rows(x, g):
    return x * lax.rsqrt(jnp.mean(x * x, axis=-1, keepdims=True) + NORM_EPS) * g


def _unit_lower_inverses(a_list, c):
    row = _iota((c, c), 0)
    col = _iota((c, c), 1)

    def same_block(shift):
        return jnp.right_shift(row, shift) == jnp.right_shift(col, shift)

    eye = (row == col).astype(F32)
    leaf = same_block(3)
    ns = [jnp.where(leaf, -a, 0.0) for a in a_list]
    ts = [eye + n for n in ns]
    n2s = [_mm(n, n) for n in ns]
    ts = [t + _mm(t, n2) for t, n2 in zip(ts, n2s)]
    n4s = [_mm(n2, n2) for n2 in n2s]
    ts = [t + _mm(t, n4) for t, n4 in zip(ts, n4s)]
    shift = 3
    while (1 << shift) < c:
        off_mask = same_block(shift + 1) & jnp.logical_not(same_block(shift))
        tos = [_mm(t, jnp.where(off_mask, a, 0.0)) for t, a in zip(ts, a_list)]
        ts = [t - _mm(to, t) for t, to in zip(ts, tos)]
        shift += 1
    return ts


_PROJ_A_QKV = 0
_PROJ_A_Z = 768
_PROJ_B_Q = 1024
_PROJ_B_K = 1536
_PROJ_B_V = 2048
_PROJ_C = 2560
_PROJ_A_AB = 3584
_PROJ_WIDTH = 3712


def _proj_kernel(*refs, n_alias):
    x_ref, g_ref, w_ref, cos_ref, sa_ref, sb_ref = refs[:6]
    aqkv_ref, az_ref, aab_ref, q_ref, k_ref, v_ref, kb_ref, vb_ref, c_ref = refs[6 + n_alias:]
    xn = _rmsnorm_rows(x_ref[...], g_ref[...]).astype(BF16)

    def proj(c0, c1):
        return jnp.dot(xn, w_ref[:, c0:c1], preferred_element_type=F32)

    aqkv_ref[...] = proj(_PROJ_A_QKV, _PROJ_A_Z)
    az_ref[...] = proj(_PROJ_A_Z, _PROJ_B_Q)
    cos = cos_ref[...]
    sa = sa_ref[...]
    sb = sb_ref[...]

    def rope(x):
        return x * cos + pltpu.roll(x, LANES - 32, 1) * sa + pltpu.roll(x, 32, 1) * sb

    for h in range(4):
        lo, hi = h * LANES, (h + 1) * LANES
        q = rope(proj(_PROJ_B_Q + lo, _PROJ_B_Q + hi))
        q_ref[:, lo:hi] = (q * ATTN_Q_SCALE).astype(BF16)
        k = rope(proj(_PROJ_B_K + lo, _PROJ_B_K + hi))
        k_ref[:, lo:hi] = k
        kb_ref[:, lo:hi] = k.astype(BF16)
    v = proj(_PROJ_B_V, _PROJ_C)
    v_ref[...] = v
    vb_ref[...] = v.astype(BF16)
    c_ref[...] = proj(_PROJ_C, _PROJ_A_AB)
    aab_ref[...] = proj(_PROJ_A_AB, _PROJ_WIDTH)


def _input_projection(x2d, g_row, w_bf16, rope_tabs, seq_len, tm, layer, depth, kv_stacked):
    t, d = x2d.shape
    nt = t // tm
    cos_t, sa_t, sb_t = rope_tabs
    if tm >= seq_len:
        reps = tm // seq_len
        cos_t, sa_t, sb_t = (jnp.tile(a, (reps, 1)) for a in (cos_t, sa_t, sb_t))
        tab_map = lambda i: (0, 0)
    else:
        per_seq = seq_len // tm
        tab_map = lambda i: (i % per_seq, 0)
    row = lambda i: (i, 0)
    fixed = lambda i: (0, 0)
    tab_spec = pl.BlockSpec((tm, LANES), tab_map)
    widths = [(768, F32), (256, F32), (128, F32), (512, BF16), (512, F32), (512, F32),
              (512, BF16), (512, BF16), (1024, F32)]
    stacked = (4, 5)
    out_specs = [pl.BlockSpec((None, tm, w), lambda i: (layer, i, 0)) if n in stacked
                 else pl.BlockSpec((tm, w), row) for n, (w, _) in enumerate(widths)]
    out_shape = [jax.ShapeDtypeStruct((depth, t, w) if n in stacked else (t, w), dt)
                 for n, (w, dt) in enumerate(widths)]
    in_specs = [pl.BlockSpec((tm, d), row), pl.BlockSpec((1, d), fixed),
                pl.BlockSpec((d, _PROJ_WIDTH), fixed), tab_spec, tab_spec, tab_spec]
    operands = [x2d, g_row, w_bf16, cos_t, sa_t, sb_t]
    aliases = {}
    if kv_stacked is not None:
        for out_idx, arr in zip(stacked, kv_stacked):
            aliases[len(operands)] = out_idx
            in_specs.append(pl.BlockSpec(memory_space=pl.ANY))
            operands.append(arr)
    return pl.pallas_call(
        functools.partial(_proj_kernel, n_alias=len(aliases)),
        grid=(nt,),
        in_specs=in_specs,
        out_specs=out_specs,
        out_shape=out_shape,
        input_output_aliases=aliases,
        compiler_params=pltpu.CompilerParams(dimension_semantics=("parallel",),
                                             vmem_limit_bytes=VMEM_LIMIT),
        name="input_projection",
    )(*operands)


def _delta_kernel(qkv_ref, z_ref, ab_ref, conv0_ref, s0_ref, convw_ref, alog_ref, dtb_ref, anorm_ref,
                  seg_ref, tril_ref, triu_ref, last_ref, expg_ref, expb_ref,
                  mix_ref, convn_ref, sn_ref, xp_ref, s_ref, o_ref, *, c, g, nl):
    l = pl.program_id(1)
    tl = c * g

    @pl.when(l == 0)
    def _():
        xp_ref[5:8, :] = conv0_ref[0]
        s_ref[...] = s0_ref[0]

    x = qkv_ref[0]
    xp_ref[8:8 + tl, :] = x
    w = convw_ref[...]
    y = xp_ref[5:5 + tl, :] * w[0:1]
    y = y + xp_ref[6:6 + tl, :] * w[1:2]
    y = y + xp_ref[7:7 + tl, :] * w[2:3]
    y = y + x * w[3:4]
    tail = xp_ref[5 + tl:8 + tl, :]
    xp_ref[5:8, :] = tail

    @pl.when(l == nl - 1)
    def _():
        convn_ref[0] = tail

    act = y * jax.nn.sigmoid(y)
    seg = seg_ref[...]

    def l2n(t):
        return t * lax.rsqrt(_mm_exact_rhs(t * t, seg) + L2_EPS)

    q = l2n(act[:, 0:256]) * (HEAD ** -0.5)
    k = l2n(act[:, 256:512])
    v = act[:, 512:768]

    ab = ab_ref[0]
    gl = -jnp.exp(alog_ref[...]) * jax.nn.softplus(ab + dtb_ref[...])
    beta_w = _mm_exact_rhs(jax.nn.sigmoid(ab), expb_ref[...])
    g_w = _mm_exact_lhs(tril_ref[...], _mm_exact_rhs(gl, expg_ref[...]))
    g_rows = _mm_exact_rhs(gl, triu_ref[...], _TN)
    g_last_w = _mm_exact_lhs(last_ref[...], g_w)
    eg_w = jnp.exp(g_w)
    v_beta = v * beta_w
    k_beta_eg = k * (beta_w * eg_w)
    q_dec = q * eg_w
    k_tail_w = k * jnp.exp(g_last_w - g_w)
    decay_end_w = jnp.exp(g_last_w)

    row = _iota((c, c), 0)
    col = _iota((c, c), 1)
    lower = row >= col
    strict = row > col

    probs = [(slice(ci * c, (ci + 1) * c), h) for ci in range(g) for h in range(4)]
    heads = lambda h: slice(h * HEAD, (h + 1) * HEAD)
    decay = [jnp.where(lower, jnp.exp(jnp.where(lower, g_w[rows, heads(h)][:, :c] - g_rows[h:h + 1, rows], 0.0)),
                       0.0) for rows, h in probs]
    kh = [k[rows, heads(h)] for rows, h in probs]
    kq = [_mm(jnp.concatenate([kk_, q[rows, heads(h)]], axis=0), kk_, _NT)
          for kk_, (rows, h) in zip(kh, probs)]
    t_inv = _unit_lower_inverses(
        [jnp.where(strict, beta_w[rows, heads(h)][:, :c] * x[:c] * d, 0.0)
         for x, d, (rows, h) in zip(kq, decay, probs)], c)
    sol = [_mm(t, jnp.concatenate([v_beta[rows, heads(h)], k_beta_eg[rows, heads(h)]], axis=1))
           for t, (rows, h) in zip(t_inv, probs)]
    qk = [jnp.where(lower, x[c:] * d, 0.0) for x, d in zip(kq, decay)]
    kt_sol = [_mm(k_tail_w[rows, heads(h)], so, _TN) for so, (rows, h) in zip(sol, probs)]
    qk_sol = [_mm(x, so) for x, so in zip(qk, sol)]
    q_eff = [q_dec[rows, heads(h)] - x[:, HEAD:] for x, (rows, h) in zip(qk_sol, probs)]
    decay_end = [decay_end_w[rows.stop - 1:rows.stop, heads(h)] for rows, h in probs]

    states = [s_ref[h] for h in range(4)]
    for ci in range(g):
        rows = slice(ci * c, (ci + 1) * c)
        idx = [ci * 4 + h for h in range(4)]
        o_new = [_mm(q_eff[n], states[h]) + qk_sol[n][:, :HEAD] for h, n in enumerate(idx)]
        states = [states[h] * decay_end[n] + kt_sol[n][:, :HEAD] - _mm(kt_sol[n][:, HEAD:], states[h])
                  for h, n in enumerate(idx)]
        for h in range(4):
            o_ref[rows, heads(h)] = o_new[h]
    for h in range(4):
        s_ref[h] = states[h]

    o = o_ref[...]
    ms = _mm_exact_rhs(o * o, seg) * (1.0 / HEAD)
    z = z_ref[0]
    ao = (o * lax.rsqrt(ms + NORM_EPS) * anorm_ref[...]) * (z * jax.nn.sigmoid(z))
    mix_ref[0] = ao.astype(BF16)

    @pl.when(l == nl - 1)
    def _():
        sn_ref[0] = s_ref[...]


def _gated_delta(a_qkv, a_z, a_ab, conv0, s0, conv_w, alog_row, dtb_row, anorm_row):
    b, seq, _ = a_qkv.shape
    c, g = _chunking(seq)
    tl = c * g
    nl = seq // tl
    tile = lambda i, l: (i, l, 0)
    per_b3 = lambda i, l: (i, 0, 0)
    per_b4 = lambda i, l: (i, 0, 0, 0)
    fixed = lambda i, l: (0, 0)
    sel = _selector_matrices(tl, c)
    consts = [sel[name] for name in ("seg", "tril", "triu", "last", "expg", "expb")]
    return pl.pallas_call(
        functools.partial(_delta_kernel, c=c, g=g, nl=nl),
        grid=(b, nl),
        in_specs=[pl.BlockSpec((1, tl, 768), tile), pl.BlockSpec((1, tl, 256), tile),
                  pl.BlockSpec((1, tl, LANES), tile), pl.BlockSpec((1, 3, 768), per_b3),
                  pl.BlockSpec((1, 4, HEAD, HEAD), per_b4), pl.BlockSpec((4, 768), fixed),
                  pl.BlockSpec((1, LANES), fixed), pl.BlockSpec((1, LANES), fixed),
                  pl.BlockSpec((1, 256), fixed)] + [pl.BlockSpec(m.shape, fixed) for m in consts],
        out_specs=[pl.BlockSpec((1, tl, 256), tile), pl.BlockSpec((1, 3, 768), per_b3),
                   pl.BlockSpec((1, 4, HEAD, HEAD), per_b4)],
        out_shape=[jax.ShapeDtypeStruct((b, seq, 256), BF16),
                   jax.ShapeDtypeStruct((b, 3, 768), F32),
                   jax.ShapeDtypeStruct((b, 4, HEAD, HEAD), F32)],
        scratch_shapes=[pltpu.VMEM((tl + 8, 768), F32), pltpu.VMEM((4, HEAD, HEAD), F32),
                        pltpu.VMEM((tl, 256), F32)],
        compiler_params=pltpu.CompilerParams(dimension_semantics=("parallel", "arbitrary"),
                                             vmem_limit_bytes=VMEM_LIMIT),
        name="gated_delta",
    )(a_qkv, a_z, a_ab, conv0, s0, conv_w, alog_row, dtb_row, anorm_row, *consts)


def _rwkv_kernel(c_ref, shift0_ref, s0_ref, mu_ref, w0_ref, wup_ref, a0_ref, aup_ref, gup_ref,
                 kk_ref, ka_ref, rk_ref, lnw_ref, lnb_ref, seg_ref, tril_ref,
                 mix_ref, shiftn_ref, sn_ref, xp_ref, s_ref, y_ref, *, c, g, nl):
    l = pl.program_id(1)
    tl = c * g

    @pl.when(l == 0)
    def _():
        xp_ref[7:8, :] = shift0_ref[0]
        s_ref[...] = s0_ref[0]

    raw = c_ref[0]
    xp_ref[8:8 + tl, :] = raw
    prev = xp_ref[7:7 + tl, :]
    last = raw[tl - 1:tl, :]
    xp_ref[7:8, :] = last

    @pl.when(l == nl - 1)
    def _():
        shiftn_ref[0] = last

    x = raw + (prev - raw) * mu_ref[...]
    cr, ck, cv = x[:, 0:256], x[:, 256:512], x[:, 512:768]
    c_wa = x[:, 768:896]
    c_g = x[:, 896:1024]
    w_log = -jnp.exp(-jax.nn.softplus(-(w0_ref[...] + _mm(jnp.tanh(c_wa), wup_ref[...])))
                     - RWKV_DECAY_OFFSET)
    ca = jax.nn.sigmoid(a0_ref[...] + _mm(c_wa, aup_ref[...]))
    cg = _mm(jax.nn.sigmoid(c_g), gup_ref[...])
    seg = seg_ref[...]
    kkv = ck * kk_ref[...]
    kk = kkv * lax.rsqrt(_mm_exact_rhs(kkv * kkv, seg) + L2_EPS)
    ck = ck * (1.0 + (ca - 1.0) * ka_ref[...])

    g_cum = _mm_exact_lhs(tril_ref[...], w_log)
    e_pos = jnp.exp(g_cum)
    e_neg = jnp.exp(-g_cum)
    a_t = -kk * jnp.exp(g_cum - w_log)
    b_t = (kk * ca) * e_neg
    k_t = ck * e_neg
    r_t = cr * e_pos

    row = _iota((c, c), 0)
    col = _iota((c, c), 1)
    lower = row >= col
    strict = row > col

    probs = [(slice(ci * c, (ci + 1) * c), h) for ci in range(g) for h in range(4)]
    heads = lambda h: slice(h * HEAD, (h + 1) * HEAD)
    bh = [b_t[rows, heads(h)] for rows, h in probs]
    kh = [k_t[rows, heads(h)] for rows, h in probs]
    vh = [cv[rows, heads(h)] for rows, h in probs]
    ar = [jnp.concatenate([a_t[rows, heads(h)], r_t[rows, heads(h)]], axis=0) for rows, h in probs]
    pb = [_mm(x, y_, _NT) for x, y_ in zip(ar, bh)]
    pk = [_mm(x, y_, _NT) for x, y_ in zip(ar, kh)]
    t_inv = _unit_lower_inverses([-jnp.where(strict, x[:c], 0.0) for x in pb], c)
    n_rb = [jnp.where(lower, x[c:], 0.0) for x in pb]
    mn = [_mm(jnp.concatenate([jnp.where(strict, x[:c], 0.0), jnp.where(lower, x[c:], 0.0)], axis=0), vv)
          for x, vv in zip(pk, vh)]
    bk = [jnp.concatenate([x, y_], axis=0) for x, y_ in zip(bh, kh)]
    g_end = [e_pos[rows.stop - 1:rows.stop, heads(h)] for rows, h in probs]
    ta = [_mm(t, jnp.concatenate([x[:c], m_[:c]], axis=1)) for t, x, m_ in zip(t_inv, ar, mn)]
    p_mat = [_mm(x[:, :HEAD], y_, _TN) for x, y_ in zip(ta, bh)]
    c_mat = [_mm(jnp.concatenate([x[:, HEAD:], vv], axis=0), y_, _TN) for x, vv, y_ in zip(ta, vh, bk)]
    nr = [_mm(x, y_) for x, y_ in zip(n_rb, ta)]
    r_eff = [x[c:] + y_[:, :HEAD] for x, y_ in zip(ar, nr)]
    y_off = [x[:, HEAD:] + m_[c:] for x, m_ in zip(nr, mn)]

    states = [s_ref[h] for h in range(4)]
    for ci in range(g):
        rows = slice(ci * c, (ci + 1) * c)
        idx = [ci * 4 + h for h in range(4)]
        y_new = [_mm(r_eff[n], states[h], _NT) + y_off[n] for h, n in enumerate(idx)]
        states = [(states[h] + _mm(states[h], p_mat[n]) + c_mat[n]) * g_end[n] for h, n in enumerate(idx)]
        for h in range(4):
            y_ref[rows, heads(h)] = y_new[h]
    for h in range(4):
        s_ref[h] = states[h]

    y = y_ref[...]
    mean = _mm_exact_rhs(y, seg) * (1.0 / HEAD)
    yc = y - mean
    var = _mm_exact_rhs(yc * yc, seg) * (1.0 / HEAD)
    cy = (yc * lax.rsqrt(var + C_LN_EPS)) * lnw_ref[...] + lnb_ref[...]
    bonus = _mm_exact_rhs(cr * ck * rk_ref[...], seg) * cv
    mix_ref[0] = ((cy + bonus) * cg).astype(BF16)

    @pl.when(l == nl - 1)
    def _():
        sn_ref[0] = s_ref[...]


def _rwkv7(c_raw, shift0, s0, params):
    b, seq, width = c_raw.shape
    c, g = _chunking(seq)
    tl = c * g
    nl = seq // tl
    tile = lambda i, l: (i, l, 0)
    per_b3 = lambda i, l: (i, 0, 0)
    per_b4 = lambda i, l: (i, 0, 0, 0)
    fixed = lambda i, l: (0, 0)
    sel = _selector_matrices(tl, c)
    params = tuple(params) + (sel["seg"], sel["tril"])
    param_specs = [pl.BlockSpec(p.shape, fixed) for p in params]
    return pl.pallas_call(
        functools.partial(_rwkv_kernel, c=c, g=g, nl=nl),
        grid=(b, nl),
        in_specs=[pl.BlockSpec((1, tl, width), tile), pl.BlockSpec((1, 1, width), per_b3),
                  pl.BlockSpec((1, 4, HEAD, HEAD), per_b4)] + param_specs,
        out_specs=[pl.BlockSpec((1, tl, 256), tile), pl.BlockSpec((1, 1, width), per_b3),
                   pl.BlockSpec((1, 4, HEAD, HEAD), per_b4)],
        out_shape=[jax.ShapeDtypeStruct((b, seq, 256), BF16),
                   jax.ShapeDtypeStruct((b, 1, width), F32),
                   jax.ShapeDtypeStruct((b, 4, HEAD, HEAD), F32)],
        scratch_shapes=[pltpu.VMEM((tl + 8, width), F32), pltpu.VMEM((4, HEAD, HEAD), F32),
                        pltpu.VMEM((tl, 256), F32)],
        compiler_params=pltpu.CompilerParams(dimension_semantics=("parallel", "arbitrary"),
                                             vmem_limit_bytes=VMEM_LIMIT),
        name="rwkv7",
    )(c_raw, shift0, s0, *params)


def _attn_kernel(i_tab, j_tab, flag_tab, q_ref, k_ref, v_ref, lam_ref, bn_ref, o_ref,
                 m_ref, acc_ref, *, tq, tk, past, lk_true, lam_init):
    p = pl.program_id(1)
    i = i_tab[p]
    j = j_tab[p]

    @pl.when(j == 0)
    def _():
        m_ref[...] = jnp.full(m_ref.shape, MASK_VALUE, F32)
        acc_ref[...] = jnp.zeros(acc_ref.shape, F32)

    nslab = tk // LANES

    def step(masked):
        if masked:
            q_pos = past + i * tq + _iota((tq, tk), 0)
            k_pos = j * tk + _iota((tq, tk), 1)
            visible = (jnp.right_shift(k_pos, 6) <= jnp.right_shift(q_pos, 6)) & (k_pos < lk_true)
        q = q_ref[0]
        k = k_ref[0]
        v = v_ref[0]

        def scores(hm):
            sl = slice(hm * HEAD, (hm + 1) * HEAD)
            return lax.dot_general(q[:, sl], k[:, sl], _NT, preferred_element_type=F32)

        ones = jnp.ones((tk, LANES), BF16)
        s_next = scores(0)
        for hm in range(8):
            h = hm // 2
            s = s_next
            if hm + 1 < 8:
                s_next = scores(hm + 1)
            if masked:
                s = jnp.where(visible, s, MASK_VALUE)
            slabs = [s[:, n * LANES:(n + 1) * LANES] for n in range(nslab)]
            m_cur = functools.reduce(jnp.maximum, slabs)
            m_prev = m_ref[hm]
            m_new = jnp.maximum(m_prev, jnp.max(m_cur, axis=1, keepdims=True))
            alpha = jnp.exp2(m_prev - m_new)
            e = jnp.concatenate([jnp.exp2(sb - m_new).astype(BF16) for sb in slabs], axis=1)
            v_ones = jnp.concatenate([v[:, h * LANES:(h + 1) * LANES], ones], axis=1)
            acc_ref[hm] = (jnp.concatenate([alpha, alpha], axis=1) * acc_ref[hm]
                           + jnp.dot(e, v_ones, preferred_element_type=F32))
            m_ref[hm] = m_new

    needs_mask = jnp.bitwise_and(flag_tab[p], 2) != 0

    @pl.when(needs_mask)
    def _():
        step(True)

    @pl.when(jnp.logical_not(needs_mask))
    def _():
        step(False)

    @pl.when(jnp.bitwise_and(flag_tab[p], 1) != 0)
    def _():
        lp = lam_ref[...]
        lam = (jnp.exp(jnp.sum(lp[0:1] * lp[1:2], axis=1, keepdims=True))
               - jnp.exp(jnp.sum(lp[2:3] * lp[3:4], axis=1, keepdims=True)) + lam_init)
        for h in range(4):
            a1 = acc_ref[2 * h]
            a2 = acc_ref[2 * h + 1]
            o = a1[:, :LANES] / a1[:, LANES:] - lam * (a2[:, :LANES] / a2[:, LANES:])
            o = _rmsnorm_rows(o, bn_ref[...]) * (1.0 - lam_init)
            o_ref[0, :, h * LANES:(h + 1) * LANES] = o.astype(BF16)


def _diff_attention(q, k, v, lam_params, bnorm_row, past, lam_init, tq, tk):
    b, lq, width = q.shape
    lk_true = k.shape[1]
    nk = -(-lk_true // tk)
    if nk * tk != lk_true:
        pad = ((0, 0), (0, nk * tk - lk_true), (0, 0))
        k = jnp.pad(k, pad)
        v = jnp.pad(v, pad)
    nq = lq // tq
    i_list, j_list, flag_list = [], [], []
    for i in range(nq):
        first_pos = past + i * tq
        last_pos = first_pos + tq - 1
        j_max = min(((last_pos // CHUNK + 1) * CHUNK - 1) // tk, nk - 1)
        for j in range(j_max + 1):
            key_end = (j + 1) * tk
            all_visible = (key_end - 1) // CHUNK <= first_pos // CHUNK and key_end <= lk_true
            i_list.append(i)
            j_list.append(j)
            flag_list.append((1 if j == j_max else 0) + (0 if all_visible else 2))
    tabs = [jnp.asarray(np.asarray(t, np.int32)) for t in (i_list, j_list, flag_list)]
    grid_spec = pltpu.PrefetchScalarGridSpec(
        num_scalar_prefetch=3,
        grid=(b, len(i_list)),
        in_specs=[pl.BlockSpec((1, tq, width), lambda bi, p, it, jt, lt: (bi, it[p], 0)),
                  pl.BlockSpec((1, tk, width), lambda bi, p, it, jt, lt: (bi, jt[p], 0)),
                  pl.BlockSpec((1, tk, width), lambda bi, p, it, jt, lt: (bi, jt[p], 0)),
                  pl.BlockSpec((4, HEAD), lambda bi, p, it, jt, lt: (0, 0)),
                  pl.BlockSpec((1, LANES), lambda bi, p, it, jt, lt: (0, 0))],
        out_specs=pl.BlockSpec((1, tq, width), lambda bi, p, it, jt, lt: (bi, it[p], 0)),
        scratch_shapes=[pltpu.VMEM((8, tq, LANES), F32), pltpu.VMEM((8, tq, 2 * LANES), F32)],
    )
    return pl.pallas_call(
        functools.partial(_attn_kernel, tq=tq, tk=tk, past=past, lk_true=lk_true, lam_init=lam_init),
        grid_spec=grid_spec,
        out_shape=jax.ShapeDtypeStruct((b, lq, width), BF16),
        compiler_params=pltpu.CompilerParams(dimension_semantics=("parallel", "arbitrary"),
                                             vmem_limit_bytes=VMEM_LIMIT),
        name="diff_attention",
    )(*tabs, q, k, v, lam_params, bnorm_row)


def _mlp_kernel(h_ref, ma_ref, mb_ref, mc_ref, wo_ref, nf_ref, w1_ref, w2_ref, np_ref, wg_ref,
                wp_ref, p_ref, nfin_ref, o_ref, h1_ref, xn_ref, acc_ref, *, nf, final_norm):
    kf = pl.program_id(1)

    @pl.when(kf == 0)
    def _():
        mixed = (jnp.dot(ma_ref[...], wo_ref[0:256, :], preferred_element_type=F32)
                 + jnp.dot(mb_ref[...], wo_ref[256:768, :], preferred_element_type=F32)
                 + jnp.dot(mc_ref[...], wo_ref[768:1024, :], preferred_element_type=F32))
        h1 = h_ref[...] + mixed
        h1_ref[...] = h1
        xn_ref[...] = _rmsnorm_rows(h1, nf_ref[...]).astype(BF16)
        acc_ref[...] = jnp.zeros(acc_ref.shape, F32)

    u = jnp.maximum(jnp.dot(xn_ref[...], w1_ref[...], preferred_element_type=F32), 0.0)
    acc_ref[...] += jnp.dot((u * u).astype(BF16), w2_ref[...], preferred_element_type=F32)

    @pl.when(kf == nf - 1)
    def _():
        h2 = h1_ref[...] + acc_ref[...]
        gate = jax.nn.sigmoid(jnp.dot(_rmsnorm_rows(h2, np_ref[...]).astype(BF16), wg_ref[...],
                                      preferred_element_type=F32))
        h3 = h2 + gate * jnp.dot(p_ref[...].astype(BF16), wp_ref[...], preferred_element_type=F32)
        if final_norm:
            h3 = _rmsnorm_rows(h3, nfin_ref[...])
        o_ref[...] = h3


def _mix_mlp(h2d, mix_a, mix_b, mix_c, w_out, norm_ffn, w_ff1, w_ff2, norm_ple, w_gate, w_proj,
             p2d, norm_final, final_norm, tm, tf):
    t, d = h2d.shape
    dff = w_ff1.shape[1]
    nt, nf = t // tm, dff // tf
    row = lambda i, kf: (i, 0)
    fixed = lambda i, kf: (0, 0)
    return pl.pallas_call(
        functools.partial(_mlp_kernel, nf=nf, final_norm=final_norm),
        grid=(nt, nf),
        in_specs=[pl.BlockSpec((tm, d), row), pl.BlockSpec((tm, 256), row),
                  pl.BlockSpec((tm, 512), row), pl.BlockSpec((tm, 256), row),
                  pl.BlockSpec((d, d), fixed), pl.BlockSpec((1, d), fixed),
                  pl.BlockSpec((d, tf), lambda i, kf: (0, kf)),
                  pl.BlockSpec((tf, d), lambda i, kf: (kf, 0)),
                  pl.BlockSpec((1, d), fixed), pl.BlockSpec((d, d), fixed),
                  pl.BlockSpec((p2d.shape[1], d), fixed), pl.BlockSpec((tm, p2d.shape[1]), row),
                  pl.BlockSpec((1, d), fixed)],
        out_specs=pl.BlockSpec((tm, d), row),
        out_shape=jax.ShapeDtypeStruct((t, d), F32),
        scratch_shapes=[pltpu.VMEM((tm, d), F32), pltpu.VMEM((tm, d), BF16), pltpu.VMEM((tm, d), F32)],
        compiler_params=pltpu.CompilerParams(dimension_semantics=("parallel", "arbitrary"),
                                             vmem_limit_bytes=VMEM_LIMIT),
        name="mix_mlp",
    )(h2d, mix_a, mix_b, mix_c, w_out, norm_ffn, w_ff1, w_ff2, norm_ple, w_gate, w_proj, p2d,
      norm_final)


def _rope_tables(past, seq_len):
    half = HEAD // 2
    inv = ROPE_THETA ** (-2.0 * jnp.arange(half, dtype=F32) / HEAD)
    pos = past + jnp.arange(seq_len, dtype=jnp.int32)
    ang = pos.astype(F32)[:, None] * inv[None, :]
    cos, sin, zero = jnp.cos(ang), jnp.sin(ang), jnp.zeros_like(ang)
    rep = LANES // HEAD
    cos_t = jnp.tile(jnp.concatenate([cos, cos], axis=1), (1, rep))
    sa_t = jnp.tile(jnp.concatenate([-sin, zero], axis=1), (1, rep))
    sb_t = jnp.tile(jnp.concatenate([zero, sin], axis=1), (1, rep))
    return cos_t, sa_t, sb_t


def _pad_lanes(v, width=LANES):
    return jnp.pad(v, (0, width - v.shape[0]))[None, :]


def _prep_layer(i, norm_mix, w_in, a_conv_w, a_A_log, a_dt_bias, a_norm,
                b_lam_q1, b_lam_k1, b_lam_q2, b_lam_k2, b_norm,
                c_mu, c_w0, c_w_up, c_a0, c_a_up, c_g_up, c_k_k, c_k_a, c_r_k, c_ln_w, c_ln_b,
                w_out, norm_ffn, w_ff1, w_ff2, norm_ple, w_ple_gate, w_ple_proj):
    w = w_in[i]
    d = w.shape[0]
    w_perm = jnp.concatenate(
        [w[:, 0:1024], w[:, 1032:3592], w[:, 1024:1032],
         jnp.zeros((d, _PROJ_WIDTH - 3592), w.dtype)], axis=1).astype(BF16)
    zeros_rank = jnp.zeros_like(c_w_up[i])
    rwkv = (c_mu[i][None, :], c_w0[i][None, :],
            jnp.concatenate([c_w_up[i], zeros_rank], axis=0).astype(BF16), c_a0[i][None, :],
            jnp.concatenate([jnp.zeros_like(c_a_up[i]), c_a_up[i]], axis=0).astype(BF16),
            c_g_up[i].astype(BF16), c_k_k[i][None, :], c_k_a[i][None, :],
            c_r_k[i].reshape(1, -1), c_ln_w[i][None, :], c_ln_b[i][None, :])
    return dict(
        norm_mix=norm_mix[i][None, :], w_in=w_perm, conv_w=a_conv_w[i],
        alog=_pad_lanes(a_A_log[i]), dtb=_pad_lanes(a_dt_bias[i]),
        anorm=jnp.tile(a_norm[i], 4)[None, :],
        lam=jnp.stack([b_lam_q1[i], b_lam_k1[i], b_lam_q2[i], b_lam_k2[i]], axis=0),
        bnorm=b_norm[i][None, :], rwkv=rwkv,
        w_out=w_out[i].astype(BF16), norm_ffn=norm_ffn[i][None, :],
        w_ff1=w_ff1[i].astype(BF16), w_ff2=w_ff2[i].astype(BF16),
        norm_ple=norm_ple[i][None, :], w_gate=w_ple_gate[i].astype(BF16),
        w_proj=w_ple_proj[i].astype(BF16))


def _pick_tile(n, target):
    t = min(n, target)
    while n % t:
        t //= 2
    return t


def _trunk(x, p, cache_k, cache_v, conv_buf, delta_s, shift_prev, wkv_s, layers, norm_final):
    b, seq, d = x.shape
    depth = len(layers)
    past = cache_k.shape[2]
    t = b * seq
    tm = _pick_tile(t, PROJ_TM)
    rope_tabs = _rope_tables(past, seq)
    h = x.reshape(t, d)
    states = []
    kv_stacked = None
    for i, lp in enumerate(layers):
        a_qkv, a_z, a_ab, q_b, k_f, v_f, k_b, v_b, c_raw = _input_projection(
            h, lp["norm_mix"], lp["w_in"], rope_tabs, seq, tm, i, depth, kv_stacked)
        kv_stacked = (k_f, v_f)
        mix_a, conv_n, delta_n = _gated_delta(
            a_qkv.reshape(b, seq, -1), a_z.reshape(b, seq, -1), a_ab.reshape(b, seq, -1),
            conv_buf[i], delta_s[i], lp["conv_w"], lp["alog"], lp["dtb"], lp["anorm"])
        k_all = k_b.reshape(b, seq, -1)
        v_all = v_b.reshape(b, seq, -1)
        if past:
            k_all = jnp.concatenate([cache_k[i].reshape(b, past, -1).astype(BF16), k_all], axis=1)
            v_all = jnp.concatenate([cache_v[i].reshape(b, past, -1).astype(BF16), v_all], axis=1)
        lam_init = 0.8 - 0.6 * math.exp(-0.3 * i)
        mix_b = _diff_attention(q_b.reshape(b, seq, -1), k_all, v_all, lp["lam"], lp["bnorm"],
                                past, lam_init, _pick_tile(seq, ATTN_TQ),
                                ATTN_TK if past + seq >= 8 * ATTN_TK else ATTN_TK // 2)
        mix_c, shift_n, wkv_n = _rwkv7(c_raw.reshape(b, seq, -1), shift_prev[i][:, None, :],
                                       wkv_s[i], lp["rwkv"])
        h = _mix_mlp(h, mix_a.reshape(t, -1), mix_b.reshape(t, -1), mix_c.reshape(t, -1),
                     lp["w_out"], lp["norm_ffn"], lp["w_ff1"], lp["w_ff2"], lp["norm_ple"],
                     lp["w_gate"], lp["w_proj"], p[i].reshape(t, -1), norm_final[None, :],
                     i == depth - 1, _pick_tile(t, MLP_TM), MLP_TF)
        states.append((conv_n, delta_n, shift_n[:, 0, :], wkv_n))
    conv_all, delta_all, shift_all, wkv_all = (
        jnp.stack([st[j] for st in states], axis=0) for j in range(4))
    k_all_layers, v_all_layers = (a.reshape(depth, b, seq, 4, LANES) for a in kv_stacked)
    return h.reshape(b, seq, d), [conv_all, delta_all, k_all_layers, v_all_layers, shift_all, wkv_all]


def kernel(x_prompt, x_sample, cache_b_k, cache_b_v, state_a_conv, state_a_delta, state_c_shift, state_c_wkv, p_prompt, p_sample, norm_mix, w_in, a_conv_w, a_A_log, a_dt_bias, a_norm, b_lam_q1, b_lam_k1, b_lam_q2, b_lam_k2, b_norm, c_mu, c_w0, c_w_up, c_a0, c_a_up, c_g_up, c_k_k, c_k_a, c_r_k, c_ln_w, c_ln_b, w_out, norm_ffn, w_ff1, w_ff2, norm_ple, w_ple_gate, w_ple_proj, norm_final):
    depth = w_in.shape[0]
    layers = [_prep_layer(i, norm_mix, w_in, a_conv_w, a_A_log, a_dt_bias, a_norm,
                          b_lam_q1, b_lam_k1, b_lam_q2, b_lam_k2, b_norm,
                          c_mu, c_w0, c_w_up, c_a0, c_a_up, c_g_up, c_k_k, c_k_a, c_r_k, c_ln_w, c_ln_b,
                          w_out, norm_ffn, w_ff1, w_ff2, norm_ple, w_ple_gate, w_ple_proj)
              for i in range(depth)]
    bp = x_prompt.shape[0]
    dt = x_prompt.dtype
    zeros = lambda ref: jnp.zeros((depth, bp) + ref.shape[2:], dt)
    empty_k = jnp.zeros((depth, bp, 0) + cache_b_k.shape[3:], dt)
    empty_v = jnp.zeros((depth, bp, 0) + cache_b_v.shape[3:], dt)
    y_prompt, st_p = _trunk(x_prompt, p_prompt, empty_k, empty_v, zeros(state_a_conv),
                            zeros(state_a_delta), zeros(state_c_shift), zeros(state_c_wkv),
                            layers, norm_final)
    y_sample, st_s = _trunk(x_sample, p_sample, cache_b_k, cache_b_v, state_a_conv, state_a_delta,
                            state_c_shift, state_c_wkv, layers, norm_final)
    return (y_prompt, y_sample, *st_p, *st_s)
```

```python
import functools
import math

import numpy as np
import jax
import jax.numpy as jnp
from jax import lax
from jax.experimental import pallas as pl
from jax.experimental.pallas import tpu as pltpu

F32 = jnp.float32
BF16 = jnp.bfloat16

CHUNK = 64
ROPE_THETA = 10000.0
NORM_EPS = 1e-6
L2_EPS = 1e-6
C_LN_EPS = 64e-5
RWKV_DECAY_OFFSET = 0.5
MASK_VALUE = float(np.finfo(np.float32).min)

LANES = 128
HEAD = 64
VMEM_LIMIT = 56 * 1024 * 1024
PROJ_TM = 512
MLP_TM = 512
MLP_TF = 1024
CHUNKS_PER_STEP = 8
ATTN_TQ = 512
ATTN_TK = 1024
ATTN_Q_SCALE = (64 ** -0.5) * math.log2(math.e)


def _iota(shape, axis):
    return lax.broadcasted_iota(jnp.int32, shape, axis)


_NN = (((1,), (0,)), ((), ()))
_NT = (((1,), (1,)), ((), ()))
_TN = (((0,), (0,)), ((), ()))


def _mm(a, b, dims=_NN):
    return lax.dot_general(a.astype(BF16), b.astype(BF16), dims, preferred_element_type=F32)


def _split3(x):
    h = x.astype(BF16)
    r = x - h.astype(F32)
    m = r.astype(BF16)
    lo = (r - m.astype(F32)).astype(BF16)
    return h, m, lo


def _mm_exact_rhs(x, ones_bf16, dims=_NN):
    h, m, lo = _split3(x)
    d = functools.partial(lax.dot_general, dimension_numbers=dims, preferred_element_type=F32)
    return d(h, ones_bf16) + d(m, ones_bf16) + d(lo, ones_bf16)


def _mm_exact_lhs(ones_bf16, x, dims=_NN):
    h, m, lo = _split3(x)
    d = functools.partial(lax.dot_general, dimension_numbers=dims, preferred_element_type=F32)
    return d(ones_bf16, h) + d(ones_bf16, m) + d(ones_bf16, lo)


def _selector_matrices(tl, c):
    tok = np.arange(tl)
    same = (tok[:, None] // c) == (tok[None, :] // c)
    lane = np.arange(4 * HEAD)
    src = np.arange(LANES)
    mats = dict(
        seg=(lane[:, None] // HEAD) == (lane[None, :] // HEAD),
        tril=same & (tok[:, None] >= tok[None, :]),
        triu=same & (tok[:, None] <= tok[None, :]),
        last=tok[None, :] == (tok[:, None] // c) * c + c - 1,
        expg=src[:, None] == lane[None, :] // HEAD,
        expb=src[:, None] == lane[None, :] // HEAD + 4)
    return {name: jnp.asarray(m.astype(np.float32), BF16) for name, m in mats.items()}


def _chunking(seq):
    c = min(CHUNK, seq)
    assert c & (c - 1) == 0 and seq % c == 0, seq
    g = min(CHUNKS_PER_STEP, seq // c)
    while (seq // c) % g:
        g -= 1
    return c, g


def _rmsnorm_rows(x, g):
    return x * lax.rsqrt(jnp.mean(x * x, axis=-1, keepdims=True) + NORM_EPS) * g


def _unit_lower_inverses(a_list, c):
    row = _iota((c, c), 0)
    col = _iota((c, c), 1)

    def same_block(shift):
        return jnp.right_shift(row, shift) == jnp.right_shift(col, shift)

    eye = (row == col).astype(F32)
    leaf = same_block(3)
    ns = [jnp.where(leaf, -a, 0.0) for a in a_list]
    ts = [eye + n for n in ns]
    n2s = [_mm(n, n) for n in ns]
    ts = [t + _mm(t, n2) for t, n2 in zip(ts, n2s)]
    n4s = [_mm(n2, n2) for n2 in n2s]
    ts = [t + _mm(t, n4) for t, n4 in zip(ts, n4s)]
    shift = 3
    while (1 << shift) < c:
        off_mask = same_block(shift + 1) & jnp.logical_not(same_block(shift))
        tos = [_mm(t, jnp.where(off_mask, a, 0.0)) for t, a in zip(ts, a_list)]
        ts = [t - _mm(to, t) for t, to in zip(ts, tos)]
        shift += 1
    return ts


_PROJ_A_QKV = 0
_PROJ_A_Z = 768
_PROJ_B_Q = 1024
_PROJ_B_K = 1536
_PROJ_B_V = 2048
_PROJ_C = 2560
_PROJ_A_AB = 3584
_PROJ_WIDTH = 3712


def _proj_kernel(*refs, n_alias):
    x_ref, g_ref, w_ref, cos_ref, sa_ref, sb_ref = refs[:6]
    aqkv_ref, az_ref, aab_ref, q_ref, k_ref, v_ref, kb_ref, vb_ref, c_ref = refs[6 + n_alias:]
    xn = _rmsnorm_rows(x_ref[...], g_ref[...]).astype(BF16)

    def proj(c0, c1):
        return jnp.dot(xn, w_ref[:, c0:c1], preferred_element_type=F32)

    a_all = proj(_PROJ_A_QKV, _PROJ_B_Q)
    aqkv_ref[...] = a_all[:, :_PROJ_A_Z]
    az_ref[...] = a_all[:, _PROJ_A_Z:]
    cos = cos_ref[...]
    sa = sa_ref[...]
    sb = sb_ref[...]

    def rope(x):
        return x * cos + pltpu.roll(x, LANES - 32, 1) * sa + pltpu.roll(x, 32, 1) * sb

    q_all = proj(_PROJ_B_Q, _PROJ_B_K)
    k_all = proj(_PROJ_B_K, _PROJ_B_V)
    for h in range(4):
        lo, hi = h * LANES, (h + 1) * LANES
        q_ref[:, lo:hi] = (rope(q_all[:, lo:hi]) * ATTN_Q_SCALE).astype(BF16)
        k = rope(k_all[:, lo:hi])
        k_ref[:, h, :] = k
        kb_ref[:, lo:hi] = k.astype(BF16)
    v = proj(_PROJ_B_V, _PROJ_C)
    for h in range(4):
        v_ref[:, h, :] = v[:, h * LANES:(h + 1) * LANES]
    vb_ref[...] = v.astype(BF16)
    c_all = proj(_PROJ_C, _PROJ_WIDTH)
    c_ref[...] = c_all[:, :_PROJ_A_AB - _PROJ_C]
    aab_ref[...] = c_all[:, _PROJ_A_AB - _PROJ_C:]


def _input_projection(x2d, g_row, w_bf16, rope_tabs, seq_len, tm, layer, depth, kv_stacked):
    t, d = x2d.shape
    nt = t // tm
    cos_t, sa_t, sb_t = rope_tabs
    if tm >= seq_len:
        reps = tm // seq_len
        cos_t, sa_t, sb_t = (jnp.tile(a, (reps, 1)) for a in (cos_t, sa_t, sb_t))
        tab_map = lambda i: (0, 0)
    else:
        per_seq = seq_len // tm
        tab_map = lambda i: (i % per_seq, 0)
    row = lambda i: (i, 0)
    fixed = lambda i: (0, 0)
    tab_spec = pl.BlockSpec((tm, LANES), tab_map)
    widths = [(768, F32), (256, F32), (128, F32), (512, BF16), (512, F32), (512, F32),
              (512, BF16), (512, BF16), (1024, F32)]
    stacked = (4, 5)
    out_specs = [pl.BlockSpec((None, tm, 4, LANES), lambda i: (layer, i, 0, 0)) if n in stacked
                 else pl.BlockSpec((tm, w), row) for n, (w, _) in enumerate(widths)]
    out_shape = [jax.ShapeDtypeStruct((depth, t, 4, LANES) if n in stacked else (t, w), dt)
                 for n, (w, dt) in enumerate(widths)]
    in_specs = [pl.BlockSpec((tm, d), row), pl.BlockSpec((1, d), fixed),
                pl.BlockSpec((d, _PROJ_WIDTH), fixed), tab_spec, tab_spec, tab_spec]
    operands = [x2d, g_row, w_bf16, cos_t, sa_t, sb_t]
    aliases = {}
    if kv_stacked is not None:
        for out_idx, arr in zip(stacked, kv_stacked):
            aliases[len(operands)] = out_idx
            in_specs.append(pl.BlockSpec(memory_space=pl.ANY))
            operands.append(arr)
    return pl.pallas_call(
        functools.partial(_proj_kernel, n_alias=len(aliases)),
        grid=(nt,),
        in_specs=in_specs,
        out_specs=out_specs,
        out_shape=out_shape,
        input_output_aliases=aliases,
        compiler_params=pltpu.CompilerParams(dimension_semantics=("parallel",),
                                             vmem_limit_bytes=VMEM_LIMIT),
        name="input_projection",
    )(*operands)


def _delta_kernel(qkv_ref, z_ref, ab_ref, conv0_ref, s0_ref, convw_ref, alog_ref, dtb_ref, anorm_ref,
                  seg_ref, tril_ref, triu_ref, last_ref, expg_ref, expb_ref,
                  mix_ref, convn_ref, sn_ref, xp_ref, s_ref, o_ref, *, c, g, nl):
    l = pl.program_id(1)
    tl = c * g

    @pl.when(l == 0)
    def _():
        xp_ref[5:8, :] = conv0_ref[0]
        s_ref[...] = s0_ref[0]

    x = qkv_ref[0]
    xp_ref[8:8 + tl, :] = x
    w = convw_ref[...]
    y = xp_ref[5:5 + tl, :] * w[0:1]
    y = y + xp_ref[6:6 + tl, :] * w[1:2]
    y = y + xp_ref[7:7 + tl, :] * w[2:3]
    y = y + x * w[3:4]
    tail = xp_ref[5 + tl:8 + tl, :]
    xp_ref[5:8, :] = tail

    @pl.when(l == nl - 1)
    def _():
        convn_ref[0] = tail

    act = y * jax.nn.sigmoid(y)
    seg = seg_ref[...]

    def l2n(t):
        return t * lax.rsqrt(_mm_exact_rhs(t * t, seg) + L2_EPS)

    q = l2n(act[:, 0:256]) * (HEAD ** -0.5)
    k = l2n(act[:, 256:512])
    v = act[:, 512:768]

    ab = ab_ref[0]
    gl = -jnp.exp(alog_ref[...]) * jax.nn.softplus(ab + dtb_ref[...])
    beta_w = _mm_exact_rhs(jax.nn.sigmoid(ab), expb_ref[...])
    g_w = _mm_exact_lhs(tril_ref[...], _mm_exact_rhs(gl, expg_ref[...]))
    g_rows = _mm_exact_rhs(gl, triu_ref[...], _TN)
    g_last_w = _mm_exact_lhs(last_ref[...], g_w)
    eg_w = jnp.exp(g_w)
    v_beta = v * beta_w
    k_beta_eg = k * (beta_w * eg_w)
    q_dec = q * eg_w
    k_tail_w = k * jnp.exp(g_last_w - g_w)
    decay_end_w = jnp.exp(g_last_w)

    row = _iota((c, c), 0)
    col = _iota((c, c), 1)
    lower = row >= col
    strict = row > col

    probs = [(slice(ci * c, (ci + 1) * c), h) for ci in range(g) for h in range(4)]
    heads = lambda h: slice(h * HEAD, (h + 1) * HEAD)
    decay = [jnp.where(lower, jnp.exp(jnp.where(lower, g_w[rows, heads(h)][:, :c] - g_rows[h:h + 1, rows], 0.0)),
                       0.0) for rows, h in probs]
    kh = [k[rows, heads(h)] for rows, h in probs]
    kq = [_mm(jnp.concatenate([kk_, q[rows, heads(h)]], axis=0), kk_, _NT)
          for kk_, (rows, h) in zip(kh, probs)]
    t_inv = _unit_lower_inverses(
        [jnp.where(strict, beta_w[rows, heads(h)][:, :c] * x[:c] * d, 0.0)
         for x, d, (rows, h) in zip(kq, decay, probs)], c)
    sol = [_mm(t, jnp.concatenate([v_beta[rows, heads(h)], k_beta_eg[rows, heads(h)]], axis=1))
           for t, (rows, h) in zip(t_inv, probs)]
    qk = [jnp.where(lower, x[c:] * d, 0.0) for x, d in zip(kq, decay)]
    kt_sol = [_mm(k_tail_w[rows, heads(h)], so, _TN) for so, (rows, h) in zip(sol, probs)]
    qk_sol = [_mm(x, so) for x, so in zip(qk, sol)]
    q_eff = [q_dec[rows, heads(h)] - x[:, HEAD:] for x, (rows, h) in zip(qk_sol, probs)]
    decay_end = [decay_end_w[rows.stop - 1:rows.stop, heads(h)] for rows, h in probs]

    states = [s_ref[h] for h in range(4)]
    for ci in range(g):
        rows = slice(ci * c, (ci + 1) * c)
        idx = [ci * 4 + h for h in range(4)]
        o_new = [_mm(q_eff[n], states[h]) + qk_sol[n][:, :HEAD] for h, n in enumerate(idx)]
        states = [states[h] * decay_end[n] + kt_sol[n][:, :HEAD] - _mm(kt_sol[n][:, HEAD:], states[h])
                  for h, n in enumerate(idx)]
        for h in range(4):
            o_ref[rows, heads(h)] = o_new[h]
    for h in range(4):
        s_ref[h] = states[h]

    o = o_ref[...]
    ms = _mm_exact_rhs(o * o, seg) * (1.0 / HEAD)
    z = z_ref[0]
    ao = (o * lax.rsqrt(ms + NORM_EPS) * anorm_ref[...]) * (z * jax.nn.sigmoid(z))
    mix_ref[0] = ao.astype(BF16)

    @pl.when(l == nl - 1)
    def _():
        sn_ref[0] = s_ref[...]


def _gated_delta(a_qkv, a_z, a_ab, conv0, s0, conv_w, alog_row, dtb_row, anorm_row):
    b, seq, _ = a_qkv.shape
    c, g = _chunking(seq)
    tl = c * g
    nl = seq // tl
    tile = lambda i, l: (i, l, 0)
    per_b3 = lambda i, l: (i, 0, 0)
    per_b4 = lambda i, l: (i, 0, 0, 0)
    fixed = lambda i, l: (0, 0)
    sel = _selector_matrices(tl, c)
    consts = [sel[name] for name in ("seg", "tril", "triu", "last", "expg", "expb")]
    return pl.pallas_call(
        functools.partial(_delta_kernel, c=c, g=g, nl=nl),
        grid=(b, nl),
        in_specs=[pl.BlockSpec((1, tl, 768), tile), pl.BlockSpec((1, tl, 256), tile),
                  pl.BlockSpec((1, tl, LANES), tile), pl.BlockSpec((1, 3, 768), per_b3),
                  pl.BlockSpec((1, 4, HEAD, HEAD), per_b4), pl.BlockSpec((4, 768), fixed),
                  pl.BlockSpec((1, LANES), fixed), pl.BlockSpec((1, LANES), fixed),
                  pl.BlockSpec((1, 256), fixed)] + [pl.BlockSpec(m.shape, fixed) for m in consts],
        out_specs=[pl.BlockSpec((1, tl, 256), tile), pl.BlockSpec((1, 3, 768), per_b3),
                   pl.BlockSpec((1, 4, HEAD, HEAD), per_b4)],
        out_shape=[jax.ShapeDtypeStruct((b, seq, 256), BF16),
                   jax.ShapeDtypeStruct((b, 3, 768), F32),
                   jax.ShapeDtypeStruct((b, 4, HEAD, HEAD), F32)],
        scratch_shapes=[pltpu.VMEM((tl + 8, 768), F32), pltpu.VMEM((4, HEAD, HEAD), F32),
                        pltpu.VMEM((tl, 256), F32)],
        compiler_params=pltpu.CompilerParams(dimension_semantics=("parallel", "arbitrary"),
                                             vmem_limit_bytes=VMEM_LIMIT),
        name="gated_delta",
    )(a_qkv, a_z, a_ab, conv0, s0, conv_w, alog_row, dtb_row, anorm_row, *consts)


def _rwkv_kernel(c_ref, shift0_ref, s0_ref, mu_ref, w0_ref, wup_ref, a0_ref, aup_ref, gup_ref,
                 kk_ref, ka_ref, rk_ref, lnw_ref, lnb_ref, seg_ref, tril_ref,
                 mix_ref, shiftn_ref, sn_ref, xp_ref, s_ref, y_ref, *, c, g, nl):
    l = pl.program_id(1)
    tl = c * g

    @pl.when(l == 0)
    def _():
        xp_ref[7:8, :] = shift0_ref[0]
        s_ref[...] = s0_ref[0]

    raw = c_ref[0]
    xp_ref[8:8 + tl, :] = raw
    prev = xp_ref[7:7 + tl, :]
    last = raw[tl - 1:tl, :]
    xp_ref[7:8, :] = last

    @pl.when(l == nl - 1)
    def _():
        shiftn_ref[0] = last

    x = raw + (prev - raw) * mu_ref[...]
    cr, ck, cv = x[:, 0:256], x[:, 256:512], x[:, 512:768]
    c_wa = x[:, 768:896]
    c_g = x[:, 896:1024]
    w_log = -jnp.exp(-jax.nn.softplus(-(w0_ref[...] + _mm(jnp.tanh(c_wa), wup_ref[...])))
                     - RWKV_DECAY_OFFSET)
    ca = jax.nn.sigmoid(a0_ref[...] + _mm(c_wa, aup_ref[...]))
    cg = _mm(jax.nn.sigmoid(c_g), gup_ref[...])
    seg = seg_ref[...]
    kkv = ck * kk_ref[...]
    kk = kkv * lax.rsqrt(_mm_exact_rhs(kkv * kkv, seg) + L2_EPS)
    ck = ck * (1.0 + (ca - 1.0) * ka_ref[...])

    g_cum = _mm_exact_lhs(tril_ref[...], w_log)
    e_pos = jnp.exp(g_cum)
    e_neg = jnp.exp(-g_cum)
    a_t = -kk * jnp.exp(g_cum - w_log)
    b_t = (kk * ca) * e_neg
    k_t = ck * e_neg
    r_t = cr * e_pos

    row = _iota((c, c), 0)
    col = _iota((c, c), 1)
    lower = row >= col
    strict = row > col

    probs = [(slice(ci * c, (ci + 1) * c), h) for ci in range(g) for h in range(4)]
    heads = lambda h: slice(h * HEAD, (h + 1) * HEAD)
    bh = [b_t[rows, heads(h)] for rows, h in probs]
    kh = [k_t[rows, heads(h)] for rows, h in probs]
    vh = [cv[rows, heads(h)] for rows, h in probs]
    ar = [jnp.concatenate([a_t[rows, heads(h)], r_t[rows, heads(h)]], axis=0) for rows, h in probs]
    pb = [_mm(x, y_, _NT) for x, y_ in zip(ar, bh)]
    pk = [_mm(x, y_, _NT) for x, y_ in zip(ar, kh)]
    t_inv = _unit_lower_inverses([-jnp.where(strict, x[:c], 0.0) for x in pb], c)
    n_rb = [jnp.where(lower, x[c:], 0.0) for x in pb]
    mn = [_mm(jnp.concatenate([jnp.where(strict, x[:c], 0.0), jnp.where(lower, x[c:], 0.0)], axis=0), vv)
          for x, vv in zip(pk, vh)]
    bk = [jnp.concatenate([x, y_], axis=0) for x, y_ in zip(bh, kh)]
    g_end = [e_pos[rows.stop - 1:rows.stop, heads(h)] for rows, h in probs]
    ta = [_mm(t, jnp.concatenate([x[:c], m_[:c]], axis=1)) for t, x, m_ in zip(t_inv, ar, mn)]
    p_mat = [_mm(x[:, :HEAD], y_, _TN) for x, y_ in zip(ta, bh)]
    c_mat = [_mm(jnp.concatenate([x[:, HEAD:], vv], axis=0), y_, _TN) for x, vv, y_ in zip(ta, vh, bk)]
    nr = [_mm(x, y_) for x, y_ in zip(n_rb, ta)]
    r_eff = [x[c:] + y_[:, :HEAD] for x, y_ in zip(ar, nr)]
    y_off = [x[:, HEAD:] + m_[c:] for x, m_ in zip(nr, mn)]

    states = [s_ref[h] for h in range(4)]
    for ci in range(g):
        rows = slice(ci * c, (ci + 1) * c)
        idx = [ci * 4 + h for h in range(4)]
        y_new = [_mm(r_eff[n], states[h], _NT) + y_off[n] for h, n in enumerate(idx)]
        states = [(states[h] + _mm(states[h], p_mat[n]) + c_mat[n]) * g_end[n] for h, n in enumerate(idx)]
        for h in range(4):
            y_ref[rows, heads(h)] = y_new[h]
    for h in range(4):
        s_ref[h] = states[h]

    y = y_ref[...]
    mean = _mm_exact_rhs(y, seg) * (1.0 / HEAD)
    yc = y - mean
    var = _mm_exact_rhs(yc * yc, seg) * (1.0 / HEAD)
    cy = (yc * lax.rsqrt(var + C_LN_EPS)) * lnw_ref[...] + lnb_ref[...]
    bonus = _mm_exact_rhs(cr * ck * rk_ref[...], seg) * cv
    mix_ref[0] = ((cy + bonus) * cg).astype(BF16)

    @pl.when(l == nl - 1)
    def _():
        sn_ref[0] = s_ref[...]


def _rwkv7(c_raw, shift0, s0, params):
    b, seq, width = c_raw.shape
    c, g = _chunking(seq)
    tl = c * g
    nl = seq // tl
    tile = lambda i, l: (i, l, 0)
    per_b3 = lambda i, l: (i, 0, 0)
    per_b4 = lambda i, l: (i, 0, 0, 0)
    fixed = lambda i, l: (0, 0)
    sel = _selector_matrices(tl, c)
    params = tuple(params) + (sel["seg"], sel["tril"])
    param_specs = [pl.BlockSpec(p.shape, fixed) for p in params]
    return pl.pallas_call(
        functools.partial(_rwkv_kernel, c=c, g=g, nl=nl),
        grid=(b, nl),
        in_specs=[pl.BlockSpec((1, tl, width), tile), pl.BlockSpec((1, 1, width), per_b3),
                  pl.BlockSpec((1, 4, HEAD, HEAD), per_b4)] + param_specs,
        out_specs=[pl.BlockSpec((1, tl, 256), tile), pl.BlockSpec((1, 1, width), per_b3),
                   pl.BlockSpec((1, 4, HEAD, HEAD), per_b4)],
        out_shape=[jax.ShapeDtypeStruct((b, seq, 256), BF16),
                   jax.ShapeDtypeStruct((b, 1, width), F32),
                   jax.ShapeDtypeStruct((b, 4, HEAD, HEAD), F32)],
        scratch_shapes=[pltpu.VMEM((tl + 8, width), F32), pltpu.VMEM((4, HEAD, HEAD), F32),
                        pltpu.VMEM((tl, 256), F32)],
        compiler_params=pltpu.CompilerParams(dimension_semantics=("parallel", "arbitrary"),
                                             vmem_limit_bytes=VMEM_LIMIT),
        name="rwkv7",
    )(c_raw, shift0, s0, *params)


def _attn_kernel(i_tab, j_tab, flag_tab, q_ref, k_ref, v_ref, lam_ref, bn_ref, o_ref,
                 m_ref, acc_ref, *, tq, tk, past, lk_true, lam_init):
    p = pl.program_id(1)
    i = i_tab[p]
    j = j_tab[p]

    @pl.when(j == 0)
    def _():
        m_ref[...] = jnp.full(m_ref.shape, MASK_VALUE, F32)
        acc_ref[...] = jnp.zeros(acc_ref.shape, F32)

    nslab = tk // LANES

    def step(masked):
        if masked:
            q_pos = past + i * tq + _iota((tq, tk), 0)
            k_pos = j * tk + _iota((tq, tk), 1)
            visible = (jnp.right_shift(k_pos, 6) <= jnp.right_shift(q_pos, 6)) & (k_pos < lk_true)
        q = q_ref[0]
        k = k_ref[0]
        v = v_ref[0]

        def scores(hm):
            sl = slice(hm * HEAD, (hm + 1) * HEAD)
            return lax.dot_general(q[:, sl], k[:, sl], _NT, preferred_element_type=F32)

        ones = jnp.ones((tk, LANES), BF16)
        s_next = scores(0)
        for hm in range(8):
            h = hm // 2
            s = s_next
            if hm + 1 < 8:
                s_next = scores(hm + 1)
            if masked:
                s = jnp.where(visible, s, MASK_VALUE)
            slabs = [s[:, n * LANES:(n + 1) * LANES] for n in range(nslab)]
            m_cur = functools.reduce(jnp.maximum, slabs)
            m_prev = m_ref[hm]
            m_new = jnp.maximum(m_prev, jnp.max(m_cur, axis=1, keepdims=True))
            alpha = jnp.exp2(m_prev - m_new)
            e = jnp.concatenate([jnp.exp2(sb - m_new).astype(BF16) for sb in slabs], axis=1)
            v_ones = jnp.concatenate([v[:, h * LANES:(h + 1) * LANES], ones], axis=1)
            acc_ref[hm] = (jnp.concatenate([alpha, alpha], axis=1) * acc_ref[hm]
                           + jnp.dot(e, v_ones, preferred_element_type=F32))
            m_ref[hm] = m_new

    needs_mask = jnp.bitwise_and(flag_tab[p], 2) != 0

    @pl.when(needs_mask)
    def _():
        step(True)

    @pl.when(jnp.logical_not(needs_mask))
    def _():
        step(False)

    @pl.when(jnp.bitwise_and(flag_tab[p], 1) != 0)
    def _():
        lp = lam_ref[...]
        lam = (jnp.exp(jnp.sum(lp[0:1] * lp[1:2], axis=1, keepdims=True))
               - jnp.exp(jnp.sum(lp[2:3] * lp[3:4], axis=1, keepdims=True)) + lam_init)
        for h in range(4):
            a1 = acc_ref[2 * h]
            a2 = acc_ref[2 * h + 1]
            o = a1[:, :LANES] / a1[:, LANES:] - lam * (a2[:, :LANES] / a2[:, LANES:])
            o = _rmsnorm_rows(o, bn_ref[...]) * (1.0 - lam_init)
            o_ref[0, :, h * LANES:(h + 1) * LANES] = o.astype(BF16)


def _diff_attention(q, k, v, lam_params, bnorm_row, past, lam_init, tq, tk):
    b, lq, width = q.shape
    lk_true = k.shape[1]
    nk = -(-lk_true // tk)
    if nk * tk != lk_true:
        pad = ((0, 0), (0, nk * tk - lk_true), (0, 0))
        k = jnp.pad(k, pad)
        v = jnp.pad(v, pad)
    nq = lq // tq
    i_list, j_list, flag_list = [], [], []
    for i in range(nq):
        first_pos = past + i * tq
        last_pos = first_pos + tq - 1
        j_max = min(((last_pos // CHUNK + 1) * CHUNK - 1) // tk, nk - 1)
        for j in range(j_max + 1):
            key_end = (j + 1) * tk
            all_visible = (key_end - 1) // CHUNK <= first_pos // CHUNK and key_end <= lk_true
            i_list.append(i)
            j_list.append(j)
            flag_list.append((1 if j == j_max else 0) + (0 if all_visible else 2))
    tabs = [jnp.asarray(np.asarray(t, np.int32)) for t in (i_list, j_list, flag_list)]
    grid_spec = pltpu.PrefetchScalarGridSpec(
        num_scalar_prefetch=3,
        grid=(b, len(i_list)),
        in_specs=[pl.BlockSpec((1, tq, width), lambda bi, p, it, jt, lt: (bi, it[p], 0)),
                  pl.BlockSpec((1, tk, width), lambda bi, p, it, jt, lt: (bi, jt[p], 0)),
                  pl.BlockSpec((1, tk, width), lambda bi, p, it, jt, lt: (bi, jt[p], 0)),
                  pl.BlockSpec((4, HEAD), lambda bi, p, it, jt, lt: (0, 0)),
                  pl.BlockSpec((1, LANES), lambda bi, p, it, jt, lt: (0, 0))],
        out_specs=pl.BlockSpec((1, tq, width), lambda bi, p, it, jt, lt: (bi, it[p], 0)),
        scratch_shapes=[pltpu.VMEM((8, tq, LANES), F32), pltpu.VMEM((8, tq, 2 * LANES), F32)],
    )
    return pl.pallas_call(
        functools.partial(_attn_kernel, tq=tq, tk=tk, past=past, lk_true=lk_true, lam_init=lam_init),
        grid_spec=grid_spec,
        out_shape=jax.ShapeDtypeStruct((b, lq, width), BF16),
        compiler_params=pltpu.CompilerParams(dimension_semantics=("parallel", "arbitrary"),
                                             vmem_limit_bytes=VMEM_LIMIT),
        name="diff_attention",
    )(*tabs, q, k, v, lam_params, bnorm_row)


def _mlp_kernel(h_ref, ma_ref, mb_ref, mc_ref, wo_ref, nf_ref, w1_ref, w2_ref, np_ref, wg_ref,
                wp_ref, p_ref, nfin_ref, o_ref, *, tf, final_norm):
    mixed = (jnp.dot(ma_ref[...], wo_ref[0:256, :], preferred_element_type=F32)
             + jnp.dot(mb_ref[...], wo_ref[256:768, :], preferred_element_type=F32)
             + jnp.dot(mc_ref[...], wo_ref[768:1024, :], preferred_element_type=F32))
    h1 = h_ref[...] + mixed
    xn = _rmsnorm_rows(h1, nf_ref[...]).astype(BF16)
    h2 = h1
    for f0 in range(0, w1_ref.shape[1], tf):
        u = jnp.maximum(jnp.dot(xn, w1_ref[:, f0:f0 + tf], preferred_element_type=F32), 0.0)
        h2 = h2 + jnp.dot((u * u).astype(BF16), w2_ref[f0:f0 + tf, :], preferred_element_type=F32)
    gate = jax.nn.sigmoid(jnp.dot(_rmsnorm_rows(h2, np_ref[...]).astype(BF16), wg_ref[...],
                                  preferred_element_type=F32))
    h3 = h2 + gate * jnp.dot(p_ref[...].astype(BF16), wp_ref[...], preferred_element_type=F32)
    if final_norm:
        h3 = _rmsnorm_rows(h3, nfin_ref[...])
    o_ref[...] = h3


def _mix_mlp(h2d, mix_a, mix_b, mix_c, w_out, norm_ffn, w_ff1, w_ff2, norm_ple, w_gate, w_proj,
             p2d, norm_final, final_norm, tm, tf):
    t, d = h2d.shape
    dff = w_ff1.shape[1]
    row = lambda i: (i, 0)
    fixed = lambda i: (0, 0)
    resident = lambda shape: pl.BlockSpec(shape, fixed, pipeline_mode=pl.Buffered(1))
    return pl.pallas_call(
        functools.partial(_mlp_kernel, tf=tf, final_norm=final_norm),
        grid=(t // tm,),
        in_specs=[pl.BlockSpec((tm, d), row), pl.BlockSpec((tm, 256), row),
                  pl.BlockSpec((tm, 512), row), pl.BlockSpec((tm, 256), row),
                  resident((d, d)), resident((1, d)), resident((d, dff)), resident((dff, d)),
                  resident((1, d)), resident((d, d)), resident((p2d.shape[1], d)),
                  pl.BlockSpec((tm, p2d.shape[1]), row), resident((1, d))],
        out_specs=pl.BlockSpec((tm, d), row),
        out_shape=jax.ShapeDtypeStruct((t, d), F32),
        compiler_params=pltpu.CompilerParams(dimension_semantics=("parallel",),
                                             vmem_limit_bytes=VMEM_LIMIT),
        name="mix_mlp",
    )(h2d, mix_a, mix_b, mix_c, w_out, norm_ffn, w_ff1, w_ff2, norm_ple, w_gate, w_proj, p2d,
      norm_final)


def _rope_tables(past, seq_len):
    half = HEAD // 2
    inv = ROPE_THETA ** (-2.0 * jnp.arange(half, dtype=F32) / HEAD)
    pos = past + jnp.arange(seq_len, dtype=jnp.int32)
    ang = pos.astype(F32)[:, None] * inv[None, :]
    cos, sin, zero = jnp.cos(ang), jnp.sin(ang), jnp.zeros_like(ang)
    rep = LANES // HEAD
    cos_t = jnp.tile(jnp.concatenate([cos, cos], axis=1), (1, rep))
    sa_t = jnp.tile(jnp.concatenate([-sin, zero], axis=1), (1, rep))
    sb_t = jnp.tile(jnp.concatenate([zero, sin], axis=1), (1, rep))
    return cos_t, sa_t, sb_t


def _pad_lanes(v, width=LANES):
    return jnp.pad(v, (0, width - v.shape[0]))[None, :]


def _prep_layer(i, norm_mix, w_in, a_conv_w, a_A_log, a_dt_bias, a_norm,
                b_lam_q1, b_lam_k1, b_lam_q2, b_lam_k2, b_norm,
                c_mu, c_w0, c_w_up, c_a0, c_a_up, c_g_up, c_k_k, c_k_a, c_r_k, c_ln_w, c_ln_b,
                w_out, norm_ffn, w_ff1, w_ff2, norm_ple, w_ple_gate, w_ple_proj):
    w = w_in[i]
    d = w.shape[0]
    w_perm = jnp.concatenate(
        [w[:, 0:1024], w[:, 1032:3592], w[:, 1024:1032],
         jnp.zeros((d, _PROJ_WIDTH - 3592), w.dtype)], axis=1).astype(BF16)
    zeros_rank = jnp.zeros_like(c_w_up[i])
    rwkv = (c_mu[i][None, :], c_w0[i][None, :],
            jnp.concatenate([c_w_up[i], zeros_rank], axis=0).astype(BF16), c_a0[i][None, :],
            jnp.concatenate([jnp.zeros_like(c_a_up[i]), c_a_up[i]], axis=0).astype(BF16),
            c_g_up[i].astype(BF16), c_k_k[i][None, :], c_k_a[i][None, :],
            c_r_k[i].reshape(1, -1), c_ln_w[i][None, :], c_ln_b[i][None, :])
    return dict(
        norm_mix=norm_mix[i][None, :], w_in=w_perm, conv_w=a_conv_w[i],
        alog=_pad_lanes(a_A_log[i]), dtb=_pad_lanes(a_dt_bias[i]),
        anorm=jnp.tile(a_norm[i], 4)[None, :],
        lam=jnp.stack([b_lam_q1[i], b_lam_k1[i], b_lam_q2[i], b_lam_k2[i]], axis=0),
        bnorm=b_norm[i][None, :], rwkv=rwkv,
        w_out=w_out[i].astype(BF16), norm_ffn=norm_ffn[i][None, :],
        w_ff1=w_ff1[i].astype(BF16), w_ff2=w_ff2[i].astype(BF16),
        norm_ple=norm_ple[i][None, :], w_gate=w_ple_gate[i].astype(BF16),
        w_proj=w_ple_proj[i].astype(BF16))


def _pick_tile(n, target):
    t = min(n, target)
    while n % t:
        t //= 2
    return t


def _trunk(x, p, cache_k, cache_v, conv_buf, delta_s, shift_prev, wkv_s, layers, norm_final):
    b, seq, d = x.shape
    depth = len(layers)
    past = cache_k.shape[2]
    t = b * seq
    tm = _pick_tile(t, PROJ_TM)
    rope_tabs = _rope_tables(past, seq)
    h = x.reshape(t, d)
    states = []
    kv_stacked = None
    for i, lp in enumerate(layers):
        a_qkv, a_z, a_ab, q_b, k_f, v_f, k_b, v_b, c_raw = _input_projection(
            h, lp["norm_mix"], lp["w_in"], rope_tabs, seq, tm, i, depth, kv_stacked)
        kv_stacked = (k_f, v_f)
        mix_a, conv_n, delta_n = _gated_delta(
            a_qkv.reshape(b, seq, -1), a_z.reshape(b, seq, -1), a_ab.reshape(b, seq, -1),
            conv_buf[i], delta_s[i], lp["conv_w"], lp["alog"], lp["dtb"], lp["anorm"])
        k_all = k_b.reshape(b, seq, -1)
        v_all = v_b.reshape(b, seq, -1)
        if past:
            k_all = jnp.concatenate([cache_k[i].reshape(b, past, -1).astype(BF16), k_all], axis=1)
            v_all = jnp.concatenate([cache_v[i].reshape(b, past, -1).astype(BF16), v_all], axis=1)
        lam_init = 0.8 - 0.6 * math.exp(-0.3 * i)
        mix_b = _diff_attention(q_b.reshape(b, seq, -1), k_all, v_all, lp["lam"], lp["bnorm"],
                                past, lam_init, _pick_tile(seq, ATTN_TQ),
                                ATTN_TK if past + seq >= 8 * ATTN_TK else ATTN_TK // 2)
        mix_c, shift_n, wkv_n = _rwkv7(c_raw.reshape(b, seq, -1), shift_prev[i][:, None, :],
                                       wkv_s[i], lp["rwkv"])
        h = _mix_mlp(h, mix_a.reshape(t, -1), mix_b.reshape(t, -1), mix_c.reshape(t, -1),
                     lp["w_out"], lp["norm_ffn"], lp["w_ff1"], lp["w_ff2"], lp["norm_ple"],
                     lp["w_gate"], lp["w_proj"], p[i].reshape(t, -1), norm_final[None, :],
                     i == depth - 1, _pick_tile(t, MLP_TM), MLP_TF)
        states.append((conv_n, delta_n, shift_n[:, 0, :], wkv_n))
    conv_all, delta_all, shift_all, wkv_all = (
        jnp.stack([st[j] for st in states], axis=0) for j in range(4))
    k_all_layers, v_all_layers = (a.reshape(depth, b, seq, 4, LANES) for a in kv_stacked)
    return h.reshape(b, seq, d), [conv_all, delta_all, k_all_layers, v_all_layers, shift_all, wkv_all]


def kernel(x_prompt, x_sample, cache_b_k, cache_b_v, state_a_conv, state_a_delta, state_c_shift, state_c_wkv, p_prompt, p_sample, norm_mix, w_in, a_conv_w, a_A_log, a_dt_bias, a_norm, b_lam_q1, b_lam_k1, b_lam_q2, b_lam_k2, b_norm, c_mu, c_w0, c_w_up, c_a0, c_a_up, c_g_up, c_k_k, c_k_a, c_r_k, c_ln_w, c_ln_b, w_out, norm_ffn, w_ff1, w_ff2, norm_ple, w_ple_gate, w_ple_proj, norm_final):
    depth = w_in.shape[0]
    layers = [_prep_layer(i, norm_mix, w_in, a_conv_w, a_A_log, a_dt_bias, a_norm,
                          b_lam_q1, b_lam_k1, b_lam_q2, b_lam_k2, b_norm,
                          c_mu, c_w0, c_w_up, c_a0, c_a_up, c_g_up, c_k_k, c_k_a, c_r_k, c_ln_w, c_ln_b,
                          w_out, norm_ffn, w_ff1, w_ff2, norm_ple, w_ple_gate, w_ple_proj)
              for i in range(depth)]
    bp = x_prompt.shape[0]
    dt = x_prompt.dtype
    zeros = lambda ref: jnp.zeros((depth, bp) + ref.shape[2:], dt)
    empty_k = jnp.zeros((depth, bp, 0) + cache_b_k.shape[3:], dt)
    empty_v = jnp.zeros((depth, bp, 0) + cache_b_v.shape[3:], dt)
    y_prompt, st_p = _trunk(x_prompt, p_prompt, empty_k, empty_v, zeros(state_a_conv),
                            zeros(state_a_delta), zeros(state_c_shift), zeros(state_c_wkv),
                            layers, norm_final)
    y_sample, st_s = _trunk(x_sample, p_sample, cache_b_k, cache_b_v, state_a_conv, state_a_delta,
                            state_c_shift, state_c_wkv, layers, norm_final)
    return (y_prompt, y_sample, *st_p, *st_s)
```

```python
import functools
import math

import numpy as np
import jax
import jax.numpy as jnp
from jax import lax
from jax.experimental import pallas as pl
from jax.experimental.pallas import tpu as pltpu

F32 = jnp.float32
BF16 = jnp.bfloat16

CHUNK = 64
ROPE_THETA = 10000.0
NORM_EPS = 1e-6
L2_EPS = 1e-6
C_LN_EPS = 64e-5
RWKV_DECAY_OFFSET = 0.5
MASK_VALUE = float(np.finfo(np.float32).min)

LANES = 128
HEAD = 64
VMEM_LIMIT = 56 * 1024 * 1024
PROJ_TM = 512
MLP_TM = 512
MLP_TF = 1024
CHUNKS_PER_STEP = 8
ATTN_TQ = 512
ATTN_TK = 1024
ATTN_ONES_ROWS = 16
ATTN_Q_SCALE = (64 ** -0.5) * math.log2(math.e)


def _iota(shape, axis):
    return lax.broadcasted_iota(jnp.int32, shape, axis)


_NN = (((1,), (0,)), ((), ()))
_NT = (((1,), (1,)), ((), ()))
_TN = (((0,), (0,)), ((), ()))


def _mm(a, b, dims=_NN):
    return lax.dot_general(a.astype(BF16), b.astype(BF16), dims, preferred_element_type=F32)


def _split3(x):
    h = x.astype(BF16)
    r = x - h.astype(F32)
    m = r.astype(BF16)
    lo = (r - m.astype(F32)).astype(BF16)
    return h, m, lo


def _mm_exact_rhs(x, ones_bf16, dims=_NN):
    h, m, lo = _split3(x)
    d = functools.partial(lax.dot_general, dimension_numbers=dims, preferred_element_type=F32)
    return d(h, ones_bf16) + d(m, ones_bf16) + d(lo, ones_bf16)


def _mm_exact_lhs(ones_bf16, x, dims=_NN):
    h, m, lo = _split3(x)
    d = functools.partial(lax.dot_general, dimension_numbers=dims, preferred_element_type=F32)
    return d(ones_bf16, h) + d(ones_bf16, m) + d(ones_bf16, lo)


def _selector_matrices(tl, c):
    tok = np.arange(tl)
    same = (tok[:, None] // c) == (tok[None, :] // c)
    lane = np.arange(4 * HEAD)
    src = np.arange(LANES)
    mats = dict(
        seg=(lane[:, None] // HEAD) == (lane[None, :] // HEAD),
        tril=same & (tok[:, None] >= tok[None, :]),
        triu=same & (tok[:, None] <= tok[None, :]),
        last=tok[None, :] == (tok[:, None] // c) * c + c - 1,
        expg=src[:, None] == lane[None, :] // HEAD,
        expb=src[:, None] == lane[None, :] // HEAD + 4)
    return {name: jnp.asarray(m.astype(np.float32), BF16) for name, m in mats.items()}


def _chunking(seq):
    c = min(CHUNK, seq)
    assert c & (c - 1) == 0 and seq % c == 0, seq
    g = min(CHUNKS_PER_STEP, seq // c)
    while (seq // c) % g:
        g -= 1
    return c, g


def _rmsnorm_rows(x, g):
    return x * lax.rsqrt(jnp.mean(x * x, axis=-1, keepdims=True) + NORM_EPS) * g


def _unit_lower_inverses(a_list, c):
    row = _iota((c, c), 0)
    col = _iota((c, c), 1)

    def same_block(shift):
        return jnp.right_shift(row, shift) == jnp.right_shift(col, shift)

    eye = (row == col).astype(F32)
    leaf = same_block(3)
    ns = [jnp.where(leaf, -a, 0.0) for a in a_list]
    ts = [eye + n for n in ns]
    n2s = [_mm(n, n) for n in ns]
    ts = [t + _mm(t, n2) for t, n2 in zip(ts, n2s)]
    n4s = [_mm(n2, n2) for n2 in n2s]
    ts = [t + _mm(t, n4) for t, n4 in zip(ts, n4s)]
    shift = 3
    while (1 << shift) < c:
        off_mask = same_block(shift + 1) & jnp.logical_not(same_block(shift))
        tos = [_mm(t, jnp.where(off_mask, a, 0.0)) for t, a in zip(ts, a_list)]
        ts = [t - _mm(to, t) for t, to in zip(ts, tos)]
        shift += 1
    return ts


_PROJ_A_QKV = 0
_PROJ_A_Z = 768
_PROJ_B_Q = 1024
_PROJ_B_K = 1536
_PROJ_B_V = 2048
_PROJ_C = 2560
_PROJ_A_AB = 3584
_PROJ_WIDTH = 3712


def _proj_kernel(*refs, n_alias, v_transposed):
    x_ref, g_ref, w_ref, cos_ref, sa_ref, sb_ref = refs[:6]
    aqkv_ref, az_ref, aab_ref, q_ref, k_ref, v_ref, kb_ref, vb_ref, c_ref = refs[6 + n_alias:]
    xn = _rmsnorm_rows(x_ref[...], g_ref[...]).astype(BF16)

    def proj(c0, c1):
        return jnp.dot(xn, w_ref[:, c0:c1], preferred_element_type=F32)

    a_all = proj(_PROJ_A_QKV, _PROJ_B_Q)
    aqkv_ref[...] = a_all[:, :_PROJ_A_Z]
    az_ref[...] = a_all[:, _PROJ_A_Z:]
    cos = cos_ref[...]
    sa = sa_ref[...]
    sb = sb_ref[...]

    def rope(x):
        return x * cos + pltpu.roll(x, LANES - 32, 1) * sa + pltpu.roll(x, 32, 1) * sb

    q_all = proj(_PROJ_B_Q, _PROJ_B_K)
    k_all = proj(_PROJ_B_K, _PROJ_B_V)
    for h in range(4):
        lo, hi = h * LANES, (h + 1) * LANES
        q_ref[:, lo:hi] = (rope(q_all[:, lo:hi]) * ATTN_Q_SCALE).astype(BF16)
        k = rope(k_all[:, lo:hi])
        k_ref[:, h, :] = k
        kb_ref[:, lo:hi] = k.astype(BF16)
    v = proj(_PROJ_B_V, _PROJ_C)
    for h in range(4):
        v_ref[:, h, :] = v[:, h * LANES:(h + 1) * LANES]
    if v_transposed:
        for h in range(4):
            vb_ref[h] = jnp.transpose(v[:, h * LANES:(h + 1) * LANES]).astype(BF16)
    else:
        vb_ref[...] = v.astype(BF16)
    c_all = proj(_PROJ_C, _PROJ_WIDTH)
    c_ref[...] = c_all[:, :_PROJ_A_AB - _PROJ_C]
    aab_ref[...] = c_all[:, _PROJ_A_AB - _PROJ_C:]


def _input_projection(x2d, g_row, w_bf16, rope_tabs, seq_len, tm, layer, depth, kv_stacked, v_transposed):
    t, d = x2d.shape
    nt = t // tm
    cos_t, sa_t, sb_t = rope_tabs
    if tm >= seq_len:
        reps = tm // seq_len
        cos_t, sa_t, sb_t = (jnp.tile(a, (reps, 1)) for a in (cos_t, sa_t, sb_t))
        tab_map = lambda i: (0, 0)
    else:
        per_seq = seq_len // tm
        tab_map = lambda i: (i % per_seq, 0)
    row = lambda i: (i, 0)
    fixed = lambda i: (0, 0)
    tab_spec = pl.BlockSpec((tm, LANES), tab_map)
    widths = [(768, F32), (256, F32), (128, F32), (512, BF16), (512, F32), (512, F32),
              (512, BF16), (512, BF16), (1024, F32)]
    stacked = (4, 5)
    out_specs = [pl.BlockSpec((None, tm, 4, LANES), lambda i: (layer, i, 0, 0)) if n in stacked
                 else pl.BlockSpec((tm, w), row) for n, (w, _) in enumerate(widths)]
    out_shape = [jax.ShapeDtypeStruct((depth, t, 4, LANES) if n in stacked else (t, w), dt)
                 for n, (w, dt) in enumerate(widths)]
    if v_transposed:
        out_specs[7] = pl.BlockSpec((4, LANES, tm), lambda i: (0, 0, i))
        out_shape[7] = jax.ShapeDtypeStruct((4, LANES, t), BF16)
    in_specs = [pl.BlockSpec((tm, d), row), pl.BlockSpec((1, d), fixed),
                pl.BlockSpec((d, _PROJ_WIDTH), fixed), tab_spec, tab_spec, tab_spec]
    operands = [x2d, g_row, w_bf16, cos_t, sa_t, sb_t]
    aliases = {}
    if kv_stacked is not None:
        for out_idx, arr in zip(stacked, kv_stacked):
            aliases[len(operands)] = out_idx
            in_specs.append(pl.BlockSpec(memory_space=pl.ANY))
            operands.append(arr)
    return pl.pallas_call(
        functools.partial(_proj_kernel, n_alias=len(aliases), v_transposed=v_transposed),
        grid=(nt,),
        in_specs=in_specs,
        out_specs=out_specs,
        out_shape=out_shape,
        input_output_aliases=aliases,
        compiler_params=pltpu.CompilerParams(dimension_semantics=("parallel",),
                                             vmem_limit_bytes=VMEM_LIMIT),
        name="input_projection",
    )(*operands)


def _delta_kernel(qkv_ref, z_ref, ab_ref, conv0_ref, s0_ref, convw_ref, alog_ref, dtb_ref, anorm_ref,
                  seg_ref, tril_ref, triu_ref, last_ref, expg_ref, expb_ref,
                  mix_ref, convn_ref, sn_ref, xp_ref, s_ref, o_ref, *, c, g, nl):
    l = pl.program_id(1)
    tl = c * g

    @pl.when(l == 0)
    def _():
        xp_ref[5:8, :] = conv0_ref[0]
        s_ref[...] = s0_ref[0]

    x = qkv_ref[0]
    xp_ref[8:8 + tl, :] = x
    w = convw_ref[...]
    y = xp_ref[5:5 + tl, :] * w[0:1]
    y = y + xp_ref[6:6 + tl, :] * w[1:2]
    y = y + xp_ref[7:7 + tl, :] * w[2:3]
    y = y + x * w[3:4]
    tail = xp_ref[5 + tl:8 + tl, :]
    xp_ref[5:8, :] = tail

    @pl.when(l == nl - 1)
    def _():
        convn_ref[0] = tail

    act = y * jax.nn.sigmoid(y)
    seg = seg_ref[...]

    def l2n(t):
        return t * lax.rsqrt(_mm_exact_rhs(t * t, seg) + L2_EPS)

    q = l2n(act[:, 0:256]) * (HEAD ** -0.5)
    k = l2n(act[:, 256:512])
    v = act[:, 512:768]

    ab = ab_ref[0]
    gl = -jnp.exp(alog_ref[...]) * jax.nn.softplus(ab + dtb_ref[...])
    beta_w = _mm_exact_rhs(jax.nn.sigmoid(ab), expb_ref[...])
    g_w = _mm_exact_lhs(tril_ref[...], _mm_exact_rhs(gl, expg_ref[...]))
    g_rows = _mm_exact_rhs(gl, triu_ref[...], _TN)
    g_last_w = _mm_exact_lhs(last_ref[...], g_w)
    eg_w = jnp.exp(g_w)
    v_beta = v * beta_w
    k_beta_eg = k * (beta_w * eg_w)
    q_dec = q * eg_w
    k_tail_w = k * jnp.exp(g_last_w - g_w)
    decay_end_w = jnp.exp(g_last_w)

    row = _iota((c, c), 0)
    col = _iota((c, c), 1)
    lower = row >= col
    strict = row > col

    probs = [(slice(ci * c, (ci + 1) * c), h) for ci in range(g) for h in range(4)]
    heads = lambda h: slice(h * HEAD, (h + 1) * HEAD)
    decay = [jnp.where(lower, jnp.exp(jnp.where(lower, g_w[rows, heads(h)][:, :c] - g_rows[h:h + 1, rows], 0.0)),
                       0.0) for rows, h in probs]
    kh = [k[rows, heads(h)] for rows, h in probs]
    kq = [_mm(jnp.concatenate([kk_, q[rows, heads(h)]], axis=0), kk_, _NT)
          for kk_, (rows, h) in zip(kh, probs)]
    t_inv = _unit_lower_inverses(
        [jnp.where(strict, beta_w[rows, heads(h)][:, :c] * x[:c] * d, 0.0)
         for x, d, (rows, h) in zip(kq, decay, probs)], c)
    sol = [_mm(t, jnp.concatenate([v_beta[rows, heads(h)], k_beta_eg[rows, heads(h)]], axis=1))
           for t, (rows, h) in zip(t_inv, probs)]
    qk = [jnp.where(lower, x[c:] * d, 0.0) for x, d in zip(kq, decay)]
    kt_sol = [_mm(k_tail_w[rows, heads(h)], so, _TN) for so, (rows, h) in zip(sol, probs)]
    qk_sol = [_mm(x, so) for x, so in zip(qk, sol)]
    q_eff = [q_dec[rows, heads(h)] - x[:, HEAD:] for x, (rows, h) in zip(qk_sol, probs)]
    decay_end = [decay_end_w[rows.stop - 1:rows.stop, heads(h)] for rows, h in probs]

    states = [s_ref[h] for h in range(4)]
    for ci in range(g):
        rows = slice(ci * c, (ci + 1) * c)
        idx = [ci * 4 + h for h in range(4)]
        o_new = [_mm(q_eff[n], states[h]) + qk_sol[n][:, :HEAD] for h, n in enumerate(idx)]
        states = [states[h] * decay_end[n] + kt_sol[n][:, :HEAD] - _mm(kt_sol[n][:, HEAD:], states[h])
                  for h, n in enumerate(idx)]
        for h in range(4):
            o_ref[rows, heads(h)] = o_new[h]
    for h in range(4):
        s_ref[h] = states[h]

    o = o_ref[...]
    ms = _mm_exact_rhs(o * o, seg) * (1.0 / HEAD)
    z = z_ref[0]
    ao = (o * lax.rsqrt(ms + NORM_EPS) * anorm_ref[...]) * (z * jax.nn.sigmoid(z))
    mix_ref[0] = ao.astype(BF16)

    @pl.when(l == nl - 1)
    def _():
        sn_ref[0] = s_ref[...]


def _gated_delta(a_qkv, a_z, a_ab, conv0, s0, conv_w, alog_row, dtb_row, anorm_row):
    b, seq, _ = a_qkv.shape
    c, g = _chunking(seq)
    tl = c * g
    nl = seq // tl
    tile = lambda i, l: (i, l, 0)
    per_b3 = lambda i, l: (i, 0, 0)
    per_b4 = lambda i, l: (i, 0, 0, 0)
    fixed = lambda i, l: (0, 0)
    sel = _selector_matrices(tl, c)
    consts = [sel[name] for name in ("seg", "tril", "triu", "last", "expg", "expb")]
    return pl.pallas_call(
        functools.partial(_delta_kernel, c=c, g=g, nl=nl),
        grid=(b, nl),
        in_specs=[pl.BlockSpec((1, tl, 768), tile), pl.BlockSpec((1, tl, 256), tile),
                  pl.BlockSpec((1, tl, LANES), tile), pl.BlockSpec((1, 3, 768), per_b3),
                  pl.BlockSpec((1, 4, HEAD, HEAD), per_b4), pl.BlockSpec((4, 768), fixed),
                  pl.BlockSpec((1, LANES), fixed), pl.BlockSpec((1, LANES), fixed),
                  pl.BlockSpec((1, 256), fixed)] + [pl.BlockSpec(m.shape, fixed) for m in consts],
        out_specs=[pl.BlockSpec((1, tl, 256), tile), pl.BlockSpec((1, 3, 768), per_b3),
                   pl.BlockSpec((1, 4, HEAD, HEAD), per_b4)],
        out_shape=[jax.ShapeDtypeStruct((b, seq, 256), BF16),
                   jax.ShapeDtypeStruct((b, 3, 768), F32),
                   jax.ShapeDtypeStruct((b, 4, HEAD, HEAD), F32)],
        scratch_shapes=[pltpu.VMEM((tl + 8, 768), F32), pltpu.VMEM((4, HEAD, HEAD), F32),
                        pltpu.VMEM((tl, 256), F32)],
        compiler_params=pltpu.CompilerParams(dimension_semantics=("parallel", "arbitrary"),
                                             vmem_limit_bytes=VMEM_LIMIT),
        name="gated_delta",
    )(a_qkv, a_z, a_ab, conv0, s0, conv_w, alog_row, dtb_row, anorm_row, *consts)


def _rwkv_kernel(c_ref, shift0_ref, s0_ref, mu_ref, w0_ref, wup_ref, a0_ref, aup_ref, gup_ref,
                 kk_ref, ka_ref, rk_ref, lnw_ref, lnb_ref, seg_ref, tril_ref,
                 mix_ref, shiftn_ref, sn_ref, xp_ref, s_ref, y_ref, *, c, g, nl):
    l = pl.program_id(1)
    tl = c * g

    @pl.when(l == 0)
    def _():
        xp_ref[7:8, :] = shift0_ref[0]
        s_ref[...] = s0_ref[0]

    raw = c_ref[0]
    xp_ref[8:8 + tl, :] = raw
    prev = xp_ref[7:7 + tl, :]
    last = raw[tl - 1:tl, :]
    xp_ref[7:8, :] = last

    @pl.when(l == nl - 1)
    def _():
        shiftn_ref[0] = last

    x = raw + (prev - raw) * mu_ref[...]
    cr, ck, cv = x[:, 0:256], x[:, 256:512], x[:, 512:768]
    c_wa = x[:, 768:896]
    c_g = x[:, 896:1024]
    w_log = -jnp.exp(-jax.nn.softplus(-(w0_ref[...] + _mm(jnp.tanh(c_wa), wup_ref[...])))
                     - RWKV_DECAY_OFFSET)
    ca = jax.nn.sigmoid(a0_ref[...] + _mm(c_wa, aup_ref[...]))
    cg = _mm(jax.nn.sigmoid(c_g), gup_ref[...])
    seg = seg_ref[...]
    kkv = ck * kk_ref[...]
    kk = kkv * lax.rsqrt(_mm_exact_rhs(kkv * kkv, seg) + L2_EPS)
    ck = ck * (1.0 + (ca - 1.0) * ka_ref[...])

    g_cum = _mm_exact_lhs(tril_ref[...], w_log)
    e_pos = jnp.exp(g_cum)
    e_neg = jnp.exp(-g_cum)
    a_t = -kk * jnp.exp(g_cum - w_log)
    b_t = (kk * ca) * e_neg
    k_t = ck * e_neg
    r_t = cr * e_pos

    row = _iota((c, c), 0)
    col = _iota((c, c), 1)
    lower = row >= col
    strict = row > col

    probs = [(slice(ci * c, (ci + 1) * c), h) for ci in range(g) for h in range(4)]
    heads = lambda h: slice(h * HEAD, (h + 1) * HEAD)
    bh = [b_t[rows, heads(h)] for rows, h in probs]
    kh = [k_t[rows, heads(h)] for rows, h in probs]
    vh = [cv[rows, heads(h)] for rows, h in probs]
    ar = [jnp.concatenate([a_t[rows, heads(h)], r_t[rows, heads(h)]], axis=0) for rows, h in probs]
    pb = [_mm(x, y_, _NT) for x, y_ in zip(ar, bh)]
    pk = [_mm(x, y_, _NT) for x, y_ in zip(ar, kh)]
    t_inv = _unit_lower_inverses([-jnp.where(strict, x[:c], 0.0) for x in pb], c)
    n_rb = [jnp.where(lower, x[c:], 0.0) for x in pb]
    mn = [_mm(jnp.concatenate([jnp.where(strict, x[:c], 0.0), jnp.where(lower, x[c:], 0.0)], axis=0), vv)
          for x, vv in zip(pk, vh)]
    bk = [jnp.concatenate([x, y_], axis=0) for x, y_ in zip(bh, kh)]
    g_end = [e_pos[rows.stop - 1:rows.stop, heads(h)] for rows, h in probs]
    ta = [_mm(t, jnp.concatenate([x[:c], m_[:c]], axis=1)) for t, x, m_ in zip(t_inv, ar, mn)]
    p_mat = [_mm(x[:, :HEAD], y_, _TN) for x, y_ in zip(ta, bh)]
    c_mat = [_mm(jnp.concatenate([x[:, HEAD:], vv], axis=0), y_, _TN) for x, vv, y_ in zip(ta, vh, bk)]
    nr = [_mm(x, y_) for x, y_ in zip(n_rb, ta)]
    r_eff = [x[c:] + y_[:, :HEAD] for x, y_ in zip(ar, nr)]
    y_off = [x[:, HEAD:] + m_[c:] for x, m_ in zip(nr, mn)]

    states = [s_ref[h] for h in range(4)]
    for ci in range(g):
        rows = slice(ci * c, (ci + 1) * c)
        idx = [ci * 4 + h for h in range(4)]
        y_new = [_mm(r_eff[n], states[h], _NT) + y_off[n] for h, n in enumerate(idx)]
        states = [(states[h] + _mm(states[h], p_mat[n]) + c_mat[n]) * g_end[n] for h, n in enumerate(idx)]
        for h in range(4):
            y_ref[rows, heads(h)] = y_new[h]
    for h in range(4):
        s_ref[h] = states[h]

    y = y_ref[...]
    mean = _mm_exact_rhs(y, seg) * (1.0 / HEAD)
    yc = y - mean
    var = _mm_exact_rhs(yc * yc, seg) * (1.0 / HEAD)
    cy = (yc * lax.rsqrt(var + C_LN_EPS)) * lnw_ref[...] + lnb_ref[...]
    bonus = _mm_exact_rhs(cr * ck * rk_ref[...], seg) * cv
    mix_ref[0] = ((cy + bonus) * cg).astype(BF16)

    @pl.when(l == nl - 1)
    def _():
        sn_ref[0] = s_ref[...]


def _rwkv7(c_raw, shift0, s0, params):
    b, seq, width = c_raw.shape
    c, g = _chunking(seq)
    tl = c * g
    nl = seq // tl
    tile = lambda i, l: (i, l, 0)
    per_b3 = lambda i, l: (i, 0, 0)
    per_b4 = lambda i, l: (i, 0, 0, 0)
    fixed = lambda i, l: (0, 0)
    sel = _selector_matrices(tl, c)
    params = tuple(params) + (sel["seg"], sel["tril"])
    param_specs = [pl.BlockSpec(p.shape, fixed) for p in params]
    return pl.pallas_call(
        functools.partial(_rwkv_kernel, c=c, g=g, nl=nl),
        grid=(b, nl),
        in_specs=[pl.BlockSpec((1, tl, width), tile), pl.BlockSpec((1, 1, width), per_b3),
                  pl.BlockSpec((1, 4, HEAD, HEAD), per_b4)] + param_specs,
        out_specs=[pl.BlockSpec((1, tl, 256), tile), pl.BlockSpec((1, 1, width), per_b3),
                   pl.BlockSpec((1, 4, HEAD, HEAD), per_b4)],
        out_shape=[jax.ShapeDtypeStruct((b, seq, 256), BF16),
                   jax.ShapeDtypeStruct((b, 1, width), F32),
                   jax.ShapeDtypeStruct((b, 4, HEAD, HEAD), F32)],
        scratch_shapes=[pltpu.VMEM((tl + 8, width), F32), pltpu.VMEM((4, HEAD, HEAD), F32),
                        pltpu.VMEM((tl, 256), F32)],
        compiler_params=pltpu.CompilerParams(dimension_semantics=("parallel", "arbitrary"),
                                             vmem_limit_bytes=VMEM_LIMIT),
        name="rwkv7",
    )(c_raw, shift0, s0, *params)


def _attn_kernel(i_tab, j_tab, flag_tab, q_ref, k_ref, v_ref, lam_ref, bn_ref, o_ref,
                 m_ref, acc_ref, *, tq, tk, past, lk_true, lam_init):
    p = pl.program_id(1)
    i = i_tab[p]
    j = j_tab[p]

    @pl.when(j == 0)
    def _():
        m_ref[...] = jnp.full(m_ref.shape, MASK_VALUE, F32)
        acc_ref[...] = jnp.zeros(acc_ref.shape, F32)

    nslab = tk // LANES

    def step(masked):
        if masked:
            q_pos = past + i * tq + _iota((tq, tk), 0)
            k_pos = j * tk + _iota((tq, tk), 1)
            visible = (jnp.right_shift(k_pos, 6) <= jnp.right_shift(q_pos, 6)) & (k_pos < lk_true)
        q = q_ref[0]
        k = k_ref[0]
        v = v_ref[0]

        def scores(hm):
            sl = slice(hm * HEAD, (hm + 1) * HEAD)
            return lax.dot_general(q[:, sl], k[:, sl], _NT, preferred_element_type=F32)

        ones = jnp.ones((tk, LANES), BF16)
        s_next = scores(0)
        for hm in range(8):
            h = hm // 2
            s = s_next
            if hm + 1 < 8:
                s_next = scores(hm + 1)
            if masked:
                s = jnp.where(visible, s, MASK_VALUE)
            slabs = [s[:, n * LANES:(n + 1) * LANES] for n in range(nslab)]
            m_cur = functools.reduce(jnp.maximum, slabs)
            m_prev = m_ref[hm]
            m_new = jnp.maximum(m_prev, jnp.max(m_cur, axis=1, keepdims=True))
            alpha = jnp.exp2(m_prev - m_new)
            e = jnp.concatenate([jnp.exp2(sb - m_new).astype(BF16) for sb in slabs], axis=1)
            v_ones = jnp.concatenate([v[:, h * LANES:(h + 1) * LANES], ones], axis=1)
            acc_ref[hm] = (jnp.concatenate([alpha, alpha], axis=1) * acc_ref[hm]
                           + jnp.dot(e, v_ones, preferred_element_type=F32))
            m_ref[hm] = m_new

    needs_mask = jnp.bitwise_and(flag_tab[p], 2) != 0

    @pl.when(needs_mask)
    def _():
        step(True)

    @pl.when(jnp.logical_not(needs_mask))
    def _():
        step(False)

    @pl.when(jnp.bitwise_and(flag_tab[p], 1) != 0)
    def _():
        lp = lam_ref[...]
        lam = (jnp.exp(jnp.sum(lp[0:1] * lp[1:2], axis=1, keepdims=True))
               - jnp.exp(jnp.sum(lp[2:3] * lp[3:4], axis=1, keepdims=True)) + lam_init)
        for h in range(4):
            a1 = acc_ref[2 * h]
            a2 = acc_ref[2 * h + 1]
            o = a1[:, :LANES] / a1[:, LANES:] - lam * (a2[:, :LANES] / a2[:, LANES:])
            o = _rmsnorm_rows(o, bn_ref[...]) * (1.0 - lam_init)
            o_ref[0, :, h * LANES:(h + 1) * LANES] = o.astype(BF16)


def _attn_t_kernel(i_tab, j_tab, flag_tab, q_ref, k_ref, vt_ref, lam_ref, bn_ref, o_ref,
                   m_ref, acc_ref, *, tq, tk, past, lk_true, lam_init):
    p = pl.program_id(1)
    i = i_tab[p]
    j = j_tab[p]

    @pl.when(j == 0)
    def _():
        m_ref[...] = jnp.full(m_ref.shape, MASK_VALUE, F32)
        acc_ref[...] = jnp.zeros(acc_ref.shape, F32)

    def step(masked):
        if masked:
            k_pos = j * tk + _iota((tk, tq), 0)
            q_pos = past + i * tq + _iota((tk, tq), 1)
            visible = (jnp.right_shift(k_pos, 6) <= jnp.right_shift(q_pos, 6)) & (k_pos < lk_true)
        q = q_ref[0]
        k = k_ref[0]

        def scores(hm):
            sl = slice(hm * HEAD, (hm + 1) * HEAD)
            return lax.dot_general(k[:, sl], q[:, sl], _NT, preferred_element_type=F32)

        ones = jnp.ones((ATTN_ONES_ROWS, tk), BF16)
        s_next = scores(0)
        for hm in range(8):
            h = hm // 2
            s = s_next
            if hm + 1 < 8:
                s_next = scores(hm + 1)
            if masked:
                s = jnp.where(visible, s, MASK_VALUE)
            m_prev = m_ref[hm]
            m_new = jnp.maximum(m_prev, jnp.max(s, axis=0, keepdims=True))
            alpha = jnp.exp2(m_prev - m_new)
            e = jnp.exp2(s - m_new).astype(BF16)
            v_ones = jnp.concatenate([vt_ref[h], ones], axis=0)
            acc_ref[hm] = alpha * acc_ref[hm] + jnp.dot(v_ones, e, preferred_element_type=F32)
            m_ref[hm] = m_new

    needs_mask = jnp.bitwise_and(flag_tab[p], 2) != 0

    @pl.when(needs_mask)
    def _():
        step(True)

    @pl.when(jnp.logical_not(needs_mask))
    def _():
        step(False)

    @pl.when(jnp.bitwise_and(flag_tab[p], 1) != 0)
    def _():
        lp = lam_ref[...]
        lam = (jnp.exp(jnp.sum(lp[0:1] * lp[1:2], axis=1, keepdims=True))
               - jnp.exp(jnp.sum(lp[2:3] * lp[3:4], axis=1, keepdims=True)) + lam_init)
        for h in range(4):
            a1 = acc_ref[2 * h]
            a2 = acc_ref[2 * h + 1]
            ot = (a1[:LANES] / a1[LANES:LANES + 1] - lam * (a2[:LANES] / a2[LANES:LANES + 1]))
            ot = ot * lax.rsqrt(jnp.mean(ot * ot, axis=0, keepdims=True) + NORM_EPS)
            o = jnp.transpose(ot) * bn_ref[...] * (1.0 - lam_init)
            o_ref[0, :, h * LANES:(h + 1) * LANES] = o.astype(BF16)


def _diff_attention(q, k, v, lam_params, bnorm_row, past, lam_init, tq, tk, v_transposed):
    b, lq, width = q.shape
    lk_true = k.shape[1]
    nk = -(-lk_true // tk)
    if nk * tk != lk_true:
        assert not v_transposed
        pad = ((0, 0), (0, nk * tk - lk_true), (0, 0))
        k = jnp.pad(k, pad)
        v = jnp.pad(v, pad)
    nq = lq // tq
    i_list, j_list, flag_list = [], [], []
    for i in range(nq):
        first_pos = past + i * tq
        last_pos = first_pos + tq - 1
        j_max = min(((last_pos // CHUNK + 1) * CHUNK - 1) // tk, nk - 1)
        for j in range(j_max + 1):
            key_end = (j + 1) * tk
            all_visible = (key_end - 1) // CHUNK <= first_pos // CHUNK and key_end <= lk_true
            i_list.append(i)
            j_list.append(j)
            flag_list.append((1 if j == j_max else 0) + (0 if all_visible else 2))
    tabs = [jnp.asarray(np.asarray(t, np.int32)) for t in (i_list, j_list, flag_list)]
    if v_transposed:
        body = _attn_t_kernel
        v_spec = pl.BlockSpec((4, LANES, tk), lambda bi, p, it, jt, lt: (0, 0, bi * nk + jt[p]))
        scratch = [pltpu.VMEM((8, 1, tq), F32), pltpu.VMEM((8, LANES + ATTN_ONES_ROWS, tq), F32)]
    else:
        body = _attn_kernel
        v_spec = pl.BlockSpec((1, tk, width), lambda bi, p, it, jt, lt: (bi, jt[p], 0))
        scratch = [pltpu.VMEM((8, tq, LANES), F32), pltpu.VMEM((8, tq, 2 * LANES), F32)]
    grid_spec = pltpu.PrefetchScalarGridSpec(
        num_scalar_prefetch=3,
        grid=(b, len(i_list)),
        in_specs=[pl.BlockSpec((1, tq, width), lambda bi, p, it, jt, lt: (bi, it[p], 0)),
                  pl.BlockSpec((1, tk, width), lambda bi, p, it, jt, lt: (bi, jt[p], 0)),
                  v_spec,
                  pl.BlockSpec((4, HEAD), lambda bi, p, it, jt, lt: (0, 0)),
                  pl.BlockSpec((1, LANES), lambda bi, p, it, jt, lt: (0, 0))],
        out_specs=pl.BlockSpec((1, tq, width), lambda bi, p, it, jt, lt: (bi, it[p], 0)),
        scratch_shapes=scratch,
    )
    return pl.pallas_call(
        functools.partial(body, tq=tq, tk=tk, past=past, lk_true=lk_true, lam_init=lam_init),
        grid_spec=grid_spec,
        out_shape=jax.ShapeDtypeStruct((b, lq, width), BF16),
        compiler_params=pltpu.CompilerParams(dimension_semantics=("parallel", "arbitrary"),
                                             vmem_limit_bytes=VMEM_LIMIT),
        name="diff_attention",
    )(*tabs, q, k, v, lam_params, bnorm_row)


def _mlp_kernel(h_ref, ma_ref, mb_ref, mc_ref, wo_ref, nf_ref, w1_ref, w2_ref, np_ref, wg_ref,
                wp_ref, p_ref, nfin_ref, o_ref, *, tf, final_norm):
    mixed = (jnp.dot(ma_ref[...], wo_ref[0:256, :], preferred_element_type=F32)
             + jnp.dot(mb_ref[...], wo_ref[256:768, :], preferred_element_type=F32)
             + jnp.dot(mc_ref[...], wo_ref[768:1024, :], preferred_element_type=F32))
    h1 = h_ref[...] + mixed
    xn = _rmsnorm_rows(h1, nf_ref[...]).astype(BF16)
    h2 = h1
    for f0 in range(0, w1_ref.shape[1], tf):
        u = jnp.maximum(jnp.dot(xn, w1_ref[:, f0:f0 + tf], preferred_element_type=F32), 0.0)
        h2 = h2 + jnp.dot((u * u).astype(BF16), w2_ref[f0:f0 + tf, :], preferred_element_type=F32)
    gate = jax.nn.sigmoid(jnp.dot(_rmsnorm_rows(h2, np_ref[...]).astype(BF16), wg_ref[...],
                                  preferred_element_type=F32))
    h3 = h2 + gate * jnp.dot(p_ref[...].astype(BF16), wp_ref[...], preferred_element_type=F32)
    if final_norm:
        h3 = _rmsnorm_rows(h3, nfin_ref[...])
    o_ref[...] = h3


def _mix_mlp(h2d, mix_a, mix_b, mix_c, w_out, norm_ffn, w_ff1, w_ff2, norm_ple, w_gate, w_proj,
             p2d, norm_final, final_norm, tm, tf):
    t, d = h2d.shape
    dff = w_ff1.shape[1]
    row = lambda i: (i, 0)
    fixed = lambda i: (0, 0)
    resident = lambda shape: pl.BlockSpec(shape, fixed, pipeline_mode=pl.Buffered(1))
    return pl.pallas_call(
        functools.partial(_mlp_kernel, tf=tf, final_norm=final_norm),
        grid=(t // tm,),
        in_specs=[pl.BlockSpec((tm, d), row), pl.BlockSpec((tm, 256), row),
                  pl.BlockSpec((tm, 512), row), pl.BlockSpec((tm, 256), row),
                  resident((d, d)), resident((1, d)), resident((d, dff)), resident((dff, d)),
                  resident((1, d)), resident((d, d)), resident((p2d.shape[1], d)),
                  pl.BlockSpec((tm, p2d.shape[1]), row), resident((1, d))],
        out_specs=pl.BlockSpec((tm, d), row),
        out_shape=jax.ShapeDtypeStruct((t, d), F32),
        compiler_params=pltpu.CompilerParams(dimension_semantics=("parallel",),
                                             vmem_limit_bytes=VMEM_LIMIT),
        name="mix_mlp",
    )(h2d, mix_a, mix_b, mix_c, w_out, norm_ffn, w_ff1, w_ff2, norm_ple, w_gate, w_proj, p2d,
      norm_final)


def _rope_tables(past, seq_len):
    half = HEAD // 2
    inv = ROPE_THETA ** (-2.0 * jnp.arange(half, dtype=F32) / HEAD)
    pos = past + jnp.arange(seq_len, dtype=jnp.int32)
    ang = pos.astype(F32)[:, None] * inv[None, :]
    cos, sin, zero = jnp.cos(ang), jnp.sin(ang), jnp.zeros_like(ang)
    rep = LANES // HEAD
    cos_t = jnp.tile(jnp.concatenate([cos, cos], axis=1), (1, rep))
    sa_t = jnp.tile(jnp.concatenate([-sin, zero], axis=1), (1, rep))
    sb_t = jnp.tile(jnp.concatenate([zero, sin], axis=1), (1, rep))
    return cos_t, sa_t, sb_t


def _pad_lanes(v, width=LANES):
    return jnp.pad(v, (0, width - v.shape[0]))[None, :]


def _prep_layer(i, norm_mix, w_in, a_conv_w, a_A_log, a_dt_bias, a_norm,
                b_lam_q1, b_lam_k1, b_lam_q2, b_lam_k2, b_norm,
                c_mu, c_w0, c_w_up, c_a0, c_a_up, c_g_up, c_k_k, c_k_a, c_r_k, c_ln_w, c_ln_b,
                w_out, norm_ffn, w_ff1, w_ff2, norm_ple, w_ple_gate, w_ple_proj):
    w = w_in[i]
    d = w.shape[0]
    w_perm = jnp.concatenate(
        [w[:, 0:1024], w[:, 1032:3592], w[:, 1024:1032],
         jnp.zeros((d, _PROJ_WIDTH - 3592), w.dtype)], axis=1).astype(BF16)
    zeros_rank = jnp.zeros_like(c_w_up[i])
    rwkv = (c_mu[i][None, :], c_w0[i][None, :],
            jnp.concatenate([c_w_up[i], zeros_rank], axis=0).astype(BF16), c_a0[i][None, :],
            jnp.concatenate([jnp.zeros_like(c_a_up[i]), c_a_up[i]], axis=0).astype(BF16),
            c_g_up[i].astype(BF16), c_k_k[i][None, :], c_k_a[i][None, :],
            c_r_k[i].reshape(1, -1), c_ln_w[i][None, :], c_ln_b[i][None, :])
    return dict(
        norm_mix=norm_mix[i][None, :], w_in=w_perm, conv_w=a_conv_w[i],
        alog=_pad_lanes(a_A_log[i]), dtb=_pad_lanes(a_dt_bias[i]),
        anorm=jnp.tile(a_norm[i], 4)[None, :],
        lam=jnp.stack([b_lam_q1[i], b_lam_k1[i], b_lam_q2[i], b_lam_k2[i]], axis=0),
        bnorm=b_norm[i][None, :], rwkv=rwkv,
        w_out=w_out[i].astype(BF16), norm_ffn=norm_ffn[i][None, :],
        w_ff1=w_ff1[i].astype(BF16), w_ff2=w_ff2[i].astype(BF16),
        norm_ple=norm_ple[i][None, :], w_gate=w_ple_gate[i].astype(BF16),
        w_proj=w_ple_proj[i].astype(BF16))


def _pick_tile(n, target):
    t = min(n, target)
    while n % t:
        t //= 2
    return t


def _trunk(x, p, cache_k, cache_v, conv_buf, delta_s, shift_prev, wkv_s, layers, norm_final):
    b, seq, d = x.shape
    depth = len(layers)
    past = cache_k.shape[2]
    t = b * seq
    tm = _pick_tile(t, PROJ_TM)
    rope_tabs = _rope_tables(past, seq)
    h = x.reshape(t, d)
    states = []
    kv_stacked = None
    attn_tq = _pick_tile(seq, ATTN_TQ)
    attn_tk = ATTN_TK if past + seq >= 8 * ATTN_TK else ATTN_TK // 2
    v_transposed = past == 0 and attn_tq % LANES == 0 and seq % attn_tk == 0 and tm % LANES == 0
    for i, lp in enumerate(layers):
        a_qkv, a_z, a_ab, q_b, k_f, v_f, k_b, v_b, c_raw = _input_projection(
            h, lp["norm_mix"], lp["w_in"], rope_tabs, seq, tm, i, depth, kv_stacked, v_transposed)
        kv_stacked = (k_f, v_f)
        mix_a, conv_n, delta_n = _gated_delta(
            a_qkv.reshape(b, seq, -1), a_z.reshape(b, seq, -1), a_ab.reshape(b, seq, -1),
            conv_buf[i], delta_s[i], lp["conv_w"], lp["alog"], lp["dtb"], lp["anorm"])
        k_all = k_b.reshape(b, seq, -1)
        v_all = v_b if v_transposed else v_b.reshape(b, seq, -1)
        if past:
            k_all = jnp.concatenate([cache_k[i].reshape(b, past, -1).astype(BF16), k_all], axis=1)
            v_all = jnp.concatenate([cache_v[i].reshape(b, past, -1).astype(BF16), v_all], axis=1)
        lam_init = 0.8 - 0.6 * math.exp(-0.3 * i)
        mix_b = _diff_attention(q_b.reshape(b, seq, -1), k_all, v_all, lp["lam"], lp["bnorm"],
                                past, lam_init, attn_tq, attn_tk, v_transposed)
        mix_c, shift_n, wkv_n = _rwkv7(c_raw.reshape(b, seq, -1), shift_prev[i][:, None, :],
                                       wkv_s[i], lp["rwkv"])
        h = _mix_mlp(h, mix_a.reshape(t, -1), mix_b.reshape(t, -1), mix_c.reshape(t, -1),
                     lp["w_out"], lp["norm_ffn"], lp["w_ff1"], lp["w_ff2"], lp["norm_ple"],
                     lp["w_gate"], lp["w_proj"], p[i].reshape(t, -1), norm_final[None, :],
                     i == depth - 1, _pick_tile(t, MLP_TM), MLP_TF)
        states.append((conv_n, delta_n, shift_n[:, 0, :], wkv_n))
    conv_all, delta_all, shift_all, wkv_all = (
        jnp.stack([st[j] for st in states], axis=0) for j in range(4))
    k_all_layers, v_all_layers = (a.reshape(depth, b, seq, 4, LANES) for a in kv_stacked)
    return h.reshape(b, seq, d), [conv_all, delta_all, k_all_layers, v_all_layers, shift_all, wkv_all]


def kernel(x_prompt, x_sample, cache_b_k, cache_b_v, state_a_conv, state_a_delta, state_c_shift, state_c_wkv, p_prompt, p_sample, norm_mix, w_in, a_conv_w, a_A_log, a_dt_bias, a_norm, b_lam_q1, b_lam_k1, b_lam_q2, b_lam_k2, b_norm, c_mu, c_w0, c_w_up, c_a0, c_a_up, c_g_up, c_k_k, c_k_a, c_r_k, c_ln_w, c_ln_b, w_out, norm_ffn, w_ff1, w_ff2, norm_ple, w_ple_gate, w_ple_proj, norm_final):
    depth = w_in.shape[0]
    layers = [_prep_layer(i, norm_mix, w_in, a_conv_w, a_A_log, a_dt_bias, a_norm,
                          b_lam_q1, b_lam_k1, b_lam_q2, b_lam_k2, b_norm,
                          c_mu, c_w0, c_w_up, c_a0, c_a_up, c_g_up, c_k_k, c_k_a, c_r_k, c_ln_w, c_ln_b,
                          w_out, norm_ffn, w_ff1, w_ff2, norm_ple, w_ple_gate, w_ple_proj)
              for i in range(depth)]
    bp = x_prompt.shape[0]
    dt = x_prompt.dtype
    zeros = lambda ref: jnp.zeros((depth, bp) + ref.shape[2:], dt)
    empty_k = jnp.zeros((depth, bp, 0) + cache_b_k.shape[3:], dt)
    empty_v = jnp.zeros((depth, bp, 0) + cache_b_v.shape[3:], dt)
    y_prompt, st_p = _trunk(x_prompt, p_prompt, empty_k, empty_v, zeros(state_a_conv),
                            zeros(state_a_delta), zeros(state_c_shift), zeros(state_c_wkv),
                            layers, norm_final)
    y_sample, st_s = _trunk(x_sample, p_sample, cache_b_k, cache_b_v, state_a_conv, state_a_delta,
                            state_c_shift, state_c_wkv, layers, norm_final)
    return (y_prompt, y_sample, *st_p, *st_s)
```

```python
import functools
import math

import numpy as np
import jax
import jax.numpy as jnp
from jax import lax
from jax.experimental import pallas as pl
from jax.experimental.pallas import tpu as pltpu

F32 = jnp.float32
BF16 = jnp.bfloat16

CHUNK = 64
ROPE_THETA = 10000.0
NORM_EPS = 1e-6
L2_EPS = 1e-6
C_LN_EPS = 64e-5
RWKV_DECAY_OFFSET = 0.5
MASK_VALUE = float(np.finfo(np.float32).min)

LANES = 128
HEAD = 64
VMEM_LIMIT = 56 * 1024 * 1024
PROJ_TM = 512
MLP_TM = 512
MLP_TF = 1024
CHUNKS_PER_STEP = 8
ATTN_TQ = 512
ATTN_TK = 1024
ATTN_ONES_ROWS = 16
ATTN_Q_SCALE = (64 ** -0.5) * math.log2(math.e)


def _iota(shape, axis):
    return lax.broadcasted_iota(jnp.int32, shape, axis)


_NN = (((1,), (0,)), ((), ()))
_NT = (((1,), (1,)), ((), ()))
_TN = (((0,), (0,)), ((), ()))


def _mm(a, b, dims=_NN):
    return lax.dot_general(a.astype(BF16), b.astype(BF16), dims, preferred_element_type=F32)


def _split3(x):
    h = x.astype(BF16)
    r = x - h.astype(F32)
    m = r.astype(BF16)
    lo = (r - m.astype(F32)).astype(BF16)
    return h, m, lo


def _mm_exact_rhs(x, ones_bf16, dims=_NN):
    h, m, lo = _split3(x)
    d = functools.partial(lax.dot_general, dimension_numbers=dims, preferred_element_type=F32)
    return d(h, ones_bf16) + d(m, ones_bf16) + d(lo, ones_bf16)


def _mm_exact_lhs(ones_bf16, x, dims=_NN):
    h, m, lo = _split3(x)
    d = functools.partial(lax.dot_general, dimension_numbers=dims, preferred_element_type=F32)
    return d(ones_bf16, h) + d(ones_bf16, m) + d(ones_bf16, lo)


def _selector_matrices(tl, c):
    tok = np.arange(tl)
    same = (tok[:, None] // c) == (tok[None, :] // c)
    lane = np.arange(4 * HEAD)
    src = np.arange(LANES)
    mats = dict(
        seg=(lane[:, None] // HEAD) == (lane[None, :] // HEAD),
        tril=same & (tok[:, None] >= tok[None, :]),
        triu=same & (tok[:, None] <= tok[None, :]),
        last=tok[None, :] == (tok[:, None] // c) * c + c - 1,
        expg=src[:, None] == lane[None, :] // HEAD,
        expb=src[:, None] == lane[None, :] // HEAD + 4)
    return {name: jnp.asarray(m.astype(np.float32), BF16) for name, m in mats.items()}


def _chunking(seq):
    c = min(CHUNK, seq)
    assert c & (c - 1) == 0 and seq % c == 0, seq
    g = min(CHUNKS_PER_STEP, seq // c)
    while (seq // c) % g:
        g -= 1
    return c, g


def _rmsnorm_rows(x, g):
    return x * lax.rsqrt(jnp.mean(x * x, axis=-1, keepdims=True) + NORM_EPS) * g


def _unit_lower_inverses(a_list, c):
    row = _iota((c, c), 0)
    col = _iota((c, c), 1)

    def same_block(shift):
        return jnp.right_shift(row, shift) == jnp.right_shift(col, shift)

    eye = (row == col).astype(F32)
    leaf = same_block(3)
    ns = [jnp.where(leaf, -a, 0.0) for a in a_list]
    ts = [eye + n for n in ns]
    n2s = [_mm(n, n) for n in ns]
    ts = [t + _mm(t, n2) for t, n2 in zip(ts, n2s)]
    n4s = [_mm(n2, n2) for n2 in n2s]
    ts = [t + _mm(t, n4) for t, n4 in zip(ts, n4s)]
    shift = 3
    while (1 << shift) < c:
        off_mask = same_block(shift + 1) & jnp.logical_not(same_block(shift))
        tos = [_mm(t, jnp.where(off_mask, a, 0.0)) for t, a in zip(ts, a_list)]
        ts = [t - _mm(to, t) for t, to in zip(ts, tos)]
        shift += 1
    return ts


_PROJ_A_QKV = 0
_PROJ_A_Z = 768
_PROJ_B_Q = 1024
_PROJ_B_K = 1536
_PROJ_B_V = 2048
_PROJ_C = 2560
_PROJ_A_AB = 3584
_PROJ_WIDTH = 3712


def _proj_kernel(*refs, n_alias, v_transposed):
    x_ref, g_ref, w_ref, cos_ref, sa_ref, sb_ref = refs[:6]
    aqkv_ref, az_ref, aab_ref, q_ref, k_ref, v_ref, kb_ref, vb_ref, c_ref = refs[6 + n_alias:]
    xn = _rmsnorm_rows(x_ref[...], g_ref[...]).astype(BF16)

    def proj(c0, c1):
        return jnp.dot(xn, w_ref[:, c0:c1], preferred_element_type=F32)

    a_all = proj(_PROJ_A_QKV, _PROJ_B_Q)
    aqkv_ref[...] = a_all[:, :_PROJ_A_Z]
    az_ref[...] = a_all[:, _PROJ_A_Z:]
    cos = cos_ref[...]
    sa = sa_ref[...]
    sb = sb_ref[...]

    def rope(x):
        return x * cos + pltpu.roll(x, LANES - 32, 1) * sa + pltpu.roll(x, 32, 1) * sb

    q_all = proj(_PROJ_B_Q, _PROJ_B_K)
    k_all = proj(_PROJ_B_K, _PROJ_B_V)
    for h in range(4):
        lo, hi = h * LANES, (h + 1) * LANES
        q_ref[:, lo:hi] = (rope(q_all[:, lo:hi]) * ATTN_Q_SCALE).astype(BF16)
        k = rope(k_all[:, lo:hi])
        k_ref[:, h, :] = k
        kb_ref[:, lo:hi] = k.astype(BF16)
    v = proj(_PROJ_B_V, _PROJ_C)
    for h in range(4):
        v_ref[:, h, :] = v[:, h * LANES:(h + 1) * LANES]
    if v_transposed:
        for h in range(4):
            vb_ref[h] = jnp.transpose(v[:, h * LANES:(h + 1) * LANES]).astype(BF16)
    else:
        vb_ref[...] = v.astype(BF16)
    c_all = proj(_PROJ_C, _PROJ_WIDTH)
    c_ref[...] = c_all[:, :_PROJ_A_AB - _PROJ_C]
    aab_ref[...] = c_all[:, _PROJ_A_AB - _PROJ_C:]


def _input_projection(x2d, g_row, w_bf16, rope_tabs, seq_len, tm, layer, depth, kv_stacked, v_transposed):
    t, d = x2d.shape
    nt = t // tm
    cos_t, sa_t, sb_t = rope_tabs
    if tm >= seq_len:
        reps = tm // seq_len
        cos_t, sa_t, sb_t = (jnp.tile(a, (reps, 1)) for a in (cos_t, sa_t, sb_t))
        tab_map = lambda i: (0, 0)
    else:
        per_seq = seq_len // tm
        tab_map = lambda i: (i % per_seq, 0)
    row = lambda i: (i, 0)
    fixed = lambda i: (0, 0)
    tab_spec = pl.BlockSpec((tm, LANES), tab_map)
    widths = [(768, F32), (256, F32), (128, F32), (512, BF16), (512, F32), (512, F32),
              (512, BF16), (512, BF16), (1024, F32)]
    stacked = (4, 5)
    out_specs = [pl.BlockSpec((None, tm, 4, LANES), lambda i: (layer, i, 0, 0)) if n in stacked
                 else pl.BlockSpec((tm, w), row) for n, (w, _) in enumerate(widths)]
    out_shape = [jax.ShapeDtypeStruct((depth, t, 4, LANES) if n in stacked else (t, w), dt)
                 for n, (w, dt) in enumerate(widths)]
    if v_transposed:
        out_specs[7] = pl.BlockSpec((None, 4, LANES, tm), lambda i: (i, 0, 0, 0))
        out_shape[7] = jax.ShapeDtypeStruct((nt, 4, LANES, tm), BF16)
    in_specs = [pl.BlockSpec((tm, d), row), pl.BlockSpec((1, d), fixed),
                pl.BlockSpec((d, _PROJ_WIDTH), fixed), tab_spec, tab_spec, tab_spec]
    operands = [x2d, g_row, w_bf16, cos_t, sa_t, sb_t]
    aliases = {}
    if kv_stacked is not None:
        for out_idx, arr in zip(stacked, kv_stacked):
            aliases[len(operands)] = out_idx
            in_specs.append(pl.BlockSpec(memory_space=pl.ANY))
            operands.append(arr)
    return pl.pallas_call(
        functools.partial(_proj_kernel, n_alias=len(aliases), v_transposed=v_transposed),
        grid=(nt,),
        in_specs=in_specs,
        out_specs=out_specs,
        out_shape=out_shape,
        input_output_aliases=aliases,
        compiler_params=pltpu.CompilerParams(dimension_semantics=("parallel",),
                                             vmem_limit_bytes=VMEM_LIMIT),
        name="input_projection",
    )(*operands)


def _delta_kernel(qkv_ref, z_ref, ab_ref, conv0_ref, s0_ref, convw_ref, alog_ref, dtb_ref, anorm_ref,
                  seg_ref, tril_ref, triu_ref, last_ref, expg_ref, expb_ref,
                  mix_ref, convn_ref, sn_ref, xp_ref, s_ref, o_ref, *, c, g, nl):
    l = pl.program_id(1)
    tl = c * g

    @pl.when(l == 0)
    def _():
        xp_ref[5:8, :] = conv0_ref[0]
        s_ref[...] = s0_ref[0]

    x = qkv_ref[0]
    xp_ref[8:8 + tl, :] = x
    w = convw_ref[...]
    y = xp_ref[5:5 + tl, :] * w[0:1]
    y = y + xp_ref[6:6 + tl, :] * w[1:2]
    y = y + xp_ref[7:7 + tl, :] * w[2:3]
    y = y + x * w[3:4]
    tail = xp_ref[5 + tl:8 + tl, :]
    xp_ref[5:8, :] = tail

    @pl.when(l == nl - 1)
    def _():
        convn_ref[0] = tail

    act = y * jax.nn.sigmoid(y)
    seg = seg_ref[...]

    def l2n(t):
        return t * lax.rsqrt(_mm_exact_rhs(t * t, seg) + L2_EPS)

    q = l2n(act[:, 0:256]) * (HEAD ** -0.5)
    k = l2n(act[:, 256:512])
    v = act[:, 512:768]

    ab = ab_ref[0]
    gl = -jnp.exp(alog_ref[...]) * jax.nn.softplus(ab + dtb_ref[...])
    beta_w = _mm_exact_rhs(jax.nn.sigmoid(ab), expb_ref[...])
    g_w = _mm_exact_lhs(tril_ref[...], _mm_exact_rhs(gl, expg_ref[...]))
    g_rows = _mm_exact_rhs(gl, triu_ref[...], _TN)
    g_last_w = _mm_exact_lhs(last_ref[...], g_w)
    eg_w = jnp.exp(g_w)
    v_beta = v * beta_w
    k_beta_eg = k * (beta_w * eg_w)
    q_dec = q * eg_w
    k_tail_w = k * jnp.exp(g_last_w - g_w)
    decay_end_w = jnp.exp(g_last_w)

    row = _iota((c, c), 0)
    col = _iota((c, c), 1)
    lower = row >= col
    strict = row > col

    probs = [(slice(ci * c, (ci + 1) * c), h) for ci in range(g) for h in range(4)]
    heads = lambda h: slice(h * HEAD, (h + 1) * HEAD)
    decay = [jnp.where(lower, jnp.exp(jnp.where(lower, g_w[rows, heads(h)][:, :c] - g_rows[h:h + 1, rows], 0.0)),
                       0.0) for rows, h in probs]
    kh = [k[rows, heads(h)] for rows, h in probs]
    kq = [_mm(jnp.concatenate([kk_, q[rows, heads(h)]], axis=0), kk_, _NT)
          for kk_, (rows, h) in zip(kh, probs)]
    t_inv = _unit_lower_inverses(
        [jnp.where(strict, beta_w[rows, heads(h)][:, :c] * x[:c] * d, 0.0)
         for x, d, (rows, h) in zip(kq, decay, probs)], c)
    sol = [_mm(t, jnp.concatenate([v_beta[rows, heads(h)], k_beta_eg[rows, heads(h)]], axis=1))
           for t, (rows, h) in zip(t_inv, probs)]
    qk = [jnp.where(lower, x[c:] * d, 0.0) for x, d in zip(kq, decay)]
    kt_sol = [_mm(k_tail_w[rows, heads(h)], so, _TN) for so, (rows, h) in zip(sol, probs)]
    qk_sol = [_mm(x, so) for x, so in zip(qk, sol)]
    q_eff = [q_dec[rows, heads(h)] - x[:, HEAD:] for x, (rows, h) in zip(qk_sol, probs)]
    decay_end = [decay_end_w[rows.stop - 1:rows.stop, heads(h)] for rows, h in probs]

    states = [s_ref[h] for h in range(4)]
    for ci in range(g):
        rows = slice(ci * c, (ci + 1) * c)
        idx = [ci * 4 + h for h in range(4)]
        o_new = [_mm(q_eff[n], states[h]) + qk_sol[n][:, :HEAD] for h, n in enumerate(idx)]
        states = [states[h] * decay_end[n] + kt_sol[n][:, :HEAD] - _mm(kt_sol[n][:, HEAD:], states[h])
                  for h, n in enumerate(idx)]
        for h in range(4):
            o_ref[rows, heads(h)] = o_new[h]
    for h in range(4):
        s_ref[h] = states[h]

    o = o_ref[...]
    ms = _mm_exact_rhs(o * o, seg) * (1.0 / HEAD)
    z = z_ref[0]
    ao = (o * lax.rsqrt(ms + NORM_EPS) * anorm_ref[...]) * (z * jax.nn.sigmoid(z))
    mix_ref[0] = ao.astype(BF16)

    @pl.when(l == nl - 1)
    def _():
        sn_ref[0] = s_ref[...]


def _gated_delta(a_qkv, a_z, a_ab, conv0, s0, conv_w, alog_row, dtb_row, anorm_row):
    b, seq, _ = a_qkv.shape
    c, g = _chunking(seq)
    tl = c * g
    nl = seq // tl
    tile = lambda i, l: (i, l, 0)
    per_b3 = lambda i, l: (i, 0, 0)
    per_b4 = lambda i, l: (i, 0, 0, 0)
    fixed = lambda i, l: (0, 0)
    sel = _selector_matrices(tl, c)
    consts = [sel[name] for name in ("seg", "tril", "triu", "last", "expg", "expb")]
    return pl.pallas_call(
        functools.partial(_delta_kernel, c=c, g=g, nl=nl),
        grid=(b, nl),
        in_specs=[pl.BlockSpec((1, tl, 768), tile), pl.BlockSpec((1, tl, 256), tile),
                  pl.BlockSpec((1, tl, LANES), tile), pl.BlockSpec((1, 3, 768), per_b3),
                  pl.BlockSpec((1, 4, HEAD, HEAD), per_b4), pl.BlockSpec((4, 768), fixed),
                  pl.BlockSpec((1, LANES), fixed), pl.BlockSpec((1, LANES), fixed),
                  pl.BlockSpec((1, 256), fixed)] + [pl.BlockSpec(m.shape, fixed) for m in consts],
        out_specs=[pl.BlockSpec((1, tl, 256), tile), pl.BlockSpec((1, 3, 768), per_b3),
                   pl.BlockSpec((1, 4, HEAD, HEAD), per_b4)],
        out_shape=[jax.ShapeDtypeStruct((b, seq, 256), BF16),
                   jax.ShapeDtypeStruct((b, 3, 768), F32),
                   jax.ShapeDtypeStruct((b, 4, HEAD, HEAD), F32)],
        scratch_shapes=[pltpu.VMEM((tl + 8, 768), F32), pltpu.VMEM((4, HEAD, HEAD), F32),
                        pltpu.VMEM((tl, 256), F32)],
        compiler_params=pltpu.CompilerParams(dimension_semantics=("parallel", "arbitrary"),
                                             vmem_limit_bytes=VMEM_LIMIT),
        name="gated_delta",
    )(a_qkv, a_z, a_ab, conv0, s0, conv_w, alog_row, dtb_row, anorm_row, *consts)


def _rwkv_kernel(c_ref, shift0_ref, s0_ref, mu_ref, w0_ref, wup_ref, a0_ref, aup_ref, gup_ref,
                 kk_ref, ka_ref, rk_ref, lnw_ref, lnb_ref, seg_ref, tril_ref,
                 mix_ref, shiftn_ref, sn_ref, xp_ref, s_ref, y_ref, *, c, g, nl):
    l = pl.program_id(1)
    tl = c * g

    @pl.when(l == 0)
    def _():
        xp_ref[7:8, :] = shift0_ref[0]
        s_ref[...] = s0_ref[0]

    raw = c_ref[0]
    xp_ref[8:8 + tl, :] = raw
    prev = xp_ref[7:7 + tl, :]
    last = raw[tl - 1:tl, :]
    xp_ref[7:8, :] = last

    @pl.when(l == nl - 1)
    def _():
        shiftn_ref[0] = last

    x = raw + (prev - raw) * mu_ref[...]
    cr, ck, cv = x[:, 0:256], x[:, 256:512], x[:, 512:768]
    c_wa = x[:, 768:896]
    c_g = x[:, 896:1024]
    w_log = -jnp.exp(-jax.nn.softplus(-(w0_ref[...] + _mm(jnp.tanh(c_wa), wup_ref[...])))
                     - RWKV_DECAY_OFFSET)
    ca = jax.nn.sigmoid(a0_ref[...] + _mm(c_wa, aup_ref[...]))
    cg = _mm(jax.nn.sigmoid(c_g), gup_ref[...])
    seg = seg_ref[...]
    kkv = ck * kk_ref[...]
    kk = kkv * lax.rsqrt(_mm_exact_rhs(kkv * kkv, seg) + L2_EPS)
    ck = ck * (1.0 + (ca - 1.0) * ka_ref[...])

    g_cum = _mm_exact_lhs(tril_ref[...], w_log)
    e_pos = jnp.exp(g_cum)
    e_neg = jnp.exp(-g_cum)
    a_t = -kk * jnp.exp(g_cum - w_log)
    b_t = (kk * ca) * e_neg
    k_t = ck * e_neg
    r_t = cr * e_pos

    row = _iota((c, c), 0)
    col = _iota((c, c), 1)
    lower = row >= col
    strict = row > col

    probs = [(slice(ci * c, (ci + 1) * c), h) for ci in range(g) for h in range(4)]
    heads = lambda h: slice(h * HEAD, (h + 1) * HEAD)
    bh = [b_t[rows, heads(h)] for rows, h in probs]
    kh = [k_t[rows, heads(h)] for rows, h in probs]
    vh = [cv[rows, heads(h)] for rows, h in probs]
    ar = [jnp.concatenate([a_t[rows, heads(h)], r_t[rows, heads(h)]], axis=0) for rows, h in probs]
    pb = [_mm(x, y_, _NT) for x, y_ in zip(ar, bh)]
    pk = [_mm(x, y_, _NT) for x, y_ in zip(ar, kh)]
    t_inv = _unit_lower_inverses([-jnp.where(strict, x[:c], 0.0) for x in pb], c)
    n_rb = [jnp.where(lower, x[c:], 0.0) for x in pb]
    mn = [_mm(jnp.concatenate([jnp.where(strict, x[:c], 0.0), jnp.where(lower, x[c:], 0.0)], axis=0), vv)
          for x, vv in zip(pk, vh)]
    bk = [jnp.concatenate([x, y_], axis=0) for x, y_ in zip(bh, kh)]
    g_end = [e_pos[rows.stop - 1:rows.stop, heads(h)] for rows, h in probs]
    ta = [_mm(t, jnp.concatenate([x[:c], m_[:c]], axis=1)) for t, x, m_ in zip(t_inv, ar, mn)]
    p_mat = [_mm(x[:, :HEAD], y_, _TN) for x, y_ in zip(ta, bh)]
    c_mat = [_mm(jnp.concatenate([x[:, HEAD:], vv], axis=0), y_, _TN) for x, vv, y_ in zip(ta, vh, bk)]
    nr = [_mm(x, y_) for x, y_ in zip(n_rb, ta)]
    r_eff = [x[c:] + y_[:, :HEAD] for x, y_ in zip(ar, nr)]
    y_off = [x[:, HEAD:] + m_[c:] for x, m_ in zip(nr, mn)]

    states = [s_ref[h] for h in range(4)]
    for ci in range(g):
        rows = slice(ci * c, (ci + 1) * c)
        idx = [ci * 4 + h for h in range(4)]
        y_new = [_mm(r_eff[n], states[h], _NT) + y_off[n] for h, n in enumerate(idx)]
        states = [(states[h] + _mm(states[h], p_mat[n]) + c_mat[n]) * g_end[n] for h, n in enumerate(idx)]
        for h in range(4):
            y_ref[rows, heads(h)] = y_new[h]
    for h in range(4):
        s_ref[h] = states[h]

    y = y_ref[...]
    mean = _mm_exact_rhs(y, seg) * (1.0 / HEAD)
    yc = y - mean
    var = _mm_exact_rhs(yc * yc, seg) * (1.0 / HEAD)
    cy = (yc * lax.rsqrt(var + C_LN_EPS)) * lnw_ref[...] + lnb_ref[...]
    bonus = _mm_exact_rhs(cr * ck * rk_ref[...], seg) * cv
    mix_ref[0] = ((cy + bonus) * cg).astype(BF16)

    @pl.when(l == nl - 1)
    def _():
        sn_ref[0] = s_ref[...]


def _rwkv7(c_raw, shift0, s0, params):
    b, seq, width = c_raw.shape
    c, g = _chunking(seq)
    tl = c * g
    nl = seq // tl
    tile = lambda i, l: (i, l, 0)
    per_b3 = lambda i, l: (i, 0, 0)
    per_b4 = lambda i, l: (i, 0, 0, 0)
    fixed = lambda i, l: (0, 0)
    sel = _selector_matrices(tl, c)
    params = tuple(params) + (sel["seg"], sel["tril"])
    param_specs = [pl.BlockSpec(p.shape, fixed) for p in params]
    return pl.pallas_call(
        functools.partial(_rwkv_kernel, c=c, g=g, nl=nl),
        grid=(b, nl),
        in_specs=[pl.BlockSpec((1, tl, width), tile), pl.BlockSpec((1, 1, width), per_b3),
                  pl.BlockSpec((1, 4, HEAD, HEAD), per_b4)] + param_specs,
        out_specs=[pl.BlockSpec((1, tl, 256), tile), pl.BlockSpec((1, 1, width), per_b3),
                   pl.BlockSpec((1, 4, HEAD, HEAD), per_b4)],
        out_shape=[jax.ShapeDtypeStruct((b, seq, 256), BF16),
                   jax.ShapeDtypeStruct((b, 1, width), F32),
                   jax.ShapeDtypeStruct((b, 4, HEAD, HEAD), F32)],
        scratch_shapes=[pltpu.VMEM((tl + 8, width), F32), pltpu.VMEM((4, HEAD, HEAD), F32),
                        pltpu.VMEM((tl, 256), F32)],
        compiler_params=pltpu.CompilerParams(dimension_semantics=("parallel", "arbitrary"),
                                             vmem_limit_bytes=VMEM_LIMIT),
        name="rwkv7",
    )(c_raw, shift0, s0, *params)


def _attn_kernel(i_tab, j_tab, flag_tab, q_ref, k_ref, v_ref, lam_ref, bn_ref, o_ref,
                 m_ref, acc_ref, *, tq, tk, past, lk_true, lam_init):
    p = pl.program_id(1)
    i = i_tab[p]
    j = j_tab[p]

    @pl.when(j == 0)
    def _():
        m_ref[...] = jnp.full(m_ref.shape, MASK_VALUE, F32)
        acc_ref[...] = jnp.zeros(acc_ref.shape, F32)

    nslab = tk // LANES

    def step(masked):
        if masked:
            q_pos = past + i * tq + _iota((tq, tk), 0)
            k_pos = j * tk + _iota((tq, tk), 1)
            visible = (jnp.right_shift(k_pos, 6) <= jnp.right_shift(q_pos, 6)) & (k_pos < lk_true)
        q = q_ref[0]
        k = k_ref[0]
        v = v_ref[0]

        def scores(hm):
            sl = slice(hm * HEAD, (hm + 1) * HEAD)
            return lax.dot_general(q[:, sl], k[:, sl], _NT, preferred_element_type=F32)

        ones = jnp.ones((tk, LANES), BF16)
        s_next = scores(0)
        for hm in range(8):
            h = hm // 2
            s = s_next
            if hm + 1 < 8:
                s_next = scores(hm + 1)
            if masked:
                s = jnp.where(visible, s, MASK_VALUE)
            slabs = [s[:, n * LANES:(n + 1) * LANES] for n in range(nslab)]
            m_cur = functools.reduce(jnp.maximum, slabs)
            m_prev = m_ref[hm]
            m_new = jnp.maximum(m_prev, jnp.max(m_cur, axis=1, keepdims=True))
            alpha = jnp.exp2(m_prev - m_new)
            e = jnp.concatenate([jnp.exp2(sb - m_new).astype(BF16) for sb in slabs], axis=1)
            v_ones = jnp.concatenate([v[:, h * LANES:(h + 1) * LANES], ones], axis=1)
            acc_ref[hm] = (jnp.concatenate([alpha, alpha], axis=1) * acc_ref[hm]
                           + jnp.dot(e, v_ones, preferred_element_type=F32))
            m_ref[hm] = m_new

    needs_mask = jnp.bitwise_and(flag_tab[p], 2) != 0

    @pl.when(needs_mask)
    def _():
        step(True)

    @pl.when(jnp.logical_not(needs_mask))
    def _():
        step(False)

    @pl.when(jnp.bitwise_and(flag_tab[p], 1) != 0)
    def _():
        lp = lam_ref[...]
        lam = (jnp.exp(jnp.sum(lp[0:1] * lp[1:2], axis=1, keepdims=True))
               - jnp.exp(jnp.sum(lp[2:3] * lp[3:4], axis=1, keepdims=True)) + lam_init)
        for h in range(4):
            a1 = acc_ref[2 * h]
            a2 = acc_ref[2 * h + 1]
            o = a1[:, :LANES] / a1[:, LANES:] - lam * (a2[:, :LANES] / a2[:, LANES:])
            o = _rmsnorm_rows(o, bn_ref[...]) * (1.0 - lam_init)
            o_ref[0, :, h * LANES:(h + 1) * LANES] = o.astype(BF16)


def _attn_t_kernel(i_tab, j_tab, flag_tab, q_ref, k_ref, vt_ref, lam_ref, bn_ref, o_ref,
                   m_ref, acc_ref, *, tq, tk, past, lk_true, lam_init):
    p = pl.program_id(1)
    i = i_tab[p]
    j = j_tab[p]

    @pl.when(j == 0)
    def _():
        m_ref[...] = jnp.full(m_ref.shape, MASK_VALUE, F32)
        acc_ref[...] = jnp.zeros(acc_ref.shape, F32)

    def step(masked):
        if masked:
            k_pos = j * tk + _iota((tk, tq), 0)
            q_pos = past + i * tq + _iota((tk, tq), 1)
            visible = (jnp.right_shift(k_pos, 6) <= jnp.right_shift(q_pos, 6)) & (k_pos < lk_true)
        q = q_ref[0]
        k = k_ref[0]

        def scores(hm):
            sl = slice(hm * HEAD, (hm + 1) * HEAD)
            return lax.dot_general(k[:, sl], q[:, sl], _NT, preferred_element_type=F32)

        ones = jnp.ones((ATTN_ONES_ROWS, tk), BF16)
        s_next = scores(0)
        for hm in range(8):
            h = hm // 2
            s = s_next
            if hm + 1 < 8:
                s_next = scores(hm + 1)
            if masked:
                s = jnp.where(visible, s, MASK_VALUE)
            m_prev = m_ref[hm]
            m_new = jnp.maximum(m_prev, jnp.max(s, axis=0, keepdims=True))
            alpha = jnp.exp2(m_prev - m_new)
            e = jnp.exp2(s - m_new).astype(BF16)
            v_ones = jnp.concatenate(
                [jnp.concatenate([vt_ref[n, h] for n in range(vt_ref.shape[0])], axis=1), ones], axis=0)
            acc_ref[hm] = alpha * acc_ref[hm] + jnp.dot(v_ones, e, preferred_element_type=F32)
            m_ref[hm] = m_new

    needs_mask = jnp.bitwise_and(flag_tab[p], 2) != 0

    @pl.when(needs_mask)
    def _():
        step(True)

    @pl.when(jnp.logical_not(needs_mask))
    def _():
        step(False)

    @pl.when(jnp.bitwise_and(flag_tab[p], 1) != 0)
    def _():
        lp = lam_ref[...]
        lam = (jnp.exp(jnp.sum(lp[0:1] * lp[1:2], axis=1, keepdims=True))
               - jnp.exp(jnp.sum(lp[2:3] * lp[3:4], axis=1, keepdims=True)) + lam_init)
        for h in range(4):
            a1 = acc_ref[2 * h]
            a2 = acc_ref[2 * h + 1]
            ot = (a1[:LANES] / a1[LANES:LANES + 1] - lam * (a2[:LANES] / a2[LANES:LANES + 1]))
            ot = ot * lax.rsqrt(jnp.mean(ot * ot, axis=0, keepdims=True) + NORM_EPS)
            o = jnp.transpose(ot) * bn_ref[...] * (1.0 - lam_init)
            o_ref[0, :, h * LANES:(h + 1) * LANES] = o.astype(BF16)


def _diff_attention(q, k, v, lam_params, bnorm_row, past, lam_init, tq, tk, v_transposed):
    b, lq, width = q.shape
    lk_true = k.shape[1]
    nk = -(-lk_true // tk)
    if nk * tk != lk_true:
        assert not v_transposed
        pad = ((0, 0), (0, nk * tk - lk_true), (0, 0))
        k = jnp.pad(k, pad)
        v = jnp.pad(v, pad)
    nq = lq // tq
    i_list, j_list, flag_list = [], [], []
    for i in range(nq):
        first_pos = past + i * tq
        last_pos = first_pos + tq - 1
        j_max = min(((last_pos // CHUNK + 1) * CHUNK - 1) // tk, nk - 1)
        for j in range(j_max + 1):
            key_end = (j + 1) * tk
            all_visible = (key_end - 1) // CHUNK <= first_pos // CHUNK and key_end <= lk_true
            i_list.append(i)
            j_list.append(j)
            flag_list.append((1 if j == j_max else 0) + (0 if all_visible else 2))
    tabs = [jnp.asarray(np.asarray(t, np.int32)) for t in (i_list, j_list, flag_list)]
    if v_transposed:
        body = _attn_t_kernel
        slab = v.shape[3]
        assert tk % slab == 0 and lk_true % tk == 0
        v_spec = pl.BlockSpec((tk // slab, 4, LANES, slab),
                              lambda bi, p, it, jt, lt: (bi * nk + jt[p], 0, 0, 0))
        scratch = [pltpu.VMEM((8, 1, tq), F32), pltpu.VMEM((8, LANES + ATTN_ONES_ROWS, tq), F32)]
    else:
        body = _attn_kernel
        v_spec = pl.BlockSpec((1, tk, width), lambda bi, p, it, jt, lt: (bi, jt[p], 0))
        scratch = [pltpu.VMEM((8, tq, LANES), F32), pltpu.VMEM((8, tq, 2 * LANES), F32)]
    grid_spec = pltpu.PrefetchScalarGridSpec(
        num_scalar_prefetch=3,
        grid=(b, len(i_list)),
        in_specs=[pl.BlockSpec((1, tq, width), lambda bi, p, it, jt, lt: (bi, it[p], 0)),
                  pl.BlockSpec((1, tk, width), lambda bi, p, it, jt, lt: (bi, jt[p], 0)),
                  v_spec,
                  pl.BlockSpec((4, HEAD), lambda bi, p, it, jt, lt: (0, 0)),
                  pl.BlockSpec((1, LANES), lambda bi, p, it, jt, lt: (0, 0))],
        out_specs=pl.BlockSpec((1, tq, width), lambda bi, p, it, jt, lt: (bi, it[p], 0)),
        scratch_shapes=scratch,
    )
    return pl.pallas_call(
        functools.partial(body, tq=tq, tk=tk, past=past, lk_true=lk_true, lam_init=lam_init),
        grid_spec=grid_spec,
        out_shape=jax.ShapeDtypeStruct((b, lq, width), BF16),
        compiler_params=pltpu.CompilerParams(dimension_semantics=("parallel", "arbitrary"),
                                             vmem_limit_bytes=VMEM_LIMIT),
        name="diff_attention",
    )(*tabs, q, k, v, lam_params, bnorm_row)


def _mlp_kernel(h_ref, ma_ref, mb_ref, mc_ref, wo_ref, nf_ref, w1_ref, w2_ref, np_ref, wg_ref,
                wp_ref, p_ref, nfin_ref, o_ref, *, tf, final_norm):
    mixed = (jnp.dot(ma_ref[...], wo_ref[0:256, :], preferred_element_type=F32)
             + jnp.dot(mb_ref[...], wo_ref[256:768, :], preferred_element_type=F32)
             + jnp.dot(mc_ref[...], wo_ref[768:1024, :], preferred_element_type=F32))
    h1 = h_ref[...] + mixed
    xn = _rmsnorm_rows(h1, nf_ref[...]).astype(BF16)
    h2 = h1
    for f0 in range(0, w1_ref.shape[1], tf):
        u = jnp.maximum(jnp.dot(xn, w1_ref[:, f0:f0 + tf], preferred_element_type=F32), 0.0)
        h2 = h2 + jnp.dot((u * u).astype(BF16), w2_ref[f0:f0 + tf, :], preferred_element_type=F32)
    gate = jax.nn.sigmoid(jnp.dot(_rmsnorm_rows(h2, np_ref[...]).astype(BF16), wg_ref[...],
                                  preferred_element_type=F32))
    h3 = h2 + gate * jnp.dot(p_ref[...].astype(BF16), wp_ref[...], preferred_element_type=F32)
    if final_norm:
        h3 = _rmsnorm_rows(h3, nfin_ref[...])
    o_ref[...] = h3


def _mix_mlp(h2d, mix_a, mix_b, mix_c, w_out, norm_ffn, w_ff1, w_ff2, norm_ple, w_gate, w_proj,
             p2d, norm_final, final_norm, tm, tf):
    t, d = h2d.shape
    dff = w_ff1.shape[1]
    row = lambda i: (i, 0)
    fixed = lambda i: (0, 0)
    resident = lambda shape: pl.BlockSpec(shape, fixed, pipeline_mode=pl.Buffered(1))
    return pl.pallas_call(
        functools.partial(_mlp_kernel, tf=tf, final_norm=final_norm),
        grid=(t // tm,),
        in_specs=[pl.BlockSpec((tm, d), row), pl.BlockSpec((tm, 256), row),
                  pl.BlockSpec((tm, 512), row), pl.BlockSpec((tm, 256), row),
                  resident((d, d)), resident((1, d)), resident((d, dff)), resident((dff, d)),
                  resident((1, d)), resident((d, d)), resident((p2d.shape[1], d)),
                  pl.BlockSpec((tm, p2d.shape[1]), row), resident((1, d))],
        out_specs=pl.BlockSpec((tm, d), row),
        out_shape=jax.ShapeDtypeStruct((t, d), F32),
        compiler_params=pltpu.CompilerParams(dimension_semantics=("parallel",),
                                             vmem_limit_bytes=VMEM_LIMIT),
        name="mix_mlp",
    )(h2d, mix_a, mix_b, mix_c, w_out, norm_ffn, w_ff1, w_ff2, norm_ple, w_gate, w_proj, p2d,
      norm_final)


def _rope_tables(past, seq_len):
    half = HEAD // 2
    inv = ROPE_THETA ** (-2.0 * jnp.arange(half, dtype=F32) / HEAD)
    pos = past + jnp.arange(seq_len, dtype=jnp.int32)
    ang = pos.astype(F32)[:, None] * inv[None, :]
    cos, sin, zero = jnp.cos(ang), jnp.sin(ang), jnp.zeros_like(ang)
    rep = LANES // HEAD
    cos_t = jnp.tile(jnp.concatenate([cos, cos], axis=1), (1, rep))
    sa_t = jnp.tile(jnp.concatenate([-sin, zero], axis=1), (1, rep))
    sb_t = jnp.tile(jnp.concatenate([zero, sin], axis=1), (1, rep))
    return cos_t, sa_t, sb_t


def _pad_lanes(v, width=LANES):
    return jnp.pad(v, (0, width - v.shape[0]))[None, :]


def _prep_layer(i, norm_mix, w_in, a_conv_w, a_A_log, a_dt_bias, a_norm,
                b_lam_q1, b_lam_k1, b_lam_q2, b_lam_k2, b_norm,
                c_mu, c_w0, c_w_up, c_a0, c_a_up, c_g_up, c_k_k, c_k_a, c_r_k, c_ln_w, c_ln_b,
                w_out, norm_ffn, w_ff1, w_ff2, norm_ple, w_ple_gate, w_ple_proj):
    w = w_in[i]
    d = w.shape[0]
    w_perm = jnp.concatenate(
        [w[:, 0:1024], w[:, 1032:3592], w[:, 1024:1032],
         jnp.zeros((d, _PROJ_WIDTH - 3592), w.dtype)], axis=1).astype(BF16)
    zeros_rank = jnp.zeros_like(c_w_up[i])
    rwkv = (c_mu[i][None, :], c_w0[i][None, :],
            jnp.concatenate([c_w_up[i], zeros_rank], axis=0).astype(BF16), c_a0[i][None, :],
            jnp.concatenate([jnp.zeros_like(c_a_up[i]), c_a_up[i]], axis=0).astype(BF16),
            c_g_up[i].astype(BF16), c_k_k[i][None, :], c_k_a[i][None, :],
            c_r_k[i].reshape(1, -1), c_ln_w[i][None, :], c_ln_b[i][None, :])
    return dict(
        norm_mix=norm_mix[i][None, :], w_in=w_perm, conv_w=a_conv_w[i],
        alog=_pad_lanes(a_A_log[i]), dtb=_pad_lanes(a_dt_bias[i]),
        anorm=jnp.tile(a_norm[i], 4)[None, :],
        lam=jnp.stack([b_lam_q1[i], b_lam_k1[i], b_lam_q2[i], b_lam_k2[i]], axis=0),
        bnorm=b_norm[i][None, :], rwkv=rwkv,
        w_out=w_out[i].astype(BF16), norm_ffn=norm_ffn[i][None, :],
        w_ff1=w_ff1[i].astype(BF16), w_ff2=w_ff2[i].astype(BF16),
        norm_ple=norm_ple[i][None, :], w_gate=w_ple_gate[i].astype(BF16),
        w_proj=w_ple_proj[i].astype(BF16))


def _pick_tile(n, target):
    t = min(n, target)
    while n % t:
        t //= 2
    return t


def _trunk(x, p, cache_k, cache_v, conv_buf, delta_s, shift_prev, wkv_s, layers, norm_final):
    b, seq, d = x.shape
    depth = len(layers)
    past = cache_k.shape[2]
    t = b * seq
    tm = _pick_tile(t, PROJ_TM)
    rope_tabs = _rope_tables(past, seq)
    h = x.reshape(t, d)
    states = []
    kv_stacked = None
    attn_tq = _pick_tile(seq, ATTN_TQ)
    attn_tk = ATTN_TK if past + seq >= 8 * ATTN_TK else ATTN_TK // 2
    v_transposed = past == 0 and attn_tq % LANES == 0 and seq % attn_tk == 0 and tm % LANES == 0
    for i, lp in enumerate(layers):
        a_qkv, a_z, a_ab, q_b, k_f, v_f, k_b, v_b, c_raw = _input_projection(
            h, lp["norm_mix"], lp["w_in"], rope_tabs, seq, tm, i, depth, kv_stacked, v_transposed)
        kv_stacked = (k_f, v_f)
        mix_a, conv_n, delta_n = _gated_delta(
            a_qkv.reshape(b, seq, -1), a_z.reshape(b, seq, -1), a_ab.reshape(b, seq, -1),
            conv_buf[i], delta_s[i], lp["conv_w"], lp["alog"], lp["dtb"], lp["anorm"])
        k_all = k_b.reshape(b, seq, -1)
        v_all = v_b if v_transposed else v_b.reshape(b, seq, -1)
        if past:
            k_all = jnp.concatenate([cache_k[i].reshape(b, past, -1).astype(BF16), k_all], axis=1)
            v_all = jnp.concatenate([cache_v[i].reshape(b, past, -1).astype(BF16), v_all], axis=1)
        lam_init = 0.8 - 0.6 * math.exp(-0.3 * i)
        mix_b = _diff_attention(q_b.reshape(b, seq, -1), k_all, v_all, lp["lam"], lp["bnorm"],
                                past, lam_init, attn_tq, attn_tk, v_transposed)
        mix_c, shift_n, wkv_n = _rwkv7(c_raw.reshape(b, seq, -1), shift_prev[i][:, None, :],
                                       wkv_s[i], lp["rwkv"])
        h = _mix_mlp(h, mix_a.reshape(t, -1), mix_b.reshape(t, -1), mix_c.reshape(t, -1),
                     lp["w_out"], lp["norm_ffn"], lp["w_ff1"], lp["w_ff2"], lp["norm_ple"],
                     lp["w_gate"], lp["w_proj"], p[i].reshape(t, -1), norm_final[None, :],
                     i == depth - 1, _pick_tile(t, MLP_TM), MLP_TF)
        states.append((conv_n, delta_n, shift_n[:, 0, :], wkv_n))
    conv_all, delta_all, shift_all, wkv_all = (
        jnp.stack([st[j] for st in states], axis=0) for j in range(4))
    k_all_layers, v_all_layers = (a.reshape(depth, b, seq, 4, LANES) for a in kv_stacked)
    return h.reshape(b, seq, d), [conv_all, delta_all, k_all_layers, v_all_layers, shift_all, wkv_all]


def kernel(x_prompt, x_sample, cache_b_k, cache_b_v, state_a_conv, state_a_delta, state_c_shift, state_c_wkv, p_prompt, p_sample, norm_mix, w_in, a_conv_w, a_A_log, a_dt_bias, a_norm, b_lam_q1, b_lam_k1, b_lam_q2, b_lam_k2, b_norm, c_mu, c_w0, c_w_up, c_a0, c_a_up, c_g_up, c_k_k, c_k_a, c_r_k, c_ln_w, c_ln_b, w_out, norm_ffn, w_ff1, w_ff2, norm_ple, w_ple_gate, w_ple_proj, norm_final):
    depth = w_in.shape[0]
    layers = [_prep_layer(i, norm_mix, w_in, a_conv_w, a_A_log, a_dt_bias, a_norm,
                          b_lam_q1, b_lam_k1, b_lam_q2, b_lam_k2, b_norm,
                          c_mu, c_w0, c_w_up, c_a0, c_a_up, c_g_up, c_k_k, c_k_a, c_r_k, c_ln_w, c_ln_b,
                          w_out, norm_ffn, w_ff1, w_ff2, norm_ple, w_ple_gate, w_ple_proj)
              for i in range(depth)]
    bp = x_prompt.shape[0]
    dt = x_prompt.dtype
    zeros = lambda ref: jnp.zeros((depth, bp) + ref.shape[2:], dt)
    empty_k = jnp.zeros((depth, bp, 0) + cache_b_k.shape[3:], dt)
    empty_v = jnp.zeros((depth, bp, 0) + cache_b_v.shape[3:], dt)
    y_prompt, st_p = _trunk(x_prompt, p_prompt, empty_k, empty_v, zeros(state_a_conv),
                            zeros(state_a_delta), zeros(state_c_shift), zeros(state_c_wkv),
                            layers, norm_final)
    y_sample, st_s = _trunk(x_sample, p_sample, cache_b_k, cache_b_v, state_a_conv, state_a_delta,
                            state_c_shift, state_c_wkv, layers, norm_final)
    return (y_prompt, y_sample, *st_p, *st_s)
```

```python
import functools
import math

import numpy as np
import jax
import jax.numpy as jnp
from jax import lax
from jax.experimental import pallas as pl
from jax.experimental.pallas import tpu as pltpu

F32 = jnp.float32
BF16 = jnp.bfloat16

CHUNK = 64
ROPE_THETA = 10000.0
NORM_EPS = 1e-6
L2_EPS = 1e-6
C_LN_EPS = 64e-5
RWKV_DECAY_OFFSET = 0.5
MASK_VALUE = float(np.finfo(np.float32).min)

LANES = 128
HEAD = 64
VMEM_LIMIT = 56 * 1024 * 1024
PROJ_TM = 512
MLP_TM = 512
MLP_TF = 1024
CHUNKS_PER_STEP = 8
ATTN_TQ = 512
ATTN_TK = 1024
ATTN_CACHE_TK = 512
ATTN_Q_SCALE = (64 ** -0.5) * math.log2(math.e)


def _iota(shape, axis):
    return lax.broadcasted_iota(jnp.int32, shape, axis)


_NN = (((1,), (0,)), ((), ()))
_NT = (((1,), (1,)), ((), ()))
_TN = (((0,), (0,)), ((), ()))


def _mm(a, b, dims=_NN):
    return lax.dot_general(a.astype(BF16), b.astype(BF16), dims, preferred_element_type=F32)


def _split3(x):
    h = x.astype(BF16)
    r = x - h.astype(F32)
    m = r.astype(BF16)
    lo = (r - m.astype(F32)).astype(BF16)
    return h, m, lo


def _mm_exact_rhs(x, ones_bf16, dims=_NN):
    h, m, lo = _split3(x)
    d = functools.partial(lax.dot_general, dimension_numbers=dims, preferred_element_type=F32)
    return d(h, ones_bf16) + d(m, ones_bf16) + d(lo, ones_bf16)


def _mm_exact_lhs(ones_bf16, x, dims=_NN):
    h, m, lo = _split3(x)
    d = functools.partial(lax.dot_general, dimension_numbers=dims, preferred_element_type=F32)
    return d(ones_bf16, h) + d(ones_bf16, m) + d(ones_bf16, lo)


def _selector_matrices(tl, c):
    tok = np.arange(tl)
    same = (tok[:, None] // c) == (tok[None, :] // c)
    lane = np.arange(4 * HEAD)
    src = np.arange(LANES)
    mats = dict(
        seg=(lane[:, None] // HEAD) == (lane[None, :] // HEAD),
        tril=same & (tok[:, None] >= tok[None, :]),
        triu=same & (tok[:, None] <= tok[None, :]),
        last=tok[None, :] == (tok[:, None] // c) * c + c - 1,
        expg=src[:, None] == lane[None, :] // HEAD,
        expb=src[:, None] == lane[None, :] // HEAD + 4)
    return {name: jnp.asarray(m.astype(np.float32), BF16) for name, m in mats.items()}


def _chunking(seq):
    c = min(CHUNK, seq)
    assert c & (c - 1) == 0 and seq % c == 0, seq
    g = min(CHUNKS_PER_STEP, seq // c)
    while (seq // c) % g:
        g -= 1
    return c, g


def _rmsnorm_rows(x, g):
    return x * lax.rsqrt(jnp.mean(x * x, axis=-1, keepdims=True) + NORM_EPS) * g


def _unit_lower_inverses(a_list, c):
    row = _iota((c, c), 0)
    col = _iota((c, c), 1)

    def same_block(shift):
        return jnp.right_shift(row, shift) == jnp.right_shift(col, shift)

    eye = (row == col).astype(F32)
    leaf = same_block(3)
    ns = [jnp.where(leaf, -a, 0.0) for a in a_list]
    ts = [eye + n for n in ns]
    n2s = [_mm(n, n) for n in ns]
    ts = [t + _mm(t, n2) for t, n2 in zip(ts, n2s)]
    n4s = [_mm(n2, n2) for n2 in n2s]
    ts = [t + _mm(t, n4) for t, n4 in zip(ts, n4s)]
    shift = 3
    while (1 << shift) < c:
        off_mask = same_block(shift + 1) & jnp.logical_not(same_block(shift))
        tos = [_mm(t, jnp.where(off_mask, a, 0.0)) for t, a in zip(ts, a_list)]
        ts = [t - _mm(to, t) for t, to in zip(ts, tos)]
        shift += 1
    return ts


_PROJ_A_QKV = 0
_PROJ_A_Z = 768
_PROJ_B_Q = 1024
_PROJ_B_K = 1536
_PROJ_B_V = 2048
_PROJ_C = 2560
_PROJ_A_AB = 3584
_PROJ_WIDTH = 3712


def _proj_kernel(*refs, n_alias):
    x_ref, g_ref, w_ref, cos_ref, sa_ref, sb_ref = refs[:6]
    aqkv_ref, az_ref, aab_ref, q_ref, k_ref, v_ref, kb_ref, vb_ref, c_ref = refs[6 + n_alias:]
    xn = _rmsnorm_rows(x_ref[...], g_ref[...]).astype(BF16)

    def proj(c0, c1):
        return jnp.dot(xn, w_ref[:, c0:c1], preferred_element_type=F32)

    a_all = proj(_PROJ_A_QKV, _PROJ_B_Q)
    aqkv_ref[...] = a_all[:, :_PROJ_A_Z]
    az_ref[...] = a_all[:, _PROJ_A_Z:]
    cos = cos_ref[...]
    sa = sa_ref[...]
    sb = sb_ref[...]

    def rope(x):
        return x * cos + pltpu.roll(x, LANES - 32, 1) * sa + pltpu.roll(x, 32, 1) * sb

    q_all = proj(_PROJ_B_Q, _PROJ_B_K)
    k_all = proj(_PROJ_B_K, _PROJ_B_V)
    for h in range(4):
        lo, hi = h * LANES, (h + 1) * LANES
        q_ref[:, lo:hi] = (rope(q_all[:, lo:hi]) * ATTN_Q_SCALE).astype(BF16)
        k = rope(k_all[:, lo:hi])
        k_ref[:, h, :] = k
        kb_ref[:, lo:hi] = k.astype(BF16)
    v = proj(_PROJ_B_V, _PROJ_C)
    for h in range(4):
        v_ref[:, h, :] = v[:, h * LANES:(h + 1) * LANES]
    vb_ref[...] = v.astype(BF16)
    c_all = proj(_PROJ_C, _PROJ_WIDTH)
    c_ref[...] = c_all[:, :_PROJ_A_AB - _PROJ_C]
    aab_ref[...] = c_all[:, _PROJ_A_AB - _PROJ_C:]


def _input_projection(x2d, g_row, w_bf16, rope_tabs, seq_len, tm, layer, depth, kv_stacked):
    t, d = x2d.shape
    nt = t // tm
    cos_t, sa_t, sb_t = rope_tabs
    if tm >= seq_len:
        reps = tm // seq_len
        cos_t, sa_t, sb_t = (jnp.tile(a, (reps, 1)) for a in (cos_t, sa_t, sb_t))
        tab_map = lambda i: (0, 0)
    else:
        per_seq = seq_len // tm
        tab_map = lambda i: (i % per_seq, 0)
    row = lambda i: (i, 0)
    fixed = lambda i: (0, 0)
    tab_spec = pl.BlockSpec((tm, LANES), tab_map)
    widths = [(768, F32), (256, F32), (128, F32), (512, BF16), (512, F32), (512, F32),
              (512, BF16), (512, BF16), (1024, F32)]
    stacked = (4, 5)
    out_specs = [pl.BlockSpec((None, tm, 4, LANES), lambda i: (layer, i, 0, 0)) if n in stacked
                 else pl.BlockSpec((tm, w), row) for n, (w, _) in enumerate(widths)]
    out_shape = [jax.ShapeDtypeStruct((depth, t, 4, LANES) if n in stacked else (t, w), dt)
                 for n, (w, dt) in enumerate(widths)]
    in_specs = [pl.BlockSpec((tm, d), row), pl.BlockSpec((1, d), fixed),
                pl.BlockSpec((d, _PROJ_WIDTH), fixed), tab_spec, tab_spec, tab_spec]
    operands = [x2d, g_row, w_bf16, cos_t, sa_t, sb_t]
    aliases = {}
    if kv_stacked is not None:
        for out_idx, arr in zip(stacked, kv_stacked):
            aliases[len(operands)] = out_idx
            in_specs.append(pl.BlockSpec(memory_space=pl.ANY))
            operands.append(arr)
    return pl.pallas_call(
        functools.partial(_proj_kernel, n_alias=len(aliases)),
        grid=(nt,),
        in_specs=in_specs,
        out_specs=out_specs,
        out_shape=out_shape,
        input_output_aliases=aliases,
        compiler_params=pltpu.CompilerParams(dimension_semantics=("parallel",),
                                             vmem_limit_bytes=VMEM_LIMIT),
        name="input_projection",
    )(*operands)


def _delta_kernel(qkv_ref, z_ref, ab_ref, conv0_ref, s0_ref, convw_ref, alog_ref, dtb_ref, anorm_ref,
                  seg_ref, tril_ref, triu_ref, last_ref, expg_ref, expb_ref,
                  mix_ref, convn_ref, sn_ref, xp_ref, s_ref, o_ref, *, c, g, nl):
    l = pl.program_id(1)
    tl = c * g

    @pl.when(l == 0)
    def _():
        xp_ref[5:8, :] = conv0_ref[0]
        s_ref[...] = s0_ref[0]

    x = qkv_ref[0]
    xp_ref[8:8 + tl, :] = x
    w = convw_ref[...]
    y = xp_ref[5:5 + tl, :] * w[0:1]
    y = y + xp_ref[6:6 + tl, :] * w[1:2]
    y = y + xp_ref[7:7 + tl, :] * w[2:3]
    y = y + x * w[3:4]
    tail = xp_ref[5 + tl:8 + tl, :]
    xp_ref[5:8, :] = tail

    @pl.when(l == nl - 1)
    def _():
        convn_ref[0] = tail

    act = y * jax.nn.sigmoid(y)
    seg = seg_ref[...]

    def l2n(t):
        return t * lax.rsqrt(_mm_exact_rhs(t * t, seg) + L2_EPS)

    q = l2n(act[:, 0:256]) * (HEAD ** -0.5)
    k = l2n(act[:, 256:512])
    v = act[:, 512:768]

    ab = ab_ref[0]
    gl = -jnp.exp(alog_ref[...]) * jax.nn.softplus(ab + dtb_ref[...])
    beta_w = _mm_exact_rhs(jax.nn.sigmoid(ab), expb_ref[...])
    g_w = _mm_exact_lhs(tril_ref[...], _mm_exact_rhs(gl, expg_ref[...]))
    g_rows = _mm_exact_rhs(gl, triu_ref[...], _TN)
    g_last_w = _mm_exact_lhs(last_ref[...], g_w)
    eg_w = jnp.exp(g_w)
    v_beta = v * beta_w
    k_beta_eg = k * (beta_w * eg_w)
    q_dec = q * eg_w
    k_tail_w = k * jnp.exp(g_last_w - g_w)
    decay_end_w = jnp.exp(g_last_w)

    row = _iota((c, c), 0)
    col = _iota((c, c), 1)
    lower = row >= col
    strict = row > col

    probs = [(slice(ci * c, (ci + 1) * c), h) for ci in range(g) for h in range(4)]
    heads = lambda h: slice(h * HEAD, (h + 1) * HEAD)
    decay = [jnp.where(lower, jnp.exp(jnp.where(lower, g_w[rows, heads(h)][:, :c] - g_rows[h:h + 1, rows], 0.0)),
                       0.0) for rows, h in probs]
    kh = [k[rows, heads(h)] for rows, h in probs]
    kq = [_mm(jnp.concatenate([kk_, q[rows, heads(h)]], axis=0), kk_, _NT)
          for kk_, (rows, h) in zip(kh, probs)]
    t_inv = _unit_lower_inverses(
        [jnp.where(strict, beta_w[rows, heads(h)][:, :c] * x[:c] * d, 0.0)
         for x, d, (rows, h) in zip(kq, decay, probs)], c)
    sol = [_mm(t, jnp.concatenate([v_beta[rows, heads(h)], k_beta_eg[rows, heads(h)]], axis=1))
           for t, (rows, h) in zip(t_inv, probs)]
    qk = [jnp.where(lower, x[c:] * d, 0.0) for x, d in zip(kq, decay)]
    kt_sol = [_mm(k_tail_w[rows, heads(h)], so, _TN) for so, (rows, h) in zip(sol, probs)]
    qk_sol = [_mm(x, so) for x, so in zip(qk, sol)]
    q_eff = [q_dec[rows, heads(h)] - x[:, HEAD:] for x, (rows, h) in zip(qk_sol, probs)]
    decay_end = [decay_end_w[rows.stop - 1:rows.stop, heads(h)] for rows, h in probs]

    states = [s_ref[h] for h in range(4)]
    for ci in range(g):
        rows = slice(ci * c, (ci + 1) * c)
        idx = [ci * 4 + h for h in range(4)]
        o_new = [_mm(q_eff[n], states[h]) + qk_sol[n][:, :HEAD] for h, n in enumerate(idx)]
        states = [states[h] * decay_end[n] + kt_sol[n][:, :HEAD] - _mm(kt_sol[n][:, HEAD:], states[h])
                  for h, n in enumerate(idx)]
        for h in range(4):
            o_ref[rows, heads(h)] = o_new[h]
    for h in range(4):
        s_ref[h] = states[h]

    o = o_ref[...]
    ms = _mm_exact_rhs(o * o, seg) * (1.0 / HEAD)
    z = z_ref[0]
    ao = (o * lax.rsqrt(ms + NORM_EPS) * anorm_ref[...]) * (z * jax.nn.sigmoid(z))
    mix_ref[0] = ao.astype(BF16)

    @pl.when(l == nl - 1)
    def _():
        sn_ref[0] = s_ref[...]


def _gated_delta(a_qkv, a_z, a_ab, conv0, s0, conv_w, alog_row, dtb_row, anorm_row):
    b, seq, _ = a_qkv.shape
    c, g = _chunking(seq)
    tl = c * g
    nl = seq // tl
    tile = lambda i, l: (i, l, 0)
    per_b3 = lambda i, l: (i, 0, 0)
    per_b4 = lambda i, l: (i, 0, 0, 0)
    fixed = lambda i, l: (0, 0)
    sel = _selector_matrices(tl, c)
    consts = [sel[name] for name in ("seg", "tril", "triu", "last", "expg", "expb")]
    return pl.pallas_call(
        functools.partial(_delta_kernel, c=c, g=g, nl=nl),
        grid=(b, nl),
        in_specs=[pl.BlockSpec((1, tl, 768), tile), pl.BlockSpec((1, tl, 256), tile),
                  pl.BlockSpec((1, tl, LANES), tile), pl.BlockSpec((1, 3, 768), per_b3),
                  pl.BlockSpec((1, 4, HEAD, HEAD), per_b4), pl.BlockSpec((4, 768), fixed),
                  pl.BlockSpec((1, LANES), fixed), pl.BlockSpec((1, LANES), fixed),
                  pl.BlockSpec((1, 256), fixed)] + [pl.BlockSpec(m.shape, fixed) for m in consts],
        out_specs=[pl.BlockSpec((1, tl, 256), tile), pl.BlockSpec((1, 3, 768), per_b3),
                   pl.BlockSpec((1, 4, HEAD, HEAD), per_b4)],
        out_shape=[jax.ShapeDtypeStruct((b, seq, 256), BF16),
                   jax.ShapeDtypeStruct((b, 3, 768), F32),
                   jax.ShapeDtypeStruct((b, 4, HEAD, HEAD), F32)],
        scratch_shapes=[pltpu.VMEM((tl + 8, 768), F32), pltpu.VMEM((4, HEAD, HEAD), F32),
                        pltpu.VMEM((tl, 256), F32)],
        compiler_params=pltpu.CompilerParams(dimension_semantics=("parallel", "arbitrary"),
                                             vmem_limit_bytes=VMEM_LIMIT),
        name="gated_delta",
    )(a_qkv, a_z, a_ab, conv0, s0, conv_w, alog_row, dtb_row, anorm_row, *consts)


def _rwkv_kernel(c_ref, shift0_ref, s0_ref, mu_ref, w0_ref, wup_ref, a0_ref, aup_ref, gup_ref,
                 kk_ref, ka_ref, rk_ref, lnw_ref, lnb_ref, seg_ref, tril_ref,
                 mix_ref, shiftn_ref, sn_ref, xp_ref, s_ref, y_ref, *, c, g, nl):
    l = pl.program_id(1)
    tl = c * g

    @pl.when(l == 0)
    def _():
        xp_ref[7:8, :] = shift0_ref[0]
        s_ref[...] = s0_ref[0]

    raw = c_ref[0]
    xp_ref[8:8 + tl, :] = raw
    prev = xp_ref[7:7 + tl, :]
    last = raw[tl - 1:tl, :]
    xp_ref[7:8, :] = last

    @pl.when(l == nl - 1)
    def _():
        shiftn_ref[0] = last

    x = raw + (prev - raw) * mu_ref[...]
    cr, ck, cv = x[:, 0:256], x[:, 256:512], x[:, 512:768]
    c_wa = x[:, 768:896]
    c_g = x[:, 896:1024]
    w_log = -jnp.exp(-jax.nn.softplus(-(w0_ref[...] + _mm(jnp.tanh(c_wa), wup_ref[...])))
                     - RWKV_DECAY_OFFSET)
    ca = jax.nn.sigmoid(a0_ref[...] + _mm(c_wa, aup_ref[...]))
    cg = _mm(jax.nn.sigmoid(c_g), gup_ref[...])
    seg = seg_ref[...]
    kkv = ck * kk_ref[...]
    kk = kkv * lax.rsqrt(_mm_exact_rhs(kkv * kkv, seg) + L2_EPS)
    ck = ck * (1.0 + (ca - 1.0) * ka_ref[...])

    g_cum = _mm_exact_lhs(tril_ref[...], w_log)
    e_pos = jnp.exp(g_cum)
    e_neg = jnp.exp(-g_cum)
    a_t = -kk * jnp.exp(g_cum - w_log)
    b_t = (kk * ca) * e_neg
    k_t = ck * e_neg
    r_t = cr * e_pos

    row = _iota((c, c), 0)
    col = _iota((c, c), 1)
    lower = row >= col
    strict = row > col

    probs = [(slice(ci * c, (ci + 1) * c), h) for ci in range(g) for h in range(4)]
    heads = lambda h: slice(h * HEAD, (h + 1) * HEAD)
    bh = [b_t[rows, heads(h)] for rows, h in probs]
    kh = [k_t[rows, heads(h)] for rows, h in probs]
    vh = [cv[rows, heads(h)] for rows, h in probs]
    ar = [jnp.concatenate([a_t[rows, heads(h)], r_t[rows, heads(h)]], axis=0) for rows, h in probs]
    pb = [_mm(x, y_, _NT) for x, y_ in zip(ar, bh)]
    pk = [_mm(x, y_, _NT) for x, y_ in zip(ar, kh)]
    t_inv = _unit_lower_inverses([-jnp.where(strict, x[:c], 0.0) for x in pb], c)
    n_rb = [jnp.where(lower, x[c:], 0.0) for x in pb]
    mn = [_mm(jnp.concatenate([jnp.where(strict, x[:c], 0.0), jnp.where(lower, x[c:], 0.0)], axis=0), vv)
          for x, vv in zip(pk, vh)]
    bk = [jnp.concatenate([x, y_], axis=0) for x, y_ in zip(bh, kh)]
    g_end = [e_pos[rows.stop - 1:rows.stop, heads(h)] for rows, h in probs]
    ta = [_mm(t, jnp.concatenate([x[:c], m_[:c]], axis=1)) for t, x, m_ in zip(t_inv, ar, mn)]
    p_mat = [_mm(x[:, :HEAD], y_, _TN) for x, y_ in zip(ta, bh)]
    c_mat = [_mm(jnp.concatenate([x[:, HEAD:], vv], axis=0), y_, _TN) for x, vv, y_ in zip(ta, vh, bk)]
    nr = [_mm(x, y_) for x, y_ in zip(n_rb, ta)]
    r_eff = [x[c:] + y_[:, :HEAD] for x, y_ in zip(ar, nr)]
    y_off = [x[:, HEAD:] + m_[c:] for x, m_ in zip(nr, mn)]

    states = [s_ref[h] for h in range(4)]
    for ci in range(g):
        rows = slice(ci * c, (ci + 1) * c)
        idx = [ci * 4 + h for h in range(4)]
        y_new = [_mm(r_eff[n], states[h], _NT) + y_off[n] for h, n in enumerate(idx)]
        states = [(states[h] + _mm(states[h], p_mat[n]) + c_mat[n]) * g_end[n] for h, n in enumerate(idx)]
        for h in range(4):
            y_ref[rows, heads(h)] = y_new[h]
    for h in range(4):
        s_ref[h] = states[h]

    y = y_ref[...]
    mean = _mm_exact_rhs(y, seg) * (1.0 / HEAD)
    yc = y - mean
    var = _mm_exact_rhs(yc * yc, seg) * (1.0 / HEAD)
    cy = (yc * lax.rsqrt(var + C_LN_EPS)) * lnw_ref[...] + lnb_ref[...]
    bonus = _mm_exact_rhs(cr * ck * rk_ref[...], seg) * cv
    mix_ref[0] = ((cy + bonus) * cg).astype(BF16)

    @pl.when(l == nl - 1)
    def _():
        sn_ref[0] = s_ref[...]


def _rwkv7(c_raw, shift0, s0, params):
    b, seq, width = c_raw.shape
    c, g = _chunking(seq)
    tl = c * g
    nl = seq // tl
    tile = lambda i, l: (i, l, 0)
    per_b3 = lambda i, l: (i, 0, 0)
    per_b4 = lambda i, l: (i, 0, 0, 0)
    fixed = lambda i, l: (0, 0)
    sel = _selector_matrices(tl, c)
    params = tuple(params) + (sel["seg"], sel["tril"])
    param_specs = [pl.BlockSpec(p.shape, fixed) for p in params]
    return pl.pallas_call(
        functools.partial(_rwkv_kernel, c=c, g=g, nl=nl),
        grid=(b, nl),
        in_specs=[pl.BlockSpec((1, tl, width), tile), pl.BlockSpec((1, 1, width), per_b3),
                  pl.BlockSpec((1, 4, HEAD, HEAD), per_b4)] + param_specs,
        out_specs=[pl.BlockSpec((1, tl, 256), tile), pl.BlockSpec((1, 1, width), per_b3),
                   pl.BlockSpec((1, 4, HEAD, HEAD), per_b4)],
        out_shape=[jax.ShapeDtypeStruct((b, seq, 256), BF16),
                   jax.ShapeDtypeStruct((b, 1, width), F32),
                   jax.ShapeDtypeStruct((b, 4, HEAD, HEAD), F32)],
        scratch_shapes=[pltpu.VMEM((tl + 8, width), F32), pltpu.VMEM((4, HEAD, HEAD), F32),
                        pltpu.VMEM((tl, 256), F32)],
        compiler_params=pltpu.CompilerParams(dimension_semantics=("parallel", "arbitrary"),
                                             vmem_limit_bytes=VMEM_LIMIT),
        name="rwkv7",
    )(c_raw, shift0, s0, *params)


def _attn_kernel(i_tab, j_tab, flag_tab, q_ref, k_ref, v_ref, lam_ref, bn_ref, o_ref,
                 m_ref, acc_ref, *, tq, tk, past, lk_true, lam_init):
    p = pl.program_id(1)
    i = i_tab[p]
    j = j_tab[p]

    @pl.when(j == 0)
    def _():
        m_ref[...] = jnp.full(m_ref.shape, MASK_VALUE, F32)
        acc_ref[...] = jnp.zeros(acc_ref.shape, F32)

    nslab = tk // LANES

    def step(masked):
        if masked:
            q_pos = past + i * tq + _iota((tq, tk), 0)
            k_pos = j * tk + _iota((tq, tk), 1)
            visible = (jnp.right_shift(k_pos, 6) <= jnp.right_shift(q_pos, 6)) & (k_pos < lk_true)
        q = q_ref[0]
        k = k_ref[0]
        v = v_ref[0]

        def scores(hm):
            sl = slice(hm * HEAD, (hm + 1) * HEAD)
            return lax.dot_general(q[:, sl], k[:, sl], _NT, preferred_element_type=F32)

        ones = jnp.ones((tk, LANES), BF16)
        s_next = scores(0)
        for hm in range(8):
            h = hm // 2
            s = s_next
            if hm + 1 < 8:
                s_next = scores(hm + 1)
            if masked:
                s = jnp.where(visible, s, MASK_VALUE)
            slabs = [s[:, n * LANES:(n + 1) * LANES] for n in range(nslab)]
            m_cur = functools.reduce(jnp.maximum, slabs)
            m_prev = m_ref[hm]
            m_new = jnp.maximum(m_prev, jnp.max(m_cur, axis=1, keepdims=True))
            alpha = jnp.exp2(m_prev - m_new)
            e = jnp.concatenate([jnp.exp2(sb - m_new).astype(BF16) for sb in slabs], axis=1)
            v_ones = jnp.concatenate([v[:, h * LANES:(h + 1) * LANES], ones], axis=1)
            acc_ref[hm] = (jnp.concatenate([alpha, alpha], axis=1) * acc_ref[hm]
                           + jnp.dot(e, v_ones, preferred_element_type=F32))
            m_ref[hm] = m_new

    needs_mask = jnp.bitwise_and(flag_tab[p], 2) != 0

    @pl.when(needs_mask)
    def _():
        step(True)

    @pl.when(jnp.logical_not(needs_mask))
    def _():
        step(False)

    @pl.when(jnp.bitwise_and(flag_tab[p], 1) != 0)
    def _():
        lp = lam_ref[...]
        lam = (jnp.exp(jnp.sum(lp[0:1] * lp[1:2], axis=1, keepdims=True))
               - jnp.exp(jnp.sum(lp[2:3] * lp[3:4], axis=1, keepdims=True)) + lam_init)
        for h in range(4):
            a1 = acc_ref[2 * h]
            a2 = acc_ref[2 * h + 1]
            o = a1[:, :LANES] / a1[:, LANES:] - lam * (a2[:, :LANES] / a2[:, LANES:])
            o = _rmsnorm_rows(o, bn_ref[...]) * (1.0 - lam_init)
            o_ref[0, :, h * LANES:(h + 1) * LANES] = o.astype(BF16)


def _diff_attention(q, k, v, lam_params, bnorm_row, past, lam_init, tq, tk):
    b, lq, width = q.shape
    lk_true = k.shape[1]
    nk = -(-lk_true // tk)
    if nk * tk != lk_true:
        pad = ((0, 0), (0, nk * tk - lk_true), (0, 0))
        k = jnp.pad(k, pad)
        v = jnp.pad(v, pad)
    nq = lq // tq
    i_list, j_list, flag_list = [], [], []
    for i in range(nq):
        first_pos = past + i * tq
        last_pos = first_pos + tq - 1
        j_max = min(((last_pos // CHUNK + 1) * CHUNK - 1) // tk, nk - 1)
        for j in range(j_max + 1):
            key_end = (j + 1) * tk
            all_visible = (key_end - 1) // CHUNK <= first_pos // CHUNK and key_end <= lk_true
            i_list.append(i)
            j_list.append(j)
            flag_list.append((1 if j == j_max else 0) + (0 if all_visible else 2))
    tabs = [jnp.asarray(np.asarray(t, np.int32)) for t in (i_list, j_list, flag_list)]
    grid_spec = pltpu.PrefetchScalarGridSpec(
        num_scalar_prefetch=3,
        grid=(b, len(i_list)),
        in_specs=[pl.BlockSpec((1, tq, width), lambda bi, p, it, jt, lt: (bi, it[p], 0)),
                  pl.BlockSpec((1, tk, width), lambda bi, p, it, jt, lt: (bi, jt[p], 0)),
                  pl.BlockSpec((1, tk, width), lambda bi, p, it, jt, lt: (bi, jt[p], 0)),
                  pl.BlockSpec((4, HEAD), lambda bi, p, it, jt, lt: (0, 0)),
                  pl.BlockSpec((1, LANES), lambda bi, p, it, jt, lt: (0, 0))],
        out_specs=pl.BlockSpec((1, tq, width), lambda bi, p, it, jt, lt: (bi, it[p], 0)),
        scratch_shapes=[pltpu.VMEM((8, tq, LANES), F32), pltpu.VMEM((8, tq, 2 * LANES), F32)],
    )
    return pl.pallas_call(
        functools.partial(_attn_kernel, tq=tq, tk=tk, past=past, lk_true=lk_true, lam_init=lam_init),
        grid_spec=grid_spec,
        out_shape=jax.ShapeDtypeStruct((b, lq, width), BF16),
        compiler_params=pltpu.CompilerParams(dimension_semantics=("parallel", "arbitrary"),
                                             vmem_limit_bytes=VMEM_LIMIT),
        name="diff_attention",
    )(*tabs, q, k, v, lam_params, bnorm_row)


def _attn_cache_kernel(q_ref, ck_ref, cv_ref, kn_ref, vn_ref, lam_ref, bn_ref, o_ref, m_ref, acc_ref, *,
                       seq, tkc, nkc, past, lam_init):
    j = pl.program_id(1)

    @pl.when(j == 0)
    def _():
        m_ref[...] = jnp.full(m_ref.shape, MASK_VALUE, F32)
        acc_ref[...] = jnp.zeros(acc_ref.shape, F32)

    q = q_ref[0]

    def update(hm, k_hm, v_h, visible):
        sl = slice(hm * HEAD, (hm + 1) * HEAD)
        s = lax.dot_general(q[:, sl], k_hm, _NT, preferred_element_type=F32)
        if visible is not None:
            s = jnp.where(visible, s, MASK_VALUE)
        m_prev = m_ref[hm]
        m_new = jnp.maximum(m_prev, jnp.max(s, axis=1, keepdims=True))
        alpha = jnp.exp2(m_prev - m_new)
        e = jnp.exp2(s - m_new[:, 0:1]).astype(BF16)
        v_ones = jnp.concatenate([v_h, jnp.ones(v_h.shape, BF16)], axis=1)
        acc_ref[hm] = (jnp.concatenate([alpha, alpha], axis=1) * acc_ref[hm]
                       + jnp.dot(e, v_ones, preferred_element_type=F32))
        m_ref[hm] = m_new

    @pl.when(j < nkc)
    def _():
        for hm in range(8):
            h, half = hm // 2, hm % 2
            k_hm = ck_ref[0, :, h, half * HEAD:(half + 1) * HEAD].astype(BF16)
            update(hm, k_hm, cv_ref[0, :, h, :].astype(BF16), None)

    @pl.when(j == nkc)
    def _():
        q_pos = past + _iota((seq, seq), 0)
        k_pos = past + _iota((seq, seq), 1)
        visible = jnp.right_shift(k_pos, 6) <= jnp.right_shift(q_pos, 6)
        kn = kn_ref[0]
        vn = vn_ref[0]
        for hm in range(8):
            h = hm // 2
            update(hm, kn[:, hm * HEAD:(hm + 1) * HEAD], vn[:, h * LANES:(h + 1) * LANES], visible)
        lp = lam_ref[...]
        lam = (jnp.exp(jnp.sum(lp[0:1] * lp[1:2], axis=1, keepdims=True))
               - jnp.exp(jnp.sum(lp[2:3] * lp[3:4], axis=1, keepdims=True)) + lam_init)
        for h in range(4):
            a1 = acc_ref[2 * h]
            a2 = acc_ref[2 * h + 1]
            o = a1[:, :LANES] / a1[:, LANES:] - lam * (a2[:, :LANES] / a2[:, LANES:])
            o = _rmsnorm_rows(o, bn_ref[...]) * (1.0 - lam_init)
            o_ref[0, :, h * LANES:(h + 1) * LANES] = o.astype(BF16)


def _diff_attention_cached(q, cache_k, cache_v, k_new, v_new, lam_params, bnorm_row, lam_init, tkc):
    b, seq, width = q.shape
    past = cache_k.shape[1]
    nkc = past // tkc
    cache_map = lambda bi, j: (bi, jnp.minimum(j, nkc - 1), 0, 0)
    own = lambda bi, j: (bi, 0, 0)
    fixed = lambda bi, j: (0, 0)
    return pl.pallas_call(
        functools.partial(_attn_cache_kernel, seq=seq, tkc=tkc, nkc=nkc, past=past, lam_init=lam_init),
        grid=(b, nkc + 1),
        in_specs=[pl.BlockSpec((1, seq, width), own),
                  pl.BlockSpec((1, tkc, 4, LANES), cache_map), pl.BlockSpec((1, tkc, 4, LANES), cache_map),
                  pl.BlockSpec((1, seq, width), own), pl.BlockSpec((1, seq, width), own),
                  pl.BlockSpec((4, HEAD), fixed), pl.BlockSpec((1, LANES), fixed)],
        out_specs=pl.BlockSpec((1, seq, width), own),
        out_shape=jax.ShapeDtypeStruct((b, seq, width), BF16),
        scratch_shapes=[pltpu.VMEM((8, seq, LANES), F32), pltpu.VMEM((8, seq, 2 * LANES), F32)],
        compiler_params=pltpu.CompilerParams(dimension_semantics=("parallel", "arbitrary"),
                                             vmem_limit_bytes=VMEM_LIMIT),
        name="diff_attention_cached",
    )(q, cache_k, cache_v, k_new, v_new, lam_params, bnorm_row)


def _mlp_kernel(h_ref, ma_ref, mb_ref, mc_ref, wo_ref, nf_ref, w1_ref, w2_ref, np_ref, wg_ref,
                wp_ref, p_ref, nfin_ref, o_ref, *, tf, final_norm):
    mixed = (jnp.dot(ma_ref[...], wo_ref[0:256, :], preferred_element_type=F32)
             + jnp.dot(mb_ref[...], wo_ref[256:768, :], preferred_element_type=F32)
             + jnp.dot(mc_ref[...], wo_ref[768:1024, :], preferred_element_type=F32))
    h1 = h_ref[...] + mixed
    xn = _rmsnorm_rows(h1, nf_ref[...]).astype(BF16)
    h2 = h1
    for f0 in range(0, w1_ref.shape[1], tf):
        u = jnp.maximum(jnp.dot(xn, w1_ref[:, f0:f0 + tf], preferred_element_type=F32), 0.0)
        h2 = h2 + jnp.dot((u * u).astype(BF16), w2_ref[f0:f0 + tf, :], preferred_element_type=F32)
    gate = jax.nn.sigmoid(jnp.dot(_rmsnorm_rows(h2, np_ref[...]).astype(BF16), wg_ref[...],
                                  preferred_element_type=F32))
    h3 = h2 + gate * jnp.dot(p_ref[...].astype(BF16), wp_ref[...], preferred_element_type=F32)
    if final_norm:
        h3 = _rmsnorm_rows(h3, nfin_ref[...])
    o_ref[...] = h3


def _mix_mlp(h2d, mix_a, mix_b, mix_c, w_out, norm_ffn, w_ff1, w_ff2, norm_ple, w_gate, w_proj,
             p2d, norm_final, final_norm, tm, tf):
    t, d = h2d.shape
    dff = w_ff1.shape[1]
    row = lambda i: (i, 0)
    fixed = lambda i: (0, 0)
    resident = lambda shape: pl.BlockSpec(shape, fixed, pipeline_mode=pl.Buffered(1))
    return pl.pallas_call(
        functools.partial(_mlp_kernel, tf=tf, final_norm=final_norm),
        grid=(t // tm,),
        in_specs=[pl.BlockSpec((tm, d), row), pl.BlockSpec((tm, 256), row),
                  pl.BlockSpec((tm, 512), row), pl.BlockSpec((tm, 256), row),
                  resident((d, d)), resident((1, d)), resident((d, dff)), resident((dff, d)),
                  resident((1, d)), resident((d, d)), resident((p2d.shape[1], d)),
                  pl.BlockSpec((tm, p2d.shape[1]), row), resident((1, d))],
        out_specs=pl.BlockSpec((tm, d), row),
        out_shape=jax.ShapeDtypeStruct((t, d), F32),
        compiler_params=pltpu.CompilerParams(dimension_semantics=("parallel",),
                                             vmem_limit_bytes=VMEM_LIMIT),
        name="mix_mlp",
    )(h2d, mix_a, mix_b, mix_c, w_out, norm_ffn, w_ff1, w_ff2, norm_ple, w_gate, w_proj, p2d,
      norm_final)


def _rope_tables(past, seq_len):
    half = HEAD // 2
    inv = ROPE_THETA ** (-2.0 * jnp.arange(half, dtype=F32) / HEAD)
    pos = past + jnp.arange(seq_len, dtype=jnp.int32)
    ang = pos.astype(F32)[:, None] * inv[None, :]
    cos, sin, zero = jnp.cos(ang), jnp.sin(ang), jnp.zeros_like(ang)
    rep = LANES // HEAD
    cos_t = jnp.tile(jnp.concatenate([cos, cos], axis=1), (1, rep))
    sa_t = jnp.tile(jnp.concatenate([-sin, zero], axis=1), (1, rep))
    sb_t = jnp.tile(jnp.concatenate([zero, sin], axis=1), (1, rep))
    return cos_t, sa_t, sb_t


def _pad_lanes(v, width=LANES):
    return jnp.pad(v, (0, width - v.shape[0]))[None, :]


def _prep_layer(i, norm_mix, w_in, a_conv_w, a_A_log, a_dt_bias, a_norm,
                b_lam_q1, b_lam_k1, b_lam_q2, b_lam_k2, b_norm,
                c_mu, c_w0, c_w_up, c_a0, c_a_up, c_g_up, c_k_k, c_k_a, c_r_k, c_ln_w, c_ln_b,
                w_out, norm_ffn, w_ff1, w_ff2, norm_ple, w_ple_gate, w_ple_proj):
    w = w_in[i]
    d = w.shape[0]
    w_perm = jnp.concatenate(
        [w[:, 0:1024], w[:, 1032:3592], w[:, 1024:1032],
         jnp.zeros((d, _PROJ_WIDTH - 3592), w.dtype)], axis=1).astype(BF16)
    zeros_rank = jnp.zeros_like(c_w_up[i])
    rwkv = (c_mu[i][None, :], c_w0[i][None, :],
            jnp.concatenate([c_w_up[i], zeros_rank], axis=0).astype(BF16), c_a0[i][None, :],
            jnp.concatenate([jnp.zeros_like(c_a_up[i]), c_a_up[i]], axis=0).astype(BF16),
            c_g_up[i].astype(BF16), c_k_k[i][None, :], c_k_a[i][None, :],
            c_r_k[i].reshape(1, -1), c_ln_w[i][None, :], c_ln_b[i][None, :])
    return dict(
        norm_mix=norm_mix[i][None, :], w_in=w_perm, conv_w=a_conv_w[i],
        alog=_pad_lanes(a_A_log[i]), dtb=_pad_lanes(a_dt_bias[i]),
        anorm=jnp.tile(a_norm[i], 4)[None, :],
        lam=jnp.stack([b_lam_q1[i], b_lam_k1[i], b_lam_q2[i], b_lam_k2[i]], axis=0),
        bnorm=b_norm[i][None, :], rwkv=rwkv,
        w_out=w_out[i].astype(BF16), norm_ffn=norm_ffn[i][None, :],
        w_ff1=w_ff1[i].astype(BF16), w_ff2=w_ff2[i].astype(BF16),
        norm_ple=norm_ple[i][None, :], w_gate=w_ple_gate[i].astype(BF16),
        w_proj=w_ple_proj[i].astype(BF16))


def _pick_tile(n, target):
    t = min(n, target)
    while n % t:
        t //= 2
    return t


def _trunk(x, p, cache_k, cache_v, conv_buf, delta_s, shift_prev, wkv_s, layers, norm_final):
    b, seq, d = x.shape
    depth = len(layers)
    past = cache_k.shape[2]
    t = b * seq
    tm = _pick_tile(t, PROJ_TM)
    rope_tabs = _rope_tables(past, seq)
    h = x.reshape(t, d)
    states = []
    kv_stacked = None
    for i, lp in enumerate(layers):
        a_qkv, a_z, a_ab, q_b, k_f, v_f, k_b, v_b, c_raw = _input_projection(
            h, lp["norm_mix"], lp["w_in"], rope_tabs, seq, tm, i, depth, kv_stacked)
        kv_stacked = (k_f, v_f)
        mix_a, conv_n, delta_n = _gated_delta(
            a_qkv.reshape(b, seq, -1), a_z.reshape(b, seq, -1), a_ab.reshape(b, seq, -1),
            conv_buf[i], delta_s[i], lp["conv_w"], lp["alog"], lp["dtb"], lp["anorm"])
        k_all = k_b.reshape(b, seq, -1)
        v_all = v_b.reshape(b, seq, -1)
        lam_init = 0.8 - 0.6 * math.exp(-0.3 * i)
        cache_tile = _pick_tile(past, ATTN_CACHE_TK) if past else 0
        if past and seq <= CHUNK and cache_tile % LANES == 0:
            mix_b = _diff_attention_cached(q_b.reshape(b, seq, -1), cache_k[i], cache_v[i], k_all, v_all,
                                           lp["lam"], lp["bnorm"], lam_init, cache_tile)
        else:
            if past:
                k_all = jnp.concatenate([cache_k[i].reshape(b, past, -1).astype(BF16), k_all], axis=1)
                v_all = jnp.concatenate([cache_v[i].reshape(b, past, -1).astype(BF16), v_all], axis=1)
            mix_b = _diff_attention(q_b.reshape(b, seq, -1), k_all, v_all, lp["lam"], lp["bnorm"],
                                    past, lam_init, _pick_tile(seq, ATTN_TQ),
                                    ATTN_TK if past + seq >= 8 * ATTN_TK else ATTN_TK // 2)
        mix_c, shift_n, wkv_n = _rwkv7(c_raw.reshape(b, seq, -1), shift_prev[i][:, None, :],
                                       wkv_s[i], lp["rwkv"])
        h = _mix_mlp(h, mix_a.reshape(t, -1), mix_b.reshape(t, -1), mix_c.reshape(t, -1),
                     lp["w_out"], lp["norm_ffn"], lp["w_ff1"], lp["w_ff2"], lp["norm_ple"],
                     lp["w_gate"], lp["w_proj"], p[i].reshape(t, -1), norm_final[None, :],
                     i == depth - 1, _pick_tile(t, MLP_TM), MLP_TF)
        states.append((conv_n, delta_n, shift_n[:, 0, :], wkv_n))
    conv_all, delta_all, shift_all, wkv_all = (
        jnp.stack([st[j] for st in states], axis=0) for j in range(4))
    k_all_layers, v_all_layers = (a.reshape(depth, b, seq, 4, LANES) for a in kv_stacked)
    return h.reshape(b, seq, d), [conv_all, delta_all, k_all_layers, v_all_layers, shift_all, wkv_all]


def kernel(x_prompt, x_sample, cache_b_k, cache_b_v, state_a_conv, state_a_delta, state_c_shift, state_c_wkv, p_prompt, p_sample, norm_mix, w_in, a_conv_w, a_A_log, a_dt_bias, a_norm, b_lam_q1, b_lam_k1, b_lam_q2, b_lam_k2, b_norm, c_mu, c_w0, c_w_up, c_a0, c_a_up, c_g_up, c_k_k, c_k_a, c_r_k, c_ln_w, c_ln_b, w_out, norm_ffn, w_ff1, w_ff2, norm_ple, w_ple_gate, w_ple_proj, norm_final):
    depth = w_in.shape[0]
    layers = [_prep_layer(i, norm_mix, w_in, a_conv_w, a_A_log, a_dt_bias, a_norm,
                          b_lam_q1, b_lam_k1, b_lam_q2, b_lam_k2, b_norm,
                          c_mu, c_w0, c_w_up, c_a0, c_a_up, c_g_up, c_k_k, c_k_a, c_r_k, c_ln_w, c_ln_b,
                          w_out, norm_ffn, w_ff1, w_ff2, norm_ple, w_ple_gate, w_ple_proj)
              for i in range(depth)]
    bp = x_prompt.shape[0]
    dt = x_prompt.dtype
    zeros = lambda ref: jnp.zeros((depth, bp) + ref.shape[2:], dt)
    empty_k = jnp.zeros((depth, bp, 0) + cache_b_k.shape[3:], dt)
    empty_v = jnp.zeros((depth, bp, 0) + cache_b_v.shape[3:], dt)
    y_prompt, st_p = _trunk(x_prompt, p_prompt, empty_k, empty_v, zeros(state_a_conv),
                            zeros(state_a_delta), zeros(state_c_shift), zeros(state_c_wkv),
                            layers, norm_final)
    y_sample, st_s = _trunk(x_sample, p_sample, cache_b_k, cache_b_v, state_a_conv, state_a_delta,
                            state_c_shift, state_c_wkv, layers, norm_final)
    return (y_prompt, y_sample, *st_p, *st_s)
```

```python
import functools
import math

import numpy as np
import jax
import jax.numpy as jnp
from jax import lax
from jax.experimental import pallas as pl
from jax.experimental.pallas import tpu as pltpu

F32 = jnp.float32
BF16 = jnp.bfloat16

CHUNK = 64
ROPE_THETA = 10000.0
NORM_EPS = 1e-6
L2_EPS = 1e-6
C_LN_EPS = 64e-5
RWKV_DECAY_OFFSET = 0.5
MASK_VALUE = float(np.finfo(np.float32).min)

LANES = 128
HEAD = 64
VMEM_LIMIT = 56 * 1024 * 1024
PROJ_TM = 512
MLP_TM = 512
MLP_TF = 1024
CHUNKS_PER_STEP = 8
ATTN_TQ = 512
ATTN_TK = 1024
ATTN_Q_SCALE = (64 ** -0.5) * math.log2(math.e)


def _iota(shape, axis):
    return lax.broadcasted_iota(jnp.int32, shape, axis)


_NN = (((1,), (0,)), ((), ()))
_NT = (((1,), (1,)), ((), ()))
_TN = (((0,), (0,)), ((), ()))


def _mm(a, b, dims=_NN):
    return lax.dot_general(a.astype(BF16), b.astype(BF16), dims, preferred_element_type=F32)


def _split3(x):
    h = x.astype(BF16)
    r = x - h.astype(F32)
    m = r.astype(BF16)
    lo = (r - m.astype(F32)).astype(BF16)
    return h, m, lo


def _mm_exact_rhs(x, ones_bf16, dims=_NN):
    h, m, lo = _split3(x)
    d = functools.partial(lax.dot_general, dimension_numbers=dims, preferred_element_type=F32)
    return d(h, ones_bf16) + d(m, ones_bf16) + d(lo, ones_bf16)


def _mm_exact_lhs(ones_bf16, x, dims=_NN):
    h, m, lo = _split3(x)
    d = functools.partial(lax.dot_general, dimension_numbers=dims, preferred_element_type=F32)
    return d(ones_bf16, h) + d(ones_bf16, m) + d(ones_bf16, lo)


def _selector_matrices(tl, c):
    tok = np.arange(tl)
    same = (tok[:, None] // c) == (tok[None, :] // c)
    lane = np.arange(4 * HEAD)
    src = np.arange(LANES)
    mats = dict(
        seg=(lane[:, None] // HEAD) == (lane[None, :] // HEAD),
        tril=same & (tok[:, None] >= tok[None, :]),
        triu=same & (tok[:, None] <= tok[None, :]),
        last=tok[None, :] == (tok[:, None] // c) * c + c - 1,
        expg=src[:, None] == lane[None, :] // HEAD,
        expb=src[:, None] == lane[None, :] // HEAD + 4)
    return {name: jnp.asarray(m.astype(np.float32), BF16) for name, m in mats.items()}


def _chunking(seq):
    c = min(CHUNK, seq)
    assert c & (c - 1) == 0 and seq % c == 0, seq
    g = min(CHUNKS_PER_STEP, seq // c)
    while (seq // c) % g:
        g -= 1
    return c, g


def _rmsnorm_rows(x, g):
    return x * lax.rsqrt(jnp.mean(x * x, axis=-1, keepdims=True) + NORM_EPS) * g


def _unit_lower_inverses(a_list, c):
    row = _iota((c, c), 0)
    col = _iota((c, c), 1)

    def same_block(shift):
        return jnp.right_shift(row, shift) == jnp.right_shift(col, shift)

    eye = (row == col).astype(F32)
    leaf = same_block(3)
    ns = [jnp.where(leaf, -a, 0.0) for a in a_list]
    ts = [eye + n for n in ns]
    n2s = [_mm(n, n) for n in ns]
    ts = [t + _mm(t, n2) for t, n2 in zip(ts, n2s)]
    n4s = [_mm(n2, n2) for n2 in n2s]
    ts = [t + _mm(t, n4) for t, n4 in zip(ts, n4s)]
    shift = 3
    while (1 << shift) < c:
        off_mask = same_block(shift + 1) & jnp.logical_not(same_block(shift))
        tos = [_mm(t, jnp.where(off_mask, a, 0.0)) for t, a in zip(ts, a_list)]
        ts = [t - _mm(to, t) for t, to in zip(ts, tos)]
        shift += 1
    return ts


_PROJ_A_QKV = 0
_PROJ_A_Z = 768
_PROJ_B_Q = 1024
_PROJ_B_K = 1536
_PROJ_B_V = 2048
_PROJ_C = 2560
_PROJ_A_AB = 3584
_PROJ_WIDTH = 3712


def _proj_kernel(*refs, n_alias):
    x_ref, g_ref, w_ref, cos_ref, sa_ref, sb_ref = refs[:6]
    aqkv_ref, az_ref, aab_ref, q_ref, k_ref, v_ref, kb_ref, vb_ref, c_ref = refs[6 + n_alias:]
    xn = _rmsnorm_rows(x_ref[...], g_ref[...]).astype(BF16)

    def proj(c0, c1):
        return jnp.dot(xn, w_ref[:, c0:c1], preferred_element_type=F32)

    a_all = proj(_PROJ_A_QKV, _PROJ_B_Q)
    aqkv_ref[...] = a_all[:, :_PROJ_A_Z]
    az_ref[...] = a_all[:, _PROJ_A_Z:]
    cos = cos_ref[...]
    sa = sa_ref[...]
    sb = sb_ref[...]

    def rope(x):
        return x * cos + pltpu.roll(x, LANES - 32, 1) * sa + pltpu.roll(x, 32, 1) * sb

    q_all = proj(_PROJ_B_Q, _PROJ_B_K)
    k_all = proj(_PROJ_B_K, _PROJ_B_V)
    for h in range(4):
        lo, hi = h * LANES, (h + 1) * LANES
        q_ref[:, lo:hi] = (rope(q_all[:, lo:hi]) * ATTN_Q_SCALE).astype(BF16)
        k = rope(k_all[:, lo:hi])
        k_ref[:, h, :] = k
        kb_ref[:, lo:hi] = k.astype(BF16)
    v = proj(_PROJ_B_V, _PROJ_C)
    for h in range(4):
        v_ref[:, h, :] = v[:, h * LANES:(h + 1) * LANES]
    vb_ref[...] = v.astype(BF16)
    c_all = proj(_PROJ_C, _PROJ_WIDTH)
    c_ref[...] = c_all[:, :_PROJ_A_AB - _PROJ_C]
    aab_ref[...] = c_all[:, _PROJ_A_AB - _PROJ_C:]


def _input_projection(x2d, g_row, w_bf16, rope_tabs, seq_len, tm, layer, depth, kv_stacked):
    t, d = x2d.shape
    nt = t // tm
    cos_t, sa_t, sb_t = rope_tabs
    if tm >= seq_len:
        reps = tm // seq_len
        cos_t, sa_t, sb_t = (jnp.tile(a, (reps, 1)) for a in (cos_t, sa_t, sb_t))
        tab_map = lambda i: (0, 0)
    else:
        per_seq = seq_len // tm
        tab_map = lambda i: (i % per_seq, 0)
    row = lambda i: (i, 0)
    fixed = lambda i: (0, 0)
    tab_spec = pl.BlockSpec((tm, LANES), tab_map)
    widths = [(768, F32), (256, F32), (128, F32), (512, BF16), (512, F32), (512, F32),
              (512, BF16), (512, BF16), (1024, F32)]
    stacked = (4, 5)
    out_specs = [pl.BlockSpec((None, tm, 4, LANES), lambda i: (layer, i, 0, 0)) if n in stacked
                 else pl.BlockSpec((tm, w), row) for n, (w, _) in enumerate(widths)]
    out_shape = [jax.ShapeDtypeStruct((depth, t, 4, LANES) if n in stacked else (t, w), dt)
                 for n, (w, dt) in enumerate(widths)]
    in_specs = [pl.BlockSpec((tm, d), row), pl.BlockSpec((1, d), fixed),
                pl.BlockSpec((d, _PROJ_WIDTH), fixed), tab_spec, tab_spec, tab_spec]
    operands = [x2d, g_row, w_bf16, cos_t, sa_t, sb_t]
    aliases = {}
    if kv_stacked is not None:
        for out_idx, arr in zip(stacked, kv_stacked):
            aliases[len(operands)] = out_idx
            in_specs.append(pl.BlockSpec(memory_space=pl.ANY))
            operands.append(arr)
    return pl.pallas_call(
        functools.partial(_proj_kernel, n_alias=len(aliases)),
        grid=(nt,),
        in_specs=in_specs,
        out_specs=out_specs,
        out_shape=out_shape,
        input_output_aliases=aliases,
        compiler_params=pltpu.CompilerParams(dimension_semantics=("parallel",),
                                             vmem_limit_bytes=VMEM_LIMIT),
        name="input_projection",
    )(*operands)


def _delta_kernel(qkv_ref, z_ref, ab_ref, conv0_ref, s0_ref, convw_ref, alog_ref, dtb_ref, anorm_ref,
                  seg_ref, tril_ref, triu_ref, last_ref, expg_ref, expb_ref,
                  mix_ref, convn_ref, sn_ref, xp_ref, s_ref, o_ref, *, c, g, nl):
    l = pl.program_id(1)
    tl = c * g

    @pl.when(l == 0)
    def _():
        xp_ref[5:8, :] = conv0_ref[0]
        s_ref[...] = s0_ref[0]

    x = qkv_ref[0]
    xp_ref[8:8 + tl, :] = x
    w = convw_ref[...]
    y = xp_ref[5:5 + tl, :] * w[0:1]
    y = y + xp_ref[6:6 + tl, :] * w[1:2]
    y = y + xp_ref[7:7 + tl, :] * w[2:3]
    y = y + x * w[3:4]
    tail = xp_ref[5 + tl:8 + tl, :]
    xp_ref[5:8, :] = tail

    @pl.when(l == nl - 1)
    def _():
        convn_ref[0] = tail

    act = y * jax.nn.sigmoid(y)
    seg = seg_ref[...]

    def l2n(t):
        return t * lax.rsqrt(_mm_exact_rhs(t * t, seg) + L2_EPS)

    q = l2n(act[:, 0:256]) * (HEAD ** -0.5)
    k = l2n(act[:, 256:512])
    v = act[:, 512:768]

    ab = ab_ref[0]
    gl = -jnp.exp(alog_ref[...]) * jax.nn.softplus(ab + dtb_ref[...])
    beta_w = _mm_exact_rhs(jax.nn.sigmoid(ab), expb_ref[...])
    g_w = _mm_exact_lhs(tril_ref[...], _mm_exact_rhs(gl, expg_ref[...]))
    g_rows = _mm_exact_rhs(gl, triu_ref[...], _TN)
    g_last_w = _mm_exact_lhs(last_ref[...], g_w)
    eg_w = jnp.exp(g_w)
    v_beta = v * beta_w
    k_beta_eg = k * (beta_w * eg_w)
    q_dec = q * eg_w
    k_tail_w = k * jnp.exp(g_last_w - g_w)
    decay_end_w = jnp.exp(g_last_w)

    row = _iota((c, c), 0)
    col = _iota((c, c), 1)
    lower = row >= col
    strict = row > col

    probs = [(slice(ci * c, (ci + 1) * c), h) for ci in range(g) for h in range(4)]
    heads = lambda h: slice(h * HEAD, (h + 1) * HEAD)
    decay = [jnp.where(lower, jnp.exp(jnp.where(lower, g_w[rows, heads(h)][:, :c] - g_rows[h:h + 1, rows], 0.0)),
                       0.0) for rows, h in probs]
    kh = [k[rows, heads(h)] for rows, h in probs]
    kq = [_mm(jnp.concatenate([kk_, q[rows, heads(h)]], axis=0), kk_, _NT)
          for kk_, (rows, h) in zip(kh, probs)]
    t_inv = _unit_lower_inverses(
        [jnp.where(strict, beta_w[rows, heads(h)][:, :c] * x[:c] * d, 0.0)
         for x, d, (rows, h) in zip(kq, decay, probs)], c)
    sol = [_mm(t, jnp.concatenate([v_beta[rows, heads(h)], k_beta_eg[rows, heads(h)]], axis=1))
           for t, (rows, h) in zip(t_inv, probs)]
    qk = [jnp.where(lower, x[c:] * d, 0.0) for x, d in zip(kq, decay)]
    kt_sol = [_mm(k_tail_w[rows, heads(h)], so, _TN) for so, (rows, h) in zip(sol, probs)]
    qk_sol = [_mm(x, so) for x, so in zip(qk, sol)]
    q_eff = [q_dec[rows, heads(h)] - x[:, HEAD:] for x, (rows, h) in zip(qk_sol, probs)]
    decay_end = [decay_end_w[rows.stop - 1:rows.stop, heads(h)] for rows, h in probs]

    states = [s_ref[h] for h in range(4)]
    for ci in range(g):
        rows = slice(ci * c, (ci + 1) * c)
        idx = [ci * 4 + h for h in range(4)]
        o_new = [_mm(q_eff[n], states[h]) + qk_sol[n][:, :HEAD] for h, n in enumerate(idx)]
        states = [states[h] * decay_end[n] + kt_sol[n][:, :HEAD] - _mm(kt_sol[n][:, HEAD:], states[h])
                  for h, n in enumerate(idx)]
        for h in range(4):
            o_ref[rows, heads(h)] = o_new[h]
    for h in range(4):
        s_ref[h] = states[h]

    o = o_ref[...]
    ms = _mm_exact_rhs(o * o, seg) * (1.0 / HEAD)
    z = z_ref[0]
    ao = (o * lax.rsqrt(ms + NORM_EPS) * anorm_ref[...]) * (z * jax.nn.sigmoid(z))
    mix_ref[0] = ao.astype(BF16)

    @pl.when(l == nl - 1)
    def _():
        sn_ref[0] = s_ref[...]


def _gated_delta(a_qkv, a_z, a_ab, conv0, s0, conv_w, alog_row, dtb_row, anorm_row):
    b, seq, _ = a_qkv.shape
    c, g = _chunking(seq)
    tl = c * g
    nl = seq // tl
    tile = lambda i, l: (i, l, 0)
    per_b3 = lambda i, l: (i, 0, 0)
    per_b4 = lambda i, l: (i, 0, 0, 0)
    fixed = lambda i, l: (0, 0)
    sel = _selector_matrices(tl, c)
    consts = [sel[name] for name in ("seg", "tril", "triu", "last", "expg", "expb")]
    return pl.pallas_call(
        functools.partial(_delta_kernel, c=c, g=g, nl=nl),
        grid=(b, nl),
        in_specs=[pl.BlockSpec((1, tl, 768), tile), pl.BlockSpec((1, tl, 256), tile),
                  pl.BlockSpec((1, tl, LANES), tile), pl.BlockSpec((1, 3, 768), per_b3),
                  pl.BlockSpec((1, 4, HEAD, HEAD), per_b4), pl.BlockSpec((4, 768), fixed),
                  pl.BlockSpec((1, LANES), fixed), pl.BlockSpec((1, LANES), fixed),
                  pl.BlockSpec((1, 256), fixed)] + [pl.BlockSpec(m.shape, fixed) for m in consts],
        out_specs=[pl.BlockSpec((1, tl, 256), tile), pl.BlockSpec((1, 3, 768), per_b3),
                   pl.BlockSpec((1, 4, HEAD, HEAD), per_b4)],
        out_shape=[jax.ShapeDtypeStruct((b, seq, 256), BF16),
                   jax.ShapeDtypeStruct((b, 3, 768), F32),
                   jax.ShapeDtypeStruct((b, 4, HEAD, HEAD), F32)],
        scratch_shapes=[pltpu.VMEM((tl + 8, 768), F32), pltpu.VMEM((4, HEAD, HEAD), F32),
                        pltpu.VMEM((tl, 256), F32)],
        compiler_params=pltpu.CompilerParams(dimension_semantics=("parallel", "arbitrary"),
                                             vmem_limit_bytes=VMEM_LIMIT),
        name="gated_delta",
    )(a_qkv, a_z, a_ab, conv0, s0, conv_w, alog_row, dtb_row, anorm_row, *consts)


def _rwkv_kernel(c_ref, shift0_ref, s0_ref, mu_ref, w0_ref, wup_ref, a0_ref, aup_ref, gup_ref,
                 kk_ref, ka_ref, rk_ref, lnw_ref, lnb_ref, seg_ref, tril_ref,
                 mix_ref, shiftn_ref, sn_ref, xp_ref, s_ref, y_ref, *, c, g, nl):
    l = pl.program_id(1)
    tl = c * g

    @pl.when(l == 0)
    def _():
        xp_ref[7:8, :] = shift0_ref[0]
        s_ref[...] = s0_ref[0]

    raw = c_ref[0]
    xp_ref[8:8 + tl, :] = raw
    prev = xp_ref[7:7 + tl, :]
    last = raw[tl - 1:tl, :]
    xp_ref[7:8, :] = last

    @pl.when(l == nl - 1)
    def _():
        shiftn_ref[0] = last

    x = raw + (prev - raw) * mu_ref[...]
    cr, ck, cv = x[:, 0:256], x[:, 256:512], x[:, 512:768]
    c_wa = x[:, 768:896]
    c_g = x[:, 896:1024]
    w_log = -jnp.exp(-jax.nn.softplus(-(w0_ref[...] + _mm(jnp.tanh(c_wa), wup_ref[...])))
                     - RWKV_DECAY_OFFSET)
    ca = jax.nn.sigmoid(a0_ref[...] + _mm(c_wa, aup_ref[...]))
    cg = _mm(jax.nn.sigmoid(c_g), gup_ref[...])
    seg = seg_ref[...]
    kkv = ck * kk_ref[...]
    kk = kkv * lax.rsqrt(_mm_exact_rhs(kkv * kkv, seg) + L2_EPS)
    ck = ck * (1.0 + (ca - 1.0) * ka_ref[...])

    g_cum = _mm_exact_lhs(tril_ref[...], w_log)
    e_pos = jnp.exp(g_cum)
    e_neg = jnp.exp(-g_cum)
    a_t = -kk * jnp.exp(g_cum - w_log)
    b_t = (kk * ca) * e_neg
    k_t = ck * e_neg
    r_t = cr * e_pos

    row = _iota((c, c), 0)
    col = _iota((c, c), 1)
    lower = row >= col
    strict = row > col

    probs = [(slice(ci * c, (ci + 1) * c), h) for ci in range(g) for h in range(4)]
    heads = lambda h: slice(h * HEAD, (h + 1) * HEAD)
    bh = [b_t[rows, heads(h)] for rows, h in probs]
    kh = [k_t[rows, heads(h)] for rows, h in probs]
    vh = [cv[rows, heads(h)] for rows, h in probs]
    ar = [jnp.concatenate([a_t[rows, heads(h)], r_t[rows, heads(h)]], axis=0) for rows, h in probs]
    pb = [_mm(x, y_, _NT) for x, y_ in zip(ar, bh)]
    pk = [_mm(x, y_, _NT) for x, y_ in zip(ar, kh)]
    t_inv = _unit_lower_inverses([-jnp.where(strict, x[:c], 0.0) for x in pb], c)
    n_rb = [jnp.where(lower, x[c:], 0.0) for x in pb]
    mn = [_mm(jnp.concatenate([jnp.where(strict, x[:c], 0.0), jnp.where(lower, x[c:], 0.0)], axis=0), vv)
          for x, vv in zip(pk, vh)]
    bk = [jnp.concatenate([x, y_], axis=0) for x, y_ in zip(bh, kh)]
    g_end = [e_pos[rows.stop - 1:rows.stop, heads(h)] for rows, h in probs]
    ta = [_mm(t, jnp.concatenate([x[:c], m_[:c]], axis=1)) for t, x, m_ in zip(t_inv, ar, mn)]
    p_mat = [_mm(x[:, :HEAD], y_, _TN) for x, y_ in zip(ta, bh)]
    c_mat = [_mm(jnp.concatenate([x[:, HEAD:], vv], axis=0), y_, _TN) for x, vv, y_ in zip(ta, vh, bk)]
    nr = [_mm(x, y_) for x, y_ in zip(n_rb, ta)]
    r_eff = [x[c:] + y_[:, :HEAD] for x, y_ in zip(ar, nr)]
    y_off = [x[:, HEAD:] + m_[c:] for x, m_ in zip(nr, mn)]

    states = [s_ref[h] for h in range(4)]
    for ci in range(g):
        rows = slice(ci * c, (ci + 1) * c)
        idx = [ci * 4 + h for h in range(4)]
        y_new = [_mm(r_eff[n], states[h], _NT) + y_off[n] for h, n in enumerate(idx)]
        states = [(states[h] + _mm(states[h], p_mat[n]) + c_mat[n]) * g_end[n] for h, n in enumerate(idx)]
        for h in range(4):
            y_ref[rows, heads(h)] = y_new[h]
    for h in range(4):
        s_ref[h] = states[h]

    y = y_ref[...]
    mean = _mm_exact_rhs(y, seg) * (1.0 / HEAD)
    yc = y - mean
    var = _mm_exact_rhs(yc * yc, seg) * (1.0 / HEAD)
    cy = (yc * lax.rsqrt(var + C_LN_EPS)) * lnw_ref[...] + lnb_ref[...]
    bonus = _mm_exact_rhs(cr * ck * rk_ref[...], seg) * cv
    mix_ref[0] = ((cy + bonus) * cg).astype(BF16)

    @pl.when(l == nl - 1)
    def _():
        sn_ref[0] = s_ref[...]


def _rwkv7(c_raw, shift0, s0, params):
    b, seq, width = c_raw.shape
    c, g = _chunking(seq)
    tl = c * g
    nl = seq // tl
    tile = lambda i, l: (i, l, 0)
    per_b3 = lambda i, l: (i, 0, 0)
    per_b4 = lambda i, l: (i, 0, 0, 0)
    fixed = lambda i, l: (0, 0)
    sel = _selector_matrices(tl, c)
    params = tuple(params) + (sel["seg"], sel["tril"])
    param_specs = [pl.BlockSpec(p.shape, fixed) for p in params]
    return pl.pallas_call(
        functools.partial(_rwkv_kernel, c=c, g=g, nl=nl),
        grid=(b, nl),
        in_specs=[pl.BlockSpec((1, tl, width), tile), pl.BlockSpec((1, 1, width), per_b3),
                  pl.BlockSpec((1, 4, HEAD, HEAD), per_b4)] + param_specs,
        out_specs=[pl.BlockSpec((1, tl, 256), tile), pl.BlockSpec((1, 1, width), per_b3),
                   pl.BlockSpec((1, 4, HEAD, HEAD), per_b4)],
        out_shape=[jax.ShapeDtypeStruct((b, seq, 256), BF16),
                   jax.ShapeDtypeStruct((b, 1, width), F32),
                   jax.ShapeDtypeStruct((b, 4, HEAD, HEAD), F32)],
        scratch_shapes=[pltpu.VMEM((tl + 8, width), F32), pltpu.VMEM((4, HEAD, HEAD), F32),
                        pltpu.VMEM((tl, 256), F32)],
        compiler_params=pltpu.CompilerParams(dimension_semantics=("parallel", "arbitrary"),
                                             vmem_limit_bytes=VMEM_LIMIT),
        name="rwkv7",
    )(c_raw, shift0, s0, *params)


def _attn_kernel(i_tab, j_tab, flag_tab, q_ref, k_ref, v_ref, lam_ref, bn_ref, o_ref,
                 m_ref, acc_ref, *, tq, tk, past, lk_true, lam_init, half_tiles):
    p = pl.program_id(1)
    i = i_tab[p]
    j = j_tab[p]

    @pl.when(j == 0)
    def _():
        m_ref[...] = jnp.full(m_ref.shape, MASK_VALUE, F32)
        acc_ref[...] = jnp.zeros(acc_ref.shape, F32)

    def step(masked, nkeys):
        nslab = nkeys // LANES
        if masked:
            q_pos = past + i * tq + _iota((tq, nkeys), 0)
            k_pos = j * tk + _iota((tq, nkeys), 1)
            visible = (jnp.right_shift(k_pos, 6) <= jnp.right_shift(q_pos, 6)) & (k_pos < lk_true)
        q = q_ref[0]
        k = k_ref[0, 0:nkeys, :]
        v = v_ref[0, 0:nkeys, :]

        def scores(hm):
            sl = slice(hm * HEAD, (hm + 1) * HEAD)
            return lax.dot_general(q[:, sl], k[:, sl], _NT, preferred_element_type=F32)

        ones = jnp.ones((nkeys, LANES), BF16)
        s_next = scores(0)
        for hm in range(8):
            h = hm // 2
            s = s_next
            if hm + 1 < 8:
                s_next = scores(hm + 1)
            if masked:
                s = jnp.where(visible, s, MASK_VALUE)
            slabs = [s[:, n * LANES:(n + 1) * LANES] for n in range(nslab)]
            m_cur = functools.reduce(jnp.maximum, slabs)
            m_prev = m_ref[hm]
            m_new = jnp.maximum(m_prev, jnp.max(m_cur, axis=1, keepdims=True))
            alpha = jnp.exp2(m_prev - m_new)
            e = jnp.concatenate([jnp.exp2(sb - m_new).astype(BF16) for sb in slabs], axis=1)
            v_ones = jnp.concatenate([v[:, h * LANES:(h + 1) * LANES], ones], axis=1)
            acc_ref[hm] = (jnp.concatenate([alpha, alpha], axis=1) * acc_ref[hm]
                           + jnp.dot(e, v_ones, preferred_element_type=F32))
            m_ref[hm] = m_new

    kind = jnp.right_shift(flag_tab[p], 1)

    @pl.when(kind == 0)
    def _():
        step(False, tk)

    @pl.when(kind == 1)
    def _():
        step(True, tk)

    if half_tiles:
        @pl.when(kind == 2)
        def _():
            step(True, tk // 2)

    @pl.when(jnp.bitwise_and(flag_tab[p], 1) != 0)
    def _():
        lp = lam_ref[...]
        lam = (jnp.exp(jnp.sum(lp[0:1] * lp[1:2], axis=1, keepdims=True))
               - jnp.exp(jnp.sum(lp[2:3] * lp[3:4], axis=1, keepdims=True)) + lam_init)
        for h in range(4):
            a1 = acc_ref[2 * h]
            a2 = acc_ref[2 * h + 1]
            o = a1[:, :LANES] / a1[:, LANES:] - lam * (a2[:, :LANES] / a2[:, LANES:])
            o = _rmsnorm_rows(o, bn_ref[...]) * (1.0 - lam_init)
            o_ref[0, :, h * LANES:(h + 1) * LANES] = o.astype(BF16)


def _diff_attention(q, k, v, lam_params, bnorm_row, past, lam_init, tq, tk):
    b, lq, width = q.shape
    lk_true = k.shape[1]
    nk = -(-lk_true // tk)
    if nk * tk != lk_true:
        pad = ((0, 0), (0, nk * tk - lk_true), (0, 0))
        k = jnp.pad(k, pad)
        v = jnp.pad(v, pad)
    nq = lq // tq
    half_tiles = (tk // 2) % LANES == 0
    i_list, j_list, flag_list = [], [], []
    for i in range(nq):
        first_pos = past + i * tq
        last_pos = first_pos + tq - 1
        visible_end = min((last_pos // CHUNK + 1) * CHUNK, lk_true)
        j_max = (visible_end - 1) // tk
        for j in range(j_max + 1):
            key_end = (j + 1) * tk
            if (key_end - 1) // CHUNK <= first_pos // CHUNK and key_end <= lk_true:
                kind = 0
            elif half_tiles and visible_end - j * tk <= tk // 2:
                kind = 2
            else:
                kind = 1
            i_list.append(i)
            j_list.append(j)
            flag_list.append((1 if j == j_max else 0) + 2 * kind)
    tabs = [jnp.asarray(np.asarray(t, np.int32)) for t in (i_list, j_list, flag_list)]
    grid_spec = pltpu.PrefetchScalarGridSpec(
        num_scalar_prefetch=3,
        grid=(b, len(i_list)),
        in_specs=[pl.BlockSpec((1, tq, width), lambda bi, p, it, jt, lt: (bi, it[p], 0)),
                  pl.BlockSpec((1, tk, width), lambda bi, p, it, jt, lt: (bi, jt[p], 0)),
                  pl.BlockSpec((1, tk, width), lambda bi, p, it, jt, lt: (bi, jt[p], 0)),
                  pl.BlockSpec((4, HEAD), lambda bi, p, it, jt, lt: (0, 0)),
                  pl.BlockSpec((1, LANES), lambda bi, p, it, jt, lt: (0, 0))],
        out_specs=pl.BlockSpec((1, tq, width), lambda bi, p, it, jt, lt: (bi, it[p], 0)),
        scratch_shapes=[pltpu.VMEM((8, tq, LANES), F32), pltpu.VMEM((8, tq, 2 * LANES), F32)],
    )
    return pl.pallas_call(
        functools.partial(_attn_kernel, tq=tq, tk=tk, past=past, lk_true=lk_true, lam_init=lam_init,
                          half_tiles=half_tiles),
        grid_spec=grid_spec,
        out_shape=jax.ShapeDtypeStruct((b, lq, width), BF16),
        compiler_params=pltpu.CompilerParams(dimension_semantics=("parallel", "arbitrary"),
                                             vmem_limit_bytes=VMEM_LIMIT),
        name="diff_attention",
    )(*tabs, q, k, v, lam_params, bnorm_row)


def _mlp_kernel(h_ref, ma_ref, mb_ref, mc_ref, wo_ref, nf_ref, w1_ref, w2_ref, np_ref, wg_ref,
                wp_ref, p_ref, nfin_ref, o_ref, *, tf, final_norm):
    mixed = (jnp.dot(ma_ref[...], wo_ref[0:256, :], preferred_element_type=F32)
             + jnp.dot(mb_ref[...], wo_ref[256:768, :], preferred_element_type=F32)
             + jnp.dot(mc_ref[...], wo_ref[768:1024, :], preferred_element_type=F32))
    h1 = h_ref[...] + mixed
    xn = _rmsnorm_rows(h1, nf_ref[...]).astype(BF16)
    h2 = h1
    for f0 in range(0, w1_ref.shape[1], tf):
        u = jnp.maximum(jnp.dot(xn, w1_ref[:, f0:f0 + tf], preferred_element_type=F32), 0.0)
        h2 = h2 + jnp.dot((u * u).astype(BF16), w2_ref[f0:f0 + tf, :], preferred_element_type=F32)
    gate = jax.nn.sigmoid(jnp.dot(_rmsnorm_rows(h2, np_ref[...]).astype(BF16), wg_ref[...],
                                  preferred_element_type=F32))
    h3 = h2 + gate * jnp.dot(p_ref[...].astype(BF16), wp_ref[...], preferred_element_type=F32)
    if final_norm:
        h3 = _rmsnorm_rows(h3, nfin_ref[...])
    o_ref[...] = h3


def _mix_mlp(h2d, mix_a, mix_b, mix_c, w_out, norm_ffn, w_ff1, w_ff2, norm_ple, w_gate, w_proj,
             p2d, norm_final, final_norm, tm, tf):
    t, d = h2d.shape
    dff = w_ff1.shape[1]
    row = lambda i: (i, 0)
    fixed = lambda i: (0, 0)
    resident = lambda shape: pl.BlockSpec(shape, fixed, pipeline_mode=pl.Buffered(1))
    return pl.pallas_call(
        functools.partial(_mlp_kernel, tf=tf, final_norm=final_norm),
        grid=(t // tm,),
        in_specs=[pl.BlockSpec((tm, d), row), pl.BlockSpec((tm, 256), row),
                  pl.BlockSpec((tm, 512), row), pl.BlockSpec((tm, 256), row),
                  resident((d, d)), resident((1, d)), resident((d, dff)), resident((dff, d)),
                  resident((1, d)), resident((d, d)), resident((p2d.shape[1], d)),
                  pl.BlockSpec((tm, p2d.shape[1]), row), resident((1, d))],
        out_specs=pl.BlockSpec((tm, d), row),
        out_shape=jax.ShapeDtypeStruct((t, d), F32),
        compiler_params=pltpu.CompilerParams(dimension_semantics=("parallel",),
                                             vmem_limit_bytes=VMEM_LIMIT),
        name="mix_mlp",
    )(h2d, mix_a, mix_b, mix_c, w_out, norm_ffn, w_ff1, w_ff2, norm_ple, w_gate, w_proj, p2d,
      norm_final)


def _rope_tables(past, seq_len):
    half = HEAD // 2
    inv = ROPE_THETA ** (-2.0 * jnp.arange(half, dtype=F32) / HEAD)
    pos = past + jnp.arange(seq_len, dtype=jnp.int32)
    ang = pos.astype(F32)[:, None] * inv[None, :]
    cos, sin, zero = jnp.cos(ang), jnp.sin(ang), jnp.zeros_like(ang)
    rep = LANES // HEAD
    cos_t = jnp.tile(jnp.concatenate([cos, cos], axis=1), (1, rep))
    sa_t = jnp.tile(jnp.concatenate([-sin, zero], axis=1), (1, rep))
    sb_t = jnp.tile(jnp.concatenate([zero, sin], axis=1), (1, rep))
    return cos_t, sa_t, sb_t


def _pad_lanes(v, width=LANES):
    return jnp.pad(v, (0, width - v.shape[0]))[None, :]


def _prep_layer(i, norm_mix, w_in, a_conv_w, a_A_log, a_dt_bias, a_norm,
                b_lam_q1, b_lam_k1, b_lam_q2, b_lam_k2, b_norm,
                c_mu, c_w0, c_w_up, c_a0, c_a_up, c_g_up, c_k_k, c_k_a, c_r_k, c_ln_w, c_ln_b,
                w_out, norm_ffn, w_ff1, w_ff2, norm_ple, w_ple_gate, w_ple_proj):
    w = w_in[i]
    d = w.shape[0]
    w_perm = jnp.concatenate(
        [w[:, 0:1024], w[:, 1032:3592], w[:, 1024:1032],
         jnp.zeros((d, _PROJ_WIDTH - 3592), w.dtype)], axis=1).astype(BF16)
    zeros_rank = jnp.zeros_like(c_w_up[i])
    rwkv = (c_mu[i][None, :], c_w0[i][None, :],
            jnp.concatenate([c_w_up[i], zeros_rank], axis=0).astype(BF16), c_a0[i][None, :],
            jnp.concatenate([jnp.zeros_like(c_a_up[i]), c_a_up[i]], axis=0).astype(BF16),
            c_g_up[i].astype(BF16), c_k_k[i][None, :], c_k_a[i][None, :],
            c_r_k[i].reshape(1, -1), c_ln_w[i][None, :], c_ln_b[i][None, :])
    return dict(
        norm_mix=norm_mix[i][None, :], w_in=w_perm, conv_w=a_conv_w[i],
        alog=_pad_lanes(a_A_log[i]), dtb=_pad_lanes(a_dt_bias[i]),
        anorm=jnp.tile(a_norm[i], 4)[None, :],
        lam=jnp.stack([b_lam_q1[i], b_lam_k1[i], b_lam_q2[i], b_lam_k2[i]], axis=0),
        bnorm=b_norm[i][None, :], rwkv=rwkv,
        w_out=w_out[i].astype(BF16), norm_ffn=norm_ffn[i][None, :],
        w_ff1=w_ff1[i].astype(BF16), w_ff2=w_ff2[i].astype(BF16),
        norm_ple=norm_ple[i][None, :], w_gate=w_ple_gate[i].astype(BF16),
        w_proj=w_ple_proj[i].astype(BF16))


def _pick_tile(n, target):
    t = min(n, target)
    while n % t:
        t //= 2
    return t


def _trunk(x, p, cache_k, cache_v, conv_buf, delta_s, shift_prev, wkv_s, layers, norm_final):
    b, seq, d = x.shape
    depth = len(layers)
    past = cache_k.shape[2]
    t = b * seq
    tm = _pick_tile(t, PROJ_TM)
    rope_tabs = _rope_tables(past, seq)
    h = x.reshape(t, d)
    states = []
    kv_stacked = None
    for i, lp in enumerate(layers):
        a_qkv, a_z, a_ab, q_b, k_f, v_f, k_b, v_b, c_raw = _input_projection(
            h, lp["norm_mix"], lp["w_in"], rope_tabs, seq, tm, i, depth, kv_stacked)
        kv_stacked = (k_f, v_f)
        mix_a, conv_n, delta_n = _gated_delta(
            a_qkv.reshape(b, seq, -1), a_z.reshape(b, seq, -1), a_ab.reshape(b, seq, -1),
            conv_buf[i], delta_s[i], lp["conv_w"], lp["alog"], lp["dtb"], lp["anorm"])
        k_all = k_b.reshape(b, seq, -1)
        v_all = v_b.reshape(b, seq, -1)
        if past:
            k_all = jnp.concatenate([cache_k[i].reshape(b, past, -1).astype(BF16), k_all], axis=1)
            v_all = jnp.concatenate([cache_v[i].reshape(b, past, -1).astype(BF16), v_all], axis=1)
        lam_init = 0.8 - 0.6 * math.exp(-0.3 * i)
        mix_b = _diff_attention(q_b.reshape(b, seq, -1), k_all, v_all, lp["lam"], lp["bnorm"],
                                past, lam_init, _pick_tile(seq, ATTN_TQ),
                                ATTN_TK if past + seq >= 8 * ATTN_TK else ATTN_TK // 2)
        mix_c, shift_n, wkv_n = _rwkv7(c_raw.reshape(b, seq, -1), shift_prev[i][:, None, :],
                                       wkv_s[i], lp["rwkv"])
        h = _mix_mlp(h, mix_a.reshape(t, -1), mix_b.reshape(t, -1), mix_c.reshape(t, -1),
                     lp["w_out"], lp["norm_ffn"], lp["w_ff1"], lp["w_ff2"], lp["norm_ple"],
                     lp["w_gate"], lp["w_proj"], p[i].reshape(t, -1), norm_final[None, :],
                     i == depth - 1, _pick_tile(t, MLP_TM), MLP_TF)
        states.append((conv_n, delta_n, shift_n[:, 0, :], wkv_n))
    conv_all, delta_all, shift_all, wkv_all = (
        jnp.stack([st[j] for st in states], axis=0) for j in range(4))
    k_all_layers, v_all_layers = (a.reshape(depth, b, seq, 4, LANES) for a in kv_stacked)
    return h.reshape(b, seq, d), [conv_all, delta_all, k_all_layers, v_all_layers, shift_all, wkv_all]


def kernel(x_prompt, x_sample, cache_b_k, cache_b_v, state_a_conv, state_a_delta, state_c_shift, state_c_wkv, p_prompt, p_sample, norm_mix, w_in, a_conv_w, a_A_log, a_dt_bias, a_norm, b_lam_q1, b_lam_k1, b_lam_q2, b_lam_k2, b_norm, c_mu, c_w0, c_w_up, c_a0, c_a_up, c_g_up, c_k_k, c_k_a, c_r_k, c_ln_w, c_ln_b, w_out, norm_ffn, w_ff1, w_ff2, norm_ple, w_ple_gate, w_ple_proj, norm_final):
    depth = w_in.shape[0]
    layers = [_prep_layer(i, norm_mix, w_in, a_conv_w, a_A_log, a_dt_bias, a_norm,
                          b_lam_q1, b_lam_k1, b_lam_q2, b_lam_k2, b_norm,
                          c_mu, c_w0, c_w_up, c_a0, c_a_up, c_g_up, c_k_k, c_k_a, c_r_k, c_ln_w, c_ln_b,
                          w_out, norm_ffn, w_ff1, w_ff2, norm_ple, w_ple_gate, w_ple_proj)
              for i in range(depth)]
    bp = x_prompt.shape[0]
    dt = x_prompt.dtype
    zeros = lambda ref: jnp.zeros((depth, bp) + ref.shape[2:], dt)
    empty_k = jnp.zeros((depth, bp, 0) + cache_b_k.shape[3:], dt)
    empty_v = jnp.zeros((depth, bp, 0) + cache_b_v.shape[3:], dt)
    y_prompt, st_p = _trunk(x_prompt, p_prompt, empty_k, empty_v, zeros(state_a_conv),
                            zeros(state_a_delta), zeros(state_c_shift), zeros(state_c_wkv),
                            layers, norm_final)
    y_sample, st_s = _trunk(x_sample, p_sample, cache_b_k, cache_b_v, state_a_conv, state_a_delta,
                            state_c_shift, state_c_wkv, layers, norm_final)
    return (y_prompt, y_sample, *st_p, *st_s)
```

```python
import functools
import math

import numpy as np
import jax
import jax.numpy as jnp
from jax import lax
from jax.experimental import pallas as pl
from jax.experimental.pallas import tpu as pltpu

F32 = jnp.float32
BF16 = jnp.bfloat16

CHUNK = 64
ROPE_THETA = 10000.0
NORM_EPS = 1e-6
L2_EPS = 1e-6
C_LN_EPS = 64e-5
RWKV_DECAY_OFFSET = 0.5
MASK_VALUE = float(np.finfo(np.float32).min)

LANES = 128
HEAD = 64
VMEM_LIMIT = 56 * 1024 * 1024
PROJ_TM = 512
MLP_TM = 512
MLP_TF = 1024
CHUNKS_PER_STEP = 8
ATTN_TQ = 512
ATTN_TK = 1024
ATTN_Q_SCALE = (64 ** -0.5) * math.log2(math.e)


def _iota(shape, axis):
    return lax.broadcasted_iota(jnp.int32, shape, axis)


_NN = (((1,), (0,)), ((), ()))
_NT = (((1,), (1,)), ((), ()))
_TN = (((0,), (0,)), ((), ()))


def _mm(a, b, dims=_NN):
    return lax.dot_general(a.astype(BF16), b.astype(BF16), dims, preferred_element_type=F32)


def _split3(x):
    h = x.astype(BF16)
    r = x - h.astype(F32)
    m = r.astype(BF16)
    lo = (r - m.astype(F32)).astype(BF16)
    return h, m, lo


def _mm_exact_rhs(x, ones_bf16, dims=_NN):
    h, m, lo = _split3(x)
    d = functools.partial(lax.dot_general, dimension_numbers=dims, preferred_element_type=F32)
    return d(h, ones_bf16) + d(m, ones_bf16) + d(lo, ones_bf16)


def _head_sums(x, seg_bf16):
    h = x.astype(BF16)
    lo = (x - h.astype(F32)).astype(BF16)
    return (jnp.dot(h, seg_bf16, preferred_element_type=F32)
            + jnp.dot(lo, seg_bf16, preferred_element_type=F32))


def _mm_exact_lhs(ones_bf16, x, dims=_NN):
    h, m, lo = _split3(x)
    d = functools.partial(lax.dot_general, dimension_numbers=dims, preferred_element_type=F32)
    return d(ones_bf16, h) + d(ones_bf16, m) + d(ones_bf16, lo)


def _selector_matrices(tl, c):
    tok = np.arange(tl)
    same = (tok[:, None] // c) == (tok[None, :] // c)
    lane = np.arange(4 * HEAD)
    src = np.arange(LANES)
    mats = dict(
        seg=(lane[:, None] // HEAD) == (lane[None, :] // HEAD),
        tril=same & (tok[:, None] >= tok[None, :]),
        triu=same & (tok[:, None] <= tok[None, :]),
        last=tok[None, :] == (tok[:, None] // c) * c + c - 1,
        expg=src[:, None] == lane[None, :] // HEAD,
        expb=src[:, None] == lane[None, :] // HEAD + 4)
    return {name: jnp.asarray(m.astype(np.float32), BF16) for name, m in mats.items()}


def _chunking(seq):
    c = min(CHUNK, seq)
    assert c & (c - 1) == 0 and seq % c == 0, seq
    g = min(CHUNKS_PER_STEP, seq // c)
    while (seq // c) % g:
        g -= 1
    return c, g


def _rmsnorm_rows(x, g):
    return x * lax.rsqrt(jnp.mean(x * x, axis=-1, keepdims=True) + NORM_EPS) * g


def _unit_lower_inverses(a_list, c):
    row = _iota((c, c), 0)
    col = _iota((c, c), 1)

    def same_block(shift):
        return jnp.right_shift(row, shift) == jnp.right_shift(col, shift)

    eye = (row == col).astype(F32)
    leaf = same_block(3)
    ns = [jnp.where(leaf, -a, 0.0) for a in a_list]
    ts = [eye + n for n in ns]
    n2s = [_mm(n, n) for n in ns]
    ts = [t + _mm(t, n2) for t, n2 in zip(ts, n2s)]
    n4s = [_mm(n2, n2) for n2 in n2s]
    ts = [t + _mm(t, n4) for t, n4 in zip(ts, n4s)]
    shift = 3
    while (1 << shift) < c:
        off_mask = same_block(shift + 1) & jnp.logical_not(same_block(shift))
        tos = [_mm(t, jnp.where(off_mask, a, 0.0)) for t, a in zip(ts, a_list)]
        ts = [t - _mm(to, t) for t, to in zip(ts, tos)]
        shift += 1
    return ts


_PROJ_A_QKV = 0
_PROJ_A_Z = 768
_PROJ_B_Q = 1024
_PROJ_B_K = 1536
_PROJ_B_V = 2048
_PROJ_C = 2560
_PROJ_A_AB = 3584
_PROJ_WIDTH = 3712


def _proj_kernel(*refs, n_alias):
    x_ref, g_ref, w_ref, cos_ref, sa_ref, sb_ref = refs[:6]
    aqkv_ref, az_ref, aab_ref, q_ref, k_ref, v_ref, kb_ref, vb_ref, c_ref = refs[6 + n_alias:]
    xn = _rmsnorm_rows(x_ref[...], g_ref[...]).astype(BF16)

    def proj(c0, c1):
        return jnp.dot(xn, w_ref[:, c0:c1], preferred_element_type=F32)

    a_all = proj(_PROJ_A_QKV, _PROJ_B_Q)
    aqkv_ref[...] = a_all[:, :_PROJ_A_Z]
    az_ref[...] = a_all[:, _PROJ_A_Z:]
    cos = cos_ref[...]
    sa = sa_ref[...]
    sb = sb_ref[...]

    def rope(x):
        return x * cos + pltpu.roll(x, LANES - 32, 1) * sa + pltpu.roll(x, 32, 1) * sb

    q_all = proj(_PROJ_B_Q, _PROJ_B_K)
    k_all = proj(_PROJ_B_K, _PROJ_B_V)
    for h in range(4):
        lo, hi = h * LANES, (h + 1) * LANES
        q_ref[:, lo:hi] = (rope(q_all[:, lo:hi]) * ATTN_Q_SCALE).astype(BF16)
        k = rope(k_all[:, lo:hi])
        k_ref[:, h, :] = k
        kb_ref[:, lo:hi] = k.astype(BF16)
    v = proj(_PROJ_B_V, _PROJ_C)
    for h in range(4):
        v_ref[:, h, :] = v[:, h * LANES:(h + 1) * LANES]
    vb_ref[...] = v.astype(BF16)
    c_all = proj(_PROJ_C, _PROJ_WIDTH)
    c_ref[...] = c_all[:, :_PROJ_A_AB - _PROJ_C]
    aab_ref[...] = c_all[:, _PROJ_A_AB - _PROJ_C:]


def _input_projection(x2d, g_row, w_bf16, rope_tabs, seq_len, tm, layer, depth, kv_stacked):
    t, d = x2d.shape
    nt = t // tm
    cos_t, sa_t, sb_t = rope_tabs
    if tm >= seq_len:
        reps = tm // seq_len
        cos_t, sa_t, sb_t = (jnp.tile(a, (reps, 1)) for a in (cos_t, sa_t, sb_t))
        tab_map = lambda i: (0, 0)
    else:
        per_seq = seq_len // tm
        tab_map = lambda i: (i % per_seq, 0)
    row = lambda i: (i, 0)
    fixed = lambda i: (0, 0)
    tab_spec = pl.BlockSpec((tm, LANES), tab_map)
    widths = [(768, F32), (256, F32), (128, F32), (512, BF16), (512, F32), (512, F32),
              (512, BF16), (512, BF16), (1024, F32)]
    stacked = (4, 5)
    out_specs = [pl.BlockSpec((None, tm, 4, LANES), lambda i: (layer, i, 0, 0)) if n in stacked
                 else pl.BlockSpec((tm, w), row) for n, (w, _) in enumerate(widths)]
    out_shape = [jax.ShapeDtypeStruct((depth, t, 4, LANES) if n in stacked else (t, w), dt)
                 for n, (w, dt) in enumerate(widths)]
    in_specs = [pl.BlockSpec((tm, d), row), pl.BlockSpec((1, d), fixed),
                pl.BlockSpec((d, _PROJ_WIDTH), fixed), tab_spec, tab_spec, tab_spec]
    operands = [x2d, g_row, w_bf16, cos_t, sa_t, sb_t]
    aliases = {}
    if kv_stacked is not None:
        for out_idx, arr in zip(stacked, kv_stacked):
            aliases[len(operands)] = out_idx
            in_specs.append(pl.BlockSpec(memory_space=pl.ANY))
            operands.append(arr)
    return pl.pallas_call(
        functools.partial(_proj_kernel, n_alias=len(aliases)),
        grid=(nt,),
        in_specs=in_specs,
        out_specs=out_specs,
        out_shape=out_shape,
        input_output_aliases=aliases,
        compiler_params=pltpu.CompilerParams(dimension_semantics=("parallel",),
                                             vmem_limit_bytes=VMEM_LIMIT),
        name="input_projection",
    )(*operands)


def _delta_kernel(qkv_ref, z_ref, ab_ref, conv0_ref, s0_ref, convw_ref, alog_ref, dtb_ref, anorm_ref,
                  seg_ref, tril_ref, triu_ref, last_ref, expg_ref, expb_ref,
                  mix_ref, convn_ref, sn_ref, xp_ref, s_ref, o_ref, *, c, g, nl):
    l = pl.program_id(1)
    tl = c * g

    @pl.when(l == 0)
    def _():
        xp_ref[5:8, :] = conv0_ref[0]
        s_ref[...] = s0_ref[0]

    x = qkv_ref[0]
    xp_ref[8:8 + tl, :] = x
    w = convw_ref[...]
    y = xp_ref[5:5 + tl, :] * w[0:1]
    y = y + xp_ref[6:6 + tl, :] * w[1:2]
    y = y + xp_ref[7:7 + tl, :] * w[2:3]
    y = y + x * w[3:4]
    tail = xp_ref[5 + tl:8 + tl, :]
    xp_ref[5:8, :] = tail

    @pl.when(l == nl - 1)
    def _():
        convn_ref[0] = tail

    act = y * jax.nn.sigmoid(y)
    seg = seg_ref[...]

    def l2n(t):
        return t * lax.rsqrt(_head_sums(t * t, seg) + L2_EPS)

    q = l2n(act[:, 0:256]) * (HEAD ** -0.5)
    k = l2n(act[:, 256:512])
    v = act[:, 512:768]

    ab = ab_ref[0]
    gl = -jnp.exp(alog_ref[...]) * jax.nn.softplus(ab + dtb_ref[...])
    beta_w = _mm_exact_rhs(jax.nn.sigmoid(ab), expb_ref[...])
    g_w = _mm_exact_lhs(tril_ref[...], _mm_exact_rhs(gl, expg_ref[...]))
    g_rows = _mm_exact_rhs(gl, triu_ref[...], _TN)
    g_last_w = _mm_exact_lhs(last_ref[...], g_w)
    eg_w = jnp.exp(g_w)
    v_beta = v * beta_w
    k_beta_eg = k * (beta_w * eg_w)
    q_dec = q * eg_w
    k_tail_w = k * jnp.exp(g_last_w - g_w)
    decay_end_w = jnp.exp(g_last_w)

    row = _iota((c, c), 0)
    col = _iota((c, c), 1)
    lower = row >= col
    strict = row > col

    probs = [(slice(ci * c, (ci + 1) * c), h) for ci in range(g) for h in range(4)]
    heads = lambda h: slice(h * HEAD, (h + 1) * HEAD)
    decay = [jnp.where(lower, jnp.exp(jnp.where(lower, g_w[rows, heads(h)][:, :c] - g_rows[h:h + 1, rows], 0.0)),
                       0.0) for rows, h in probs]
    kh = [k[rows, heads(h)] for rows, h in probs]
    kq = [_mm(jnp.concatenate([kk_, q[rows, heads(h)]], axis=0), kk_, _NT)
          for kk_, (rows, h) in zip(kh, probs)]
    t_inv = _unit_lower_inverses(
        [jnp.where(strict, beta_w[rows, heads(h)][:, :c] * x[:c] * d, 0.0)
         for x, d, (rows, h) in zip(kq, decay, probs)], c)
    sol = [_mm(t, jnp.concatenate([v_beta[rows, heads(h)], k_beta_eg[rows, heads(h)]], axis=1))
           for t, (rows, h) in zip(t_inv, probs)]
    qk = [jnp.where(lower, x[c:] * d, 0.0) for x, d in zip(kq, decay)]
    kt_sol = [_mm(k_tail_w[rows, heads(h)], so, _TN) for so, (rows, h) in zip(sol, probs)]
    qk_sol = [_mm(x, so) for x, so in zip(qk, sol)]
    q_eff = [q_dec[rows, heads(h)] - x[:, HEAD:] for x, (rows, h) in zip(qk_sol, probs)]
    decay_end = [decay_end_w[rows.stop - 1:rows.stop, heads(h)] for rows, h in probs]

    states = [s_ref[h] for h in range(4)]
    for ci in range(g):
        rows = slice(ci * c, (ci + 1) * c)
        idx = [ci * 4 + h for h in range(4)]
        o_new = [_mm(q_eff[n], states[h]) + qk_sol[n][:, :HEAD] for h, n in enumerate(idx)]
        states = [states[h] * decay_end[n] + kt_sol[n][:, :HEAD] - _mm(kt_sol[n][:, HEAD:], states[h])
                  for h, n in enumerate(idx)]
        for h in range(4):
            o_ref[rows, heads(h)] = o_new[h]
    for h in range(4):
        s_ref[h] = states[h]

    o = o_ref[...]
    ms = _head_sums(o * o, seg) * (1.0 / HEAD)
    z = z_ref[0]
    ao = (o * lax.rsqrt(ms + NORM_EPS) * anorm_ref[...]) * (z * jax.nn.sigmoid(z))
    mix_ref[0] = ao.astype(BF16)

    @pl.when(l == nl - 1)
    def _():
        sn_ref[0] = s_ref[...]


def _gated_delta(a_qkv, a_z, a_ab, conv0, s0, conv_w, alog_row, dtb_row, anorm_row):
    b, seq, _ = a_qkv.shape
    c, g = _chunking(seq)
    tl = c * g
    nl = seq // tl
    tile = lambda i, l: (i, l, 0)
    per_b3 = lambda i, l: (i, 0, 0)
    per_b4 = lambda i, l: (i, 0, 0, 0)
    fixed = lambda i, l: (0, 0)
    sel = _selector_matrices(tl, c)
    consts = [sel[name] for name in ("seg", "tril", "triu", "last", "expg", "expb")]
    return pl.pallas_call(
        functools.partial(_delta_kernel, c=c, g=g, nl=nl),
        grid=(b, nl),
        in_specs=[pl.BlockSpec((1, tl, 768), tile), pl.BlockSpec((1, tl, 256), tile),
                  pl.BlockSpec((1, tl, LANES), tile), pl.BlockSpec((1, 3, 768), per_b3),
                  pl.BlockSpec((1, 4, HEAD, HEAD), per_b4), pl.BlockSpec((4, 768), fixed),
                  pl.BlockSpec((1, LANES), fixed), pl.BlockSpec((1, LANES), fixed),
                  pl.BlockSpec((1, 256), fixed)] + [pl.BlockSpec(m.shape, fixed) for m in consts],
        out_specs=[pl.BlockSpec((1, tl, 256), tile), pl.BlockSpec((1, 3, 768), per_b3),
                   pl.BlockSpec((1, 4, HEAD, HEAD), per_b4)],
        out_shape=[jax.ShapeDtypeStruct((b, seq, 256), BF16),
                   jax.ShapeDtypeStruct((b, 3, 768), F32),
                   jax.ShapeDtypeStruct((b, 4, HEAD, HEAD), F32)],
        scratch_shapes=[pltpu.VMEM((tl + 8, 768), F32), pltpu.VMEM((4, HEAD, HEAD), F32),
                        pltpu.VMEM((tl, 256), F32)],
        compiler_params=pltpu.CompilerParams(dimension_semantics=("parallel", "arbitrary"),
                                             vmem_limit_bytes=VMEM_LIMIT),
        name="gated_delta",
    )(a_qkv, a_z, a_ab, conv0, s0, conv_w, alog_row, dtb_row, anorm_row, *consts)


def _rwkv_kernel(c_ref, shift0_ref, s0_ref, mu_ref, w0_ref, wup_ref, a0_ref, aup_ref, gup_ref,
                 kk_ref, ka_ref, rk_ref, lnw_ref, lnb_ref, seg_ref, tril_ref,
                 mix_ref, shiftn_ref, sn_ref, xp_ref, s_ref, y_ref, *, c, g, nl):
    l = pl.program_id(1)
    tl = c * g

    @pl.when(l == 0)
    def _():
        xp_ref[7:8, :] = shift0_ref[0]
        s_ref[...] = s0_ref[0]

    raw = c_ref[0]
    xp_ref[8:8 + tl, :] = raw
    prev = xp_ref[7:7 + tl, :]
    last = raw[tl - 1:tl, :]
    xp_ref[7:8, :] = last

    @pl.when(l == nl - 1)
    def _():
        shiftn_ref[0] = last

    x = raw + (prev - raw) * mu_ref[...]
    cr, ck, cv = x[:, 0:256], x[:, 256:512], x[:, 512:768]
    c_wa = x[:, 768:896]
    c_g = x[:, 896:1024]
    w_log = -jnp.exp(-jax.nn.softplus(-(w0_ref[...] + _mm(jnp.tanh(c_wa), wup_ref[...])))
                     - RWKV_DECAY_OFFSET)
    ca = jax.nn.sigmoid(a0_ref[...] + _mm(c_wa, aup_ref[...]))
    cg = _mm(jax.nn.sigmoid(c_g), gup_ref[...])
    seg = seg_ref[...]
    kkv = ck * kk_ref[...]
    kk = kkv * lax.rsqrt(_head_sums(kkv * kkv, seg) + L2_EPS)
    ck = ck * (1.0 + (ca - 1.0) * ka_ref[...])

    g_cum = _mm_exact_lhs(tril_ref[...], w_log)
    e_pos = jnp.exp(g_cum)
    e_neg = jnp.exp(-g_cum)
    a_t = -kk * jnp.exp(g_cum - w_log)
    b_t = (kk * ca) * e_neg
    k_t = ck * e_neg
    r_t = cr * e_pos

    row = _iota((c, c), 0)
    col = _iota((c, c), 1)
    lower = row >= col
    strict = row > col

    probs = [(slice(ci * c, (ci + 1) * c), h) for ci in range(g) for h in range(4)]
    heads = lambda h: slice(h * HEAD, (h + 1) * HEAD)
    bh = [b_t[rows, heads(h)] for rows, h in probs]
    kh = [k_t[rows, heads(h)] for rows, h in probs]
    vh = [cv[rows, heads(h)] for rows, h in probs]
    ar = [jnp.concatenate([a_t[rows, heads(h)], r_t[rows, heads(h)]], axis=0) for rows, h in probs]
    pb = [_mm(x, y_, _NT) for x, y_ in zip(ar, bh)]
    pk = [_mm(x, y_, _NT) for x, y_ in zip(ar, kh)]
    t_inv = _unit_lower_inverses([-jnp.where(strict, x[:c], 0.0) for x in pb], c)
    n_rb = [jnp.where(lower, x[c:], 0.0) for x in pb]
    mn = [_mm(jnp.concatenate([jnp.where(strict, x[:c], 0.0), jnp.where(lower, x[c:], 0.0)], axis=0), vv)
          for x, vv in zip(pk, vh)]
    bk = [jnp.concatenate([x, y_], axis=0) for x, y_ in zip(bh, kh)]
    g_end = [e_pos[rows.stop - 1:rows.stop, heads(h)] for rows, h in probs]
    ta = [_mm(t, jnp.concatenate([x[:c], m_[:c]], axis=1)) for t, x, m_ in zip(t_inv, ar, mn)]
    p_mat = [_mm(x[:, :HEAD], y_, _TN) for x, y_ in zip(ta, bh)]
    c_mat = [_mm(jnp.concatenate([x[:, HEAD:], vv], axis=0), y_, _TN) for x, vv, y_ in zip(ta, vh, bk)]
    nr = [_mm(x, y_) for x, y_ in zip(n_rb, ta)]
    r_eff = [x[c:] + y_[:, :HEAD] for x, y_ in zip(ar, nr)]
    y_off = [x[:, HEAD:] + m_[c:] for x, m_ in zip(nr, mn)]

    states = [s_ref[h] for h in range(4)]
    for ci in range(g):
        rows = slice(ci * c, (ci + 1) * c)
        idx = [ci * 4 + h for h in range(4)]
        y_new = [_mm(r_eff[n], states[h], _NT) + y_off[n] for h, n in enumerate(idx)]
        states = [(states[h] + _mm(states[h], p_mat[n]) + c_mat[n]) * g_end[n] for h, n in enumerate(idx)]
        for h in range(4):
            y_ref[rows, heads(h)] = y_new[h]
    for h in range(4):
        s_ref[h] = states[h]

    y = y_ref[...]
    mean = _head_sums(y, seg) * (1.0 / HEAD)
    yc = y - mean
    var = _head_sums(yc * yc, seg) * (1.0 / HEAD)
    cy = (yc * lax.rsqrt(var + C_LN_EPS)) * lnw_ref[...] + lnb_ref[...]
    bonus = _head_sums(cr * ck * rk_ref[...], seg) * cv
    mix_ref[0] = ((cy + bonus) * cg).astype(BF16)

    @pl.when(l == nl - 1)
    def _():
        sn_ref[0] = s_ref[...]


def _rwkv7(c_raw, shift0, s0, params):
    b, seq, width = c_raw.shape
    c, g = _chunking(seq)
    tl = c * g
    nl = seq // tl
    tile = lambda i, l: (i, l, 0)
    per_b3 = lambda i, l: (i, 0, 0)
    per_b4 = lambda i, l: (i, 0, 0, 0)
    fixed = lambda i, l: (0, 0)
    sel = _selector_matrices(tl, c)
    params = tuple(params) + (sel["seg"], sel["tril"])
    param_specs = [pl.BlockSpec(p.shape, fixed) for p in params]
    return pl.pallas_call(
        functools.partial(_rwkv_kernel, c=c, g=g, nl=nl),
        grid=(b, nl),
        in_specs=[pl.BlockSpec((1, tl, width), tile), pl.BlockSpec((1, 1, width), per_b3),
                  pl.BlockSpec((1, 4, HEAD, HEAD), per_b4)] + param_specs,
        out_specs=[pl.BlockSpec((1, tl, 256), tile), pl.BlockSpec((1, 1, width), per_b3),
                   pl.BlockSpec((1, 4, HEAD, HEAD), per_b4)],
        out_shape=[jax.ShapeDtypeStruct((b, seq, 256), BF16),
                   jax.ShapeDtypeStruct((b, 1, width), F32),
                   jax.ShapeDtypeStruct((b, 4, HEAD, HEAD), F32)],
        scratch_shapes=[pltpu.VMEM((tl + 8, width), F32), pltpu.VMEM((4, HEAD, HEAD), F32),
                        pltpu.VMEM((tl, 256), F32)],
        compiler_params=pltpu.CompilerParams(dimension_semantics=("parallel", "arbitrary"),
                                             vmem_limit_bytes=VMEM_LIMIT),
        name="rwkv7",
    )(c_raw, shift0, s0, *params)


def _attn_kernel(i_tab, j_tab, flag_tab, q_ref, k_ref, v_ref, lam_ref, bn_ref, o_ref,
                 m_ref, acc_ref, *, tq, tk, past, lk_true, lam_init, half_tiles):
    p = pl.program_id(1)
    i = i_tab[p]
    j = j_tab[p]

    @pl.when(j == 0)
    def _():
        m_ref[...] = jnp.full(m_ref.shape, MASK_VALUE, F32)
        acc_ref[...] = jnp.zeros(acc_ref.shape, F32)

    def step(masked, nkeys):
        nslab = nkeys // LANES
        if masked:
            q_pos = past + i * tq + _iota((tq, nkeys), 0)
            k_pos = j * tk + _iota((tq, nkeys), 1)
            visible = (jnp.right_shift(k_pos, 6) <= jnp.right_shift(q_pos, 6)) & (k_pos < lk_true)
        q = q_ref[0]
        k = k_ref[0, 0:nkeys, :]
        v = v_ref[0, 0:nkeys, :]

        def scores(hm):
            sl = slice(hm * HEAD, (hm + 1) * HEAD)
            return lax.dot_general(q[:, sl], k[:, sl], _NT, preferred_element_type=F32)

        ones = jnp.ones((nkeys, LANES), BF16)
        s_next = scores(0)
        for hm in range(8):
            h = hm // 2
            s = s_next
            if hm + 1 < 8:
                s_next = scores(hm + 1)
            if masked:
                s = jnp.where(visible, s, MASK_VALUE)
            slabs = [s[:, n * LANES:(n + 1) * LANES] for n in range(nslab)]
            m_cur = functools.reduce(jnp.maximum, slabs)
            m_prev = m_ref[hm]
            m_new = jnp.maximum(m_prev, jnp.max(m_cur, axis=1, keepdims=True))
            alpha = jnp.exp2(m_prev - m_new)
            e = jnp.concatenate([jnp.exp2(sb - m_new).astype(BF16) for sb in slabs], axis=1)
            v_ones = jnp.concatenate([v[:, h * LANES:(h + 1) * LANES], ones], axis=1)
            acc_ref[hm] = (jnp.concatenate([alpha, alpha], axis=1) * acc_ref[hm]
                           + jnp.dot(e, v_ones, preferred_element_type=F32))
            m_ref[hm] = m_new

    kind = jnp.right_shift(flag_tab[p], 1)

    @pl.when(kind == 0)
    def _():
        step(False, tk)

    @pl.when(kind == 1)
    def _():
        step(True, tk)

    if half_tiles:
        @pl.when(kind == 2)
        def _():
            step(True, tk // 2)

    @pl.when(jnp.bitwise_and(flag_tab[p], 1) != 0)
    def _():
        lp = lam_ref[...]
        lam = (jnp.exp(jnp.sum(lp[0:1] * lp[1:2], axis=1, keepdims=True))
               - jnp.exp(jnp.sum(lp[2:3] * lp[3:4], axis=1, keepdims=True)) + lam_init)
        for h in range(4):
            a1 = acc_ref[2 * h]
            a2 = acc_ref[2 * h + 1]
            o = a1[:, :LANES] / a1[:, LANES:] - lam * (a2[:, :LANES] / a2[:, LANES:])
            o = _rmsnorm_rows(o, bn_ref[...]) * (1.0 - lam_init)
            o_ref[0, :, h * LANES:(h + 1) * LANES] = o.astype(BF16)


def _diff_attention(q, k, v, lam_params, bnorm_row, past, lam_init, tq, tk):
    b, lq, width = q.shape
    lk_true = k.shape[1]
    nk = -(-lk_true // tk)
    if nk * tk != lk_true:
        pad = ((0, 0), (0, nk * tk - lk_true), (0, 0))
        k = jnp.pad(k, pad)
        v = jnp.pad(v, pad)
    nq = lq // tq
    half_tiles = (tk // 2) % LANES == 0
    i_list, j_list, flag_list = [], [], []
    for i in range(nq):
        first_pos = past + i * tq
        last_pos = first_pos + tq - 1
        visible_end = min((last_pos // CHUNK + 1) * CHUNK, lk_true)
        j_max = (visible_end - 1) // tk
        for j in range(j_max + 1):
            key_end = (j + 1) * tk
            if (key_end - 1) // CHUNK <= first_pos // CHUNK and key_end <= lk_true:
                kind = 0
            elif half_tiles and visible_end - j * tk <= tk // 2:
                kind = 2
            else:
                kind = 1
            i_list.append(i)
            j_list.append(j)
            flag_list.append((1 if j == j_max else 0) + 2 * kind)
    tabs = [jnp.asarray(np.asarray(t, np.int32)) for t in (i_list, j_list, flag_list)]
    grid_spec = pltpu.PrefetchScalarGridSpec(
        num_scalar_prefetch=3,
        grid=(b, len(i_list)),
        in_specs=[pl.BlockSpec((1, tq, width), lambda bi, p, it, jt, lt: (bi, it[p], 0)),
                  pl.BlockSpec((1, tk, width), lambda bi, p, it, jt, lt: (bi, jt[p], 0)),
                  pl.BlockSpec((1, tk, width), lambda bi, p, it, jt, lt: (bi, jt[p], 0)),
                  pl.BlockSpec((4, HEAD), lambda bi, p, it, jt, lt: (0, 0)),
                  pl.BlockSpec((1, LANES), lambda bi, p, it, jt, lt: (0, 0))],
        out_specs=pl.BlockSpec((1, tq, width), lambda bi, p, it, jt, lt: (bi, it[p], 0)),
        scratch_shapes=[pltpu.VMEM((8, tq, LANES), F32), pltpu.VMEM((8, tq, 2 * LANES), F32)],
    )
    return pl.pallas_call(
        functools.partial(_attn_kernel, tq=tq, tk=tk, past=past, lk_true=lk_true, lam_init=lam_init,
                          half_tiles=half_tiles),
        grid_spec=grid_spec,
        out_shape=jax.ShapeDtypeStruct((b, lq, width), BF16),
        compiler_params=pltpu.CompilerParams(dimension_semantics=("parallel", "arbitrary"),
                                             vmem_limit_bytes=VMEM_LIMIT),
        name="diff_attention",
    )(*tabs, q, k, v, lam_params, bnorm_row)


def _mlp_kernel(h_ref, ma_ref, mb_ref, mc_ref, wo_ref, nf_ref, w1_ref, w2_ref, np_ref, wg_ref,
                wp_ref, p_ref, nfin_ref, o_ref, *, tf, final_norm):
    mixed = (jnp.dot(ma_ref[...], wo_ref[0:256, :], preferred_element_type=F32)
             + jnp.dot(mb_ref[...], wo_ref[256:768, :], preferred_element_type=F32)
             + jnp.dot(mc_ref[...], wo_ref[768:1024, :], preferred_element_type=F32))
    h1 = h_ref[...] + mixed
    xn = _rmsnorm_rows(h1, nf_ref[...]).astype(BF16)
    h2 = h1
    for f0 in range(0, w1_ref.shape[1], tf):
        u = jnp.maximum(jnp.dot(xn, w1_ref[:, f0:f0 + tf], preferred_element_type=F32), 0.0)
        h2 = h2 + jnp.dot((u * u).astype(BF16), w2_ref[f0:f0 + tf, :], preferred_element_type=F32)
    gate = jax.nn.sigmoid(jnp.dot(_rmsnorm_rows(h2, np_ref[...]).astype(BF16), wg_ref[...],
                                  preferred_element_type=F32))
    h3 = h2 + gate * jnp.dot(p_ref[...].astype(BF16), wp_ref[...], preferred_element_type=F32)
    if final_norm:
        h3 = _rmsnorm_rows(h3, nfin_ref[...])
    o_ref[...] = h3


def _mix_mlp(h2d, mix_a, mix_b, mix_c, w_out, norm_ffn, w_ff1, w_ff2, norm_ple, w_gate, w_proj,
             p2d, norm_final, final_norm, tm, tf):
    t, d = h2d.shape
    dff = w_ff1.shape[1]
    row = lambda i: (i, 0)
    fixed = lambda i: (0, 0)
    resident = lambda shape: pl.BlockSpec(shape, fixed, pipeline_mode=pl.Buffered(1))
    return pl.pallas_call(
        functools.partial(_mlp_kernel, tf=tf, final_norm=final_norm),
        grid=(t // tm,),
        in_specs=[pl.BlockSpec((tm, d), row), pl.BlockSpec((tm, 256), row),
                  pl.BlockSpec((tm, 512), row), pl.BlockSpec((tm, 256), row),
                  resident((d, d)), resident((1, d)), resident((d, dff)), resident((dff, d)),
                  resident((1, d)), resident((d, d)), resident((p2d.shape[1], d)),
                  pl.BlockSpec((tm, p2d.shape[1]), row), resident((1, d))],
        out_specs=pl.BlockSpec((tm, d), row),
        out_shape=jax.ShapeDtypeStruct((t, d), F32),
        compiler_params=pltpu.CompilerParams(dimension_semantics=("parallel",),
                                             vmem_limit_bytes=VMEM_LIMIT),
        name="mix_mlp",
    )(h2d, mix_a, mix_b, mix_c, w_out, norm_ffn, w_ff1, w_ff2, norm_ple, w_gate, w_proj, p2d,
      norm_final)


def _rope_tables(past, seq_len):
    half = HEAD // 2
    inv = ROPE_THETA ** (-2.0 * jnp.arange(half, dtype=F32) / HEAD)
    pos = past + jnp.arange(seq_len, dtype=jnp.int32)
    ang = pos.astype(F32)[:, None] * inv[None, :]
    cos, sin, zero = jnp.cos(ang), jnp.sin(ang), jnp.zeros_like(ang)
    rep = LANES // HEAD
    cos_t = jnp.tile(jnp.concatenate([cos, cos], axis=1), (1, rep))
    sa_t = jnp.tile(jnp.concatenate([-sin, zero], axis=1), (1, rep))
    sb_t = jnp.tile(jnp.concatenate([zero, sin], axis=1), (1, rep))
    return cos_t, sa_t, sb_t


def _pad_lanes(v, width=LANES):
    return jnp.pad(v, (0, width - v.shape[0]))[None, :]


def _prep_layer(i, norm_mix, w_in, a_conv_w, a_A_log, a_dt_bias, a_norm,
                b_lam_q1, b_lam_k1, b_lam_q2, b_lam_k2, b_norm,
                c_mu, c_w0, c_w_up, c_a0, c_a_up, c_g_up, c_k_k, c_k_a, c_r_k, c_ln_w, c_ln_b,
                w_out, norm_ffn, w_ff1, w_ff2, norm_ple, w_ple_gate, w_ple_proj):
    w = w_in[i]
    d = w.shape[0]
    w_perm = jnp.concatenate(
        [w[:, 0:1024], w[:, 1032:3592], w[:, 1024:1032],
         jnp.zeros((d, _PROJ_WIDTH - 3592), w.dtype)], axis=1).astype(BF16)
    zeros_rank = jnp.zeros_like(c_w_up[i])
    rwkv = (c_mu[i][None, :], c_w0[i][None, :],
            jnp.concatenate([c_w_up[i], zeros_rank], axis=0).astype(BF16), c_a0[i][None, :],
            jnp.concatenate([jnp.zeros_like(c_a_up[i]), c_a_up[i]], axis=0).astype(BF16),
            c_g_up[i].astype(BF16), c_k_k[i][None, :], c_k_a[i][None, :],
            c_r_k[i].reshape(1, -1), c_ln_w[i][None, :], c_ln_b[i][None, :])
    return dict(
        norm_mix=norm_mix[i][None, :], w_in=w_perm, conv_w=a_conv_w[i],
        alog=_pad_lanes(a_A_log[i]), dtb=_pad_lanes(a_dt_bias[i]),
        anorm=jnp.tile(a_norm[i], 4)[None, :],
        lam=jnp.stack([b_lam_q1[i], b_lam_k1[i], b_lam_q2[i], b_lam_k2[i]], axis=0),
        bnorm=b_norm[i][None, :], rwkv=rwkv,
        w_out=w_out[i].astype(BF16), norm_ffn=norm_ffn[i][None, :],
        w_ff1=w_ff1[i].astype(BF16), w_ff2=w_ff2[i].astype(BF16),
        norm_ple=norm_ple[i][None, :], w_gate=w_ple_gate[i].astype(BF16),
        w_proj=w_ple_proj[i].astype(BF16))


def _pick_tile(n, target):
    t = min(n, target)
    while n % t:
        t //= 2
    return t


def _trunk(x, p, cache_k, cache_v, conv_buf, delta_s, shift_prev, wkv_s, layers, norm_final):
    b, seq, d = x.shape
    depth = len(layers)
    past = cache_k.shape[2]
    t = b * seq
    tm = _pick_tile(t, PROJ_TM)
    rope_tabs = _rope_tables(past, seq)
    h = x.reshape(t, d)
    states = []
    kv_stacked = None
    for i, lp in enumerate(layers):
        a_qkv, a_z, a_ab, q_b, k_f, v_f, k_b, v_b, c_raw = _input_projection(
            h, lp["norm_mix"], lp["w_in"], rope_tabs, seq, tm, i, depth, kv_stacked)
        kv_stacked = (k_f, v_f)
        mix_a, conv_n, delta_n = _gated_delta(
            a_qkv.reshape(b, seq, -1), a_z.reshape(b, seq, -1), a_ab.reshape(b, seq, -1),
            conv_buf[i], delta_s[i], lp["conv_w"], lp["alog"], lp["dtb"], lp["anorm"])
        k_all = k_b.reshape(b, seq, -1)
        v_all = v_b.reshape(b, seq, -1)
        if past:
            k_all = jnp.concatenate([cache_k[i].reshape(b, past, -1).astype(BF16), k_all], axis=1)
            v_all = jnp.concatenate([cache_v[i].reshape(b, past, -1).astype(BF16), v_all], axis=1)
        lam_init = 0.8 - 0.6 * math.exp(-0.3 * i)
        mix_b = _diff_attention(q_b.reshape(b, seq, -1), k_all, v_all, lp["lam"], lp["bnorm"],
                                past, lam_init, _pick_tile(seq, ATTN_TQ),
                                ATTN_TK if past + seq >= 8 * ATTN_TK else ATTN_TK // 2)
        mix_c, shift_n, wkv_n = _rwkv7(c_raw.reshape(b, seq, -1), shift_prev[i][:, None, :],
                                       wkv_s[i], lp["rwkv"])
        h = _mix_mlp(h, mix_a.reshape(t, -1), mix_b.reshape(t, -1), mix_c.reshape(t, -1),
                     lp["w_out"], lp["norm_ffn"], lp["w_ff1"], lp["w_ff2"], lp["norm_ple"],
                     lp["w_gate"], lp["w_proj"], p[i].reshape(t, -1), norm_final[None, :],
                     i == depth - 1, _pick_tile(t, MLP_TM), MLP_TF)
        states.append((conv_n, delta_n, shift_n[:, 0, :], wkv_n))
    conv_all, delta_all, shift_all, wkv_all = (
        jnp.stack([st[j] for st in states], axis=0) for j in range(4))
    k_all_layers, v_all_layers = (a.reshape(depth, b, seq, 4, LANES) for a in kv_stacked)
    return h.reshape(b, seq, d), [conv_all, delta_all, k_all_layers, v_all_layers, shift_all, wkv_all]


def kernel(x_prompt, x_sample, cache_b_k, cache_b_v, state_a_conv, state_a_delta, state_c_shift, state_c_wkv, p_prompt, p_sample, norm_mix, w_in, a_conv_w, a_A_log, a_dt_bias, a_norm, b_lam_q1, b_lam_k1, b_lam_q2, b_lam_k2, b_norm, c_mu, c_w0, c_w_up, c_a0, c_a_up, c_g_up, c_k_k, c_k_a, c_r_k, c_ln_w, c_ln_b, w_out, norm_ffn, w_ff1, w_ff2, norm_ple, w_ple_gate, w_ple_proj, norm_final):
    depth = w_in.shape[0]
    layers = [_prep_layer(i, norm_mix, w_in, a_conv_w, a_A_log, a_dt_bias, a_norm,
                          b_lam_q1, b_lam_k1, b_lam_q2, b_lam_k2, b_norm,
                          c_mu, c_w0, c_w_up, c_a0, c_a_up, c_g_up, c_k_k, c_k_a, c_r_k, c_ln_w, c_ln_b,
                          w_out, norm_ffn, w_ff1, w_ff2, norm_ple, w_ple_gate, w_ple_proj)
              for i in range(depth)]
    bp = x_prompt.shape[0]
    dt = x_prompt.dtype
    zeros = lambda ref: jnp.zeros((depth, bp) + ref.shape[2:], dt)
    empty_k = jnp.zeros((depth, bp, 0) + cache_b_k.shape[3:], dt)
    empty_v = jnp.zeros((depth, bp, 0) + cache_b_v.shape[3:], dt)
    y_prompt, st_p = _trunk(x_prompt, p_prompt, empty_k, empty_v, zeros(state_a_conv),
                            zeros(state_a_delta), zeros(state_c_shift), zeros(state_c_wkv),
                            layers, norm_final)
    y_sample, st_s = _trunk(x_sample, p_sample, cache_b_k, cache_b_v, state_a_conv, state_a_delta,
                            state_c_shift, state_c_wkv, layers, norm_final)
    return (y_prompt, y_sample, *st_p, *st_s)
```

```python
import functools
import math

import numpy as np
import jax
import jax.numpy as jnp
from jax import lax
from jax.experimental import pallas as pl
from jax.experimental.pallas import tpu as pltpu

F32 = jnp.float32
BF16 = jnp.bfloat16

CHUNK = 64
ROPE_THETA = 10000.0
NORM_EPS = 1e-6
L2_EPS = 1e-6
C_LN_EPS = 64e-5
RWKV_DECAY_OFFSET = 0.5
MASK_VALUE = float(np.finfo(np.float32).min)

LANES = 128
HEAD = 64
VMEM_LIMIT = 56 * 1024 * 1024
PROJ_TM = 512
MLP_TM = 512
MLP_TF = 1024
CHUNKS_PER_STEP = 8
ATTN_TQ = 512
ATTN_TK = 1024
ATTN_Q_SCALE = (64 ** -0.5) * math.log2(math.e)


def _iota(shape, axis):
    return lax.broadcasted_iota(jnp.int32, shape, axis)


_NN = (((1,), (0,)), ((), ()))
_NT = (((1,), (1,)), ((), ()))
_TN = (((0,), (0,)), ((), ()))


def _mm(a, b, dims=_NN):
    return lax.dot_general(a.astype(BF16), b.astype(BF16), dims, preferred_element_type=F32)


def _split3(x):
    h = x.astype(BF16)
    r = x - h.astype(F32)
    m = r.astype(BF16)
    lo = (r - m.astype(F32)).astype(BF16)
    return h, m, lo


def _mm_exact_rhs(x, ones_bf16, dims=_NN):
    h, m, lo = _split3(x)
    d = functools.partial(lax.dot_general, dimension_numbers=dims, preferred_element_type=F32)
    return d(h, ones_bf16) + d(m, ones_bf16) + d(lo, ones_bf16)


def _head_sums(x, seg_bf16):
    h = x.astype(BF16)
    lo = (x - h.astype(F32)).astype(BF16)
    return (jnp.dot(h, seg_bf16, preferred_element_type=F32)
            + jnp.dot(lo, seg_bf16, preferred_element_type=F32))


def _mm_exact_lhs(ones_bf16, x, dims=_NN):
    h, m, lo = _split3(x)
    d = functools.partial(lax.dot_general, dimension_numbers=dims, preferred_element_type=F32)
    return d(ones_bf16, h) + d(ones_bf16, m) + d(ones_bf16, lo)


def _selector_matrices(tl, c):
    tok = np.arange(tl)
    same = (tok[:, None] // c) == (tok[None, :] // c)
    lane = np.arange(4 * HEAD)
    src = np.arange(LANES)
    mats = dict(
        seg=(lane[:, None] // HEAD) == (lane[None, :] // HEAD),
        tril=same & (tok[:, None] >= tok[None, :]),
        triu=same & (tok[:, None] <= tok[None, :]),
        last=tok[None, :] == (tok[:, None] // c) * c + c - 1,
        expg=src[:, None] == lane[None, :] // HEAD,
        expb=src[:, None] == lane[None, :] // HEAD + 4)
    return {name: jnp.asarray(m.astype(np.float32), BF16) for name, m in mats.items()}


def _chunking(seq):
    c = min(CHUNK, seq)
    assert c & (c - 1) == 0 and seq % c == 0, seq
    g = min(CHUNKS_PER_STEP, seq // c)
    while (seq // c) % g:
        g -= 1
    return c, g


def _rmsnorm_rows(x, g):
    return x * lax.rsqrt(jnp.mean(x * x, axis=-1, keepdims=True) + NORM_EPS) * g


def _unit_lower_inverses(a_list, c):
    row = _iota((c, c), 0)
    col = _iota((c, c), 1)

    def same_block(shift):
        return jnp.right_shift(row, shift) == jnp.right_shift(col, shift)

    eye = (row == col).astype(F32)
    leaf = same_block(3)
    ns = [jnp.where(leaf, -a, 0.0) for a in a_list]
    ts = [eye + n for n in ns]
    n2s = [_mm(n, n) for n in ns]
    ts = [t + _mm(t, n2) for t, n2 in zip(ts, n2s)]
    n4s = [_mm(n2, n2) for n2 in n2s]
    ts = [t + _mm(t, n4) for t, n4 in zip(ts, n4s)]
    shift = 3
    while (1 << shift) < c:
        off_mask = same_block(shift + 1) & jnp.logical_not(same_block(shift))
        tos = [_mm(t, jnp.where(off_mask, a, 0.0)) for t, a in zip(ts, a_list)]
        ts = [t - _mm(to, t) for t, to in zip(ts, tos)]
        shift += 1
    return ts


_PROJ_A_QKV = 0
_PROJ_A_Z = 768
_PROJ_B_Q = 1024
_PROJ_B_K = 1536
_PROJ_B_V = 2048
_PROJ_C = 2560
_PROJ_A_AB = 3584
_PROJ_WIDTH = 3712


def _proj_kernel(*refs, n_alias, tiles_per_seq):
    x_ref, g_ref, w_ref, cos_ref, sa_ref, sb_ref = refs[:6]
    n_in = 6 + (2 if tiles_per_seq else 0)
    aqkv_ref, az_ref, aab_ref, q_ref, k_ref, v_ref, kb_ref, vb_ref, c_ref = refs[n_in + n_alias:n_in + n_alias + 9]
    xn = _rmsnorm_rows(x_ref[...], g_ref[...]).astype(BF16)

    def proj(c0, c1):
        return jnp.dot(xn, w_ref[:, c0:c1], preferred_element_type=F32)

    if tiles_per_seq:
        conv0_ref, convw_ref = refs[6:8]
        convn_ref, xp_ref = refs[n_in + n_alias + 9:]
        tm = x_ref.shape[0]

        @pl.when(pl.program_id(0) % tiles_per_seq == 0)
        def _():
            xp_ref[5:8, :] = conv0_ref[0]

        for c0 in range(_PROJ_A_QKV, _PROJ_A_Z, 2 * LANES):
            cols = slice(c0, c0 + 2 * LANES)
            raw = proj(c0, c0 + 2 * LANES)
            xp_ref[8:8 + tm, cols] = raw
            y = xp_ref[5:5 + tm, cols] * convw_ref[0:1, cols]
            y = y + xp_ref[6:6 + tm, cols] * convw_ref[1:2, cols]
            y = y + xp_ref[7:7 + tm, cols] * convw_ref[2:3, cols]
            y = y + raw * convw_ref[3:4, cols]
            tail = xp_ref[5 + tm:8 + tm, cols]
            xp_ref[5:8, cols] = tail
            convn_ref[0, :, cols] = tail
            aqkv_ref[:, cols] = y * jax.nn.sigmoid(y)
        az_ref[...] = proj(_PROJ_A_Z, _PROJ_B_Q)
    else:
        a_all = proj(_PROJ_A_QKV, _PROJ_B_Q)
        aqkv_ref[...] = a_all[:, :_PROJ_A_Z]
        az_ref[...] = a_all[:, _PROJ_A_Z:]
    cos = cos_ref[...]
    sa = sa_ref[...]
    sb = sb_ref[...]

    def rope(x):
        return x * cos + pltpu.roll(x, LANES - 32, 1) * sa + pltpu.roll(x, 32, 1) * sb

    q_all = proj(_PROJ_B_Q, _PROJ_B_K)
    k_all = proj(_PROJ_B_K, _PROJ_B_V)
    for h in range(4):
        lo, hi = h * LANES, (h + 1) * LANES
        q_ref[:, lo:hi] = (rope(q_all[:, lo:hi]) * ATTN_Q_SCALE).astype(BF16)
        k = rope(k_all[:, lo:hi])
        k_ref[:, h, :] = k
        kb_ref[:, lo:hi] = k.astype(BF16)
    v = proj(_PROJ_B_V, _PROJ_C)
    for h in range(4):
        v_ref[:, h, :] = v[:, h * LANES:(h + 1) * LANES]
    vb_ref[...] = v.astype(BF16)
    c_all = proj(_PROJ_C, _PROJ_WIDTH)
    c_ref[...] = c_all[:, :_PROJ_A_AB - _PROJ_C]
    aab_ref[...] = c_all[:, _PROJ_A_AB - _PROJ_C:]


def _input_projection(x2d, g_row, w_bf16, rope_tabs, seq_len, tm, layer, depth, kv_stacked, conv=None):
    t, d = x2d.shape
    nt = t // tm
    cos_t, sa_t, sb_t = rope_tabs
    if tm >= seq_len:
        reps = tm // seq_len
        cos_t, sa_t, sb_t = (jnp.tile(a, (reps, 1)) for a in (cos_t, sa_t, sb_t))
        tab_map = lambda i: (0, 0)
    else:
        per_seq = seq_len // tm
        tab_map = lambda i: (i % per_seq, 0)
    row = lambda i: (i, 0)
    fixed = lambda i: (0, 0)
    tab_spec = pl.BlockSpec((tm, LANES), tab_map)
    widths = [(768, F32), (256, F32), (128, F32), (512, BF16), (512, F32), (512, F32),
              (512, BF16), (512, BF16), (1024, F32)]
    stacked = (4, 5)
    out_specs = [pl.BlockSpec((None, tm, 4, LANES), lambda i: (layer, i, 0, 0)) if n in stacked
                 else pl.BlockSpec((tm, w), row) for n, (w, _) in enumerate(widths)]
    out_shape = [jax.ShapeDtypeStruct((depth, t, 4, LANES) if n in stacked else (t, w), dt)
                 for n, (w, dt) in enumerate(widths)]
    in_specs = [pl.BlockSpec((tm, d), row), pl.BlockSpec((1, d), fixed),
                pl.BlockSpec((d, _PROJ_WIDTH), fixed), tab_spec, tab_spec, tab_spec]
    operands = [x2d, g_row, w_bf16, cos_t, sa_t, sb_t]
    tiles_per_seq, scratch = 0, []
    if conv is not None:
        assert tm < seq_len and seq_len % tm == 0
        tiles_per_seq = seq_len // tm
        conv0, conv_w = conv
        per_sequence = lambda i: (i // tiles_per_seq, 0, 0)
        in_specs += [pl.BlockSpec((1, 3, 768), per_sequence), pl.BlockSpec((4, 768), fixed)]
        operands += [conv0, conv_w]
        out_specs.append(pl.BlockSpec((1, 3, 768), per_sequence))
        out_shape.append(jax.ShapeDtypeStruct(conv0.shape, F32))
        scratch = [pltpu.VMEM((tm + 8, 768), F32)]
    aliases = {}
    if kv_stacked is not None:
        for out_idx, arr in zip(stacked, kv_stacked):
            aliases[len(operands)] = out_idx
            in_specs.append(pl.BlockSpec(memory_space=pl.ANY))
            operands.append(arr)
    return pl.pallas_call(
        functools.partial(_proj_kernel, n_alias=len(aliases), tiles_per_seq=tiles_per_seq),
        grid=(nt,),
        in_specs=in_specs,
        out_specs=out_specs,
        out_shape=out_shape,
        scratch_shapes=scratch,
        input_output_aliases=aliases,
        compiler_params=pltpu.CompilerParams(dimension_semantics=("arbitrary",),
                                             vmem_limit_bytes=VMEM_LIMIT),
        name="input_projection",
    )(*operands)


def _delta_kernel(qkv_ref, z_ref, ab_ref, conv0_ref, s0_ref, convw_ref, alog_ref, dtb_ref, anorm_ref,
                  seg_ref, tril_ref, triu_ref, last_ref, expg_ref, expb_ref,
                  mix_ref, convn_ref, sn_ref, xp_ref, s_ref, o_ref, *, c, g, nl, conv_done):
    l = pl.program_id(1)
    tl = c * g

    @pl.when(l == 0)
    def _():
        xp_ref[5:8, :] = conv0_ref[0]
        s_ref[...] = s0_ref[0]

    if conv_done:
        act = qkv_ref[0]
        tail = conv0_ref[0]
    else:
        x = qkv_ref[0]
        xp_ref[8:8 + tl, :] = x
        w = convw_ref[...]
        y = xp_ref[5:5 + tl, :] * w[0:1]
        y = y + xp_ref[6:6 + tl, :] * w[1:2]
        y = y + xp_ref[7:7 + tl, :] * w[2:3]
        y = y + x * w[3:4]
        tail = xp_ref[5 + tl:8 + tl, :]
        xp_ref[5:8, :] = tail
        act = y * jax.nn.sigmoid(y)

    @pl.when(l == nl - 1)
    def _():
        convn_ref[0] = tail

    seg = seg_ref[...]

    def l2n(t):
        return t * lax.rsqrt(_head_sums(t * t, seg) + L2_EPS)

    q = l2n(act[:, 0:256]) * (HEAD ** -0.5)
    k = l2n(act[:, 256:512])
    v = act[:, 512:768]

    ab = ab_ref[0]
    gl = -jnp.exp(alog_ref[...]) * jax.nn.softplus(ab + dtb_ref[...])
    beta_w = _mm_exact_rhs(jax.nn.sigmoid(ab), expb_ref[...])
    g_w = _mm_exact_lhs(tril_ref[...], _mm_exact_rhs(gl, expg_ref[...]))
    g_rows = _mm_exact_rhs(gl, triu_ref[...], _TN)
    g_last_w = _mm_exact_lhs(last_ref[...], g_w)
    eg_w = jnp.exp(g_w)
    v_beta = v * beta_w
    k_beta_eg = k * (beta_w * eg_w)
    q_dec = q * eg_w
    k_tail_w = k * jnp.exp(g_last_w - g_w)
    decay_end_w = jnp.exp(g_last_w)

    row = _iota((c, c), 0)
    col = _iota((c, c), 1)
    lower = row >= col
    strict = row > col

    probs = [(slice(ci * c, (ci + 1) * c), h) for ci in range(g) for h in range(4)]
    heads = lambda h: slice(h * HEAD, (h + 1) * HEAD)
    decay = [jnp.where(lower, jnp.exp(jnp.where(lower, g_w[rows, heads(h)][:, :c] - g_rows[h:h + 1, rows], 0.0)),
                       0.0) for rows, h in probs]
    kh = [k[rows, heads(h)] for rows, h in probs]
    kq = [_mm(jnp.concatenate([kk_, q[rows, heads(h)]], axis=0), kk_, _NT)
          for kk_, (rows, h) in zip(kh, probs)]
    t_inv = _unit_lower_inverses(
        [jnp.where(strict, beta_w[rows, heads(h)][:, :c] * x[:c] * d, 0.0)
         for x, d, (rows, h) in zip(kq, decay, probs)], c)
    sol = [_mm(t, jnp.concatenate([v_beta[rows, heads(h)], k_beta_eg[rows, heads(h)]], axis=1))
           for t, (rows, h) in zip(t_inv, probs)]
    qk = [jnp.where(lower, x[c:] * d, 0.0) for x, d in zip(kq, decay)]
    kt_sol = [_mm(k_tail_w[rows, heads(h)], so, _TN) for so, (rows, h) in zip(sol, probs)]
    qk_sol = [_mm(x, so) for x, so in zip(qk, sol)]
    q_eff = [q_dec[rows, heads(h)] - x[:, HEAD:] for x, (rows, h) in zip(qk_sol, probs)]
    decay_end = [decay_end_w[rows.stop - 1:rows.stop, heads(h)] for rows, h in probs]

    states = [s_ref[h] for h in range(4)]
    for ci in range(g):
        rows = slice(ci * c, (ci + 1) * c)
        idx = [ci * 4 + h for h in range(4)]
        o_new = [_mm(q_eff[n], states[h]) + qk_sol[n][:, :HEAD] for h, n in enumerate(idx)]
        states = [states[h] * decay_end[n] + kt_sol[n][:, :HEAD] - _mm(kt_sol[n][:, HEAD:], states[h])
                  for h, n in enumerate(idx)]
        for h in range(4):
            o_ref[rows, heads(h)] = o_new[h]
    for h in range(4):
        s_ref[h] = states[h]

    o = o_ref[...]
    ms = _head_sums(o * o, seg) * (1.0 / HEAD)
    z = z_ref[0]
    ao = (o * lax.rsqrt(ms + NORM_EPS) * anorm_ref[...]) * (z * jax.nn.sigmoid(z))
    mix_ref[0] = ao.astype(BF16)

    @pl.when(l == nl - 1)
    def _():
        sn_ref[0] = s_ref[...]


def _gated_delta(a_qkv, a_z, a_ab, conv0, s0, conv_w, alog_row, dtb_row, anorm_row, conv_done=False):
    b, seq, _ = a_qkv.shape
    c, g = _chunking(seq)
    tl = c * g
    nl = seq // tl
    tile = lambda i, l: (i, l, 0)
    per_b3 = lambda i, l: (i, 0, 0)
    per_b4 = lambda i, l: (i, 0, 0, 0)
    fixed = lambda i, l: (0, 0)
    sel = _selector_matrices(tl, c)
    consts = [sel[name] for name in ("seg", "tril", "triu", "last", "expg", "expb")]
    return pl.pallas_call(
        functools.partial(_delta_kernel, c=c, g=g, nl=nl, conv_done=conv_done),
        grid=(b, nl),
        in_specs=[pl.BlockSpec((1, tl, 768), tile), pl.BlockSpec((1, tl, 256), tile),
                  pl.BlockSpec((1, tl, LANES), tile), pl.BlockSpec((1, 3, 768), per_b3),
                  pl.BlockSpec((1, 4, HEAD, HEAD), per_b4), pl.BlockSpec((4, 768), fixed),
                  pl.BlockSpec((1, LANES), fixed), pl.BlockSpec((1, LANES), fixed),
                  pl.BlockSpec((1, 256), fixed)] + [pl.BlockSpec(m.shape, fixed) for m in consts],
        out_specs=[pl.BlockSpec((1, tl, 256), tile), pl.BlockSpec((1, 3, 768), per_b3),
                   pl.BlockSpec((1, 4, HEAD, HEAD), per_b4)],
        out_shape=[jax.ShapeDtypeStruct((b, seq, 256), BF16),
                   jax.ShapeDtypeStruct((b, 3, 768), F32),
                   jax.ShapeDtypeStruct((b, 4, HEAD, HEAD), F32)],
        scratch_shapes=[pltpu.VMEM((tl + 8, 768), F32), pltpu.VMEM((4, HEAD, HEAD), F32),
                        pltpu.VMEM((tl, 256), F32)],
        compiler_params=pltpu.CompilerParams(dimension_semantics=("parallel", "arbitrary"),
                                             vmem_limit_bytes=VMEM_LIMIT),
        name="gated_delta",
    )(a_qkv, a_z, a_ab, conv0, s0, conv_w, alog_row, dtb_row, anorm_row, *consts)


def _rwkv_kernel(c_ref, shift0_ref, s0_ref, mu_ref, w0_ref, wup_ref, a0_ref, aup_ref, gup_ref,
                 kk_ref, ka_ref, rk_ref, lnw_ref, lnb_ref, seg_ref, tril_ref,
                 mix_ref, shiftn_ref, sn_ref, xp_ref, s_ref, y_ref, *, c, g, nl):
    l = pl.program_id(1)
    tl = c * g

    @pl.when(l == 0)
    def _():
        xp_ref[7:8, :] = shift0_ref[0]
        s_ref[...] = s0_ref[0]

    raw = c_ref[0]
    xp_ref[8:8 + tl, :] = raw
    prev = xp_ref[7:7 + tl, :]
    last = raw[tl - 1:tl, :]
    xp_ref[7:8, :] = last

    @pl.when(l == nl - 1)
    def _():
        shiftn_ref[0] = last

    x = raw + (prev - raw) * mu_ref[...]
    cr, ck, cv = x[:, 0:256], x[:, 256:512], x[:, 512:768]
    c_wa = x[:, 768:896]
    c_g = x[:, 896:1024]
    w_log = -jnp.exp(-jax.nn.softplus(-(w0_ref[...] + _mm(jnp.tanh(c_wa), wup_ref[...])))
                     - RWKV_DECAY_OFFSET)
    ca = jax.nn.sigmoid(a0_ref[...] + _mm(c_wa, aup_ref[...]))
    cg = _mm(jax.nn.sigmoid(c_g), gup_ref[...])
    seg = seg_ref[...]
    kkv = ck * kk_ref[...]
    kk = kkv * lax.rsqrt(_head_sums(kkv * kkv, seg) + L2_EPS)
    ck = ck * (1.0 + (ca - 1.0) * ka_ref[...])

    g_cum = _mm_exact_lhs(tril_ref[...], w_log)
    e_pos = jnp.exp(g_cum)
    e_neg = jnp.exp(-g_cum)
    a_t = -kk * jnp.exp(g_cum - w_log)
    b_t = (kk * ca) * e_neg
    k_t = ck * e_neg
    r_t = cr * e_pos

    row = _iota((c, c), 0)
    col = _iota((c, c), 1)
    lower = row >= col
    strict = row > col

    probs = [(slice(ci * c, (ci + 1) * c), h) for ci in range(g) for h in range(4)]
    heads = lambda h: slice(h * HEAD, (h + 1) * HEAD)
    bh = [b_t[rows, heads(h)] for rows, h in probs]
    kh = [k_t[rows, heads(h)] for rows, h in probs]
    vh = [cv[rows, heads(h)] for rows, h in probs]
    ar = [jnp.concatenate([a_t[rows, heads(h)], r_t[rows, heads(h)]], axis=0) for rows, h in probs]
    pb = [_mm(x, y_, _NT) for x, y_ in zip(ar, bh)]
    pk = [_mm(x, y_, _NT) for x, y_ in zip(ar, kh)]
    t_inv = _unit_lower_inverses([-jnp.where(strict, x[:c], 0.0) for x in pb], c)
    n_rb = [jnp.where(lower, x[c:], 0.0) for x in pb]
    mn = [_mm(jnp.concatenate([jnp.where(strict, x[:c], 0.0), jnp.where(lower, x[c:], 0.0)], axis=0), vv)
          for x, vv in zip(pk, vh)]
    bk = [jnp.concatenate([x, y_], axis=0) for x, y_ in zip(bh, kh)]
    g_end = [e_pos[rows.stop - 1:rows.stop, heads(h)] for rows, h in probs]
    ta = [_mm(t, jnp.concatenate([x[:c], m_[:c]], axis=1)) for t, x, m_ in zip(t_inv, ar, mn)]
    p_mat = [_mm(x[:, :HEAD], y_, _TN) for x, y_ in zip(ta, bh)]
    c_mat = [_mm(jnp.concatenate([x[:, HEAD:], vv], axis=0), y_, _TN) for x, vv, y_ in zip(ta, vh, bk)]
    nr = [_mm(x, y_) for x, y_ in zip(n_rb, ta)]
    r_eff = [x[c:] + y_[:, :HEAD] for x, y_ in zip(ar, nr)]
    y_off = [x[:, HEAD:] + m_[c:] for x, m_ in zip(nr, mn)]

    states = [s_ref[h] for h in range(4)]
    for ci in range(g):
        rows = slice(ci * c, (ci + 1) * c)
        idx = [ci * 4 + h for h in range(4)]
        y_new = [_mm(r_eff[n], states[h], _NT) + y_off[n] for h, n in enumerate(idx)]
        states = [(states[h] + _mm(states[h], p_mat[n]) + c_mat[n]) * g_end[n] for h, n in enumerate(idx)]
        for h in range(4):
            y_ref[rows, heads(h)] = y_new[h]
    for h in range(4):
        s_ref[h] = states[h]

    y = y_ref[...]
    mean = _head_sums(y, seg) * (1.0 / HEAD)
    yc = y - mean
    var = _head_sums(yc * yc, seg) * (1.0 / HEAD)
    cy = (yc * lax.rsqrt(var + C_LN_EPS)) * lnw_ref[...] + lnb_ref[...]
    bonus = _head_sums(cr * ck * rk_ref[...], seg) * cv
    mix_ref[0] = ((cy + bonus) * cg).astype(BF16)

    @pl.when(l == nl - 1)
    def _():
        sn_ref[0] = s_ref[...]


def _rwkv7(c_raw, shift0, s0, params):
    b, seq, width = c_raw.shape
    c, g = _chunking(seq)
    tl = c * g
    nl = seq // tl
    tile = lambda i, l: (i, l, 0)
    per_b3 = lambda i, l: (i, 0, 0)
    per_b4 = lambda i, l: (i, 0, 0, 0)
    fixed = lambda i, l: (0, 0)
    sel = _selector_matrices(tl, c)
    params = tuple(params) + (sel["seg"], sel["tril"])
    param_specs = [pl.BlockSpec(p.shape, fixed) for p in params]
    return pl.pallas_call(
        functools.partial(_rwkv_kernel, c=c, g=g, nl=nl),
        grid=(b, nl),
        in_specs=[pl.BlockSpec((1, tl, width), tile), pl.BlockSpec((1, 1, width), per_b3),
                  pl.BlockSpec((1, 4, HEAD, HEAD), per_b4)] + param_specs,
        out_specs=[pl.BlockSpec((1, tl, 256), tile), pl.BlockSpec((1, 1, width), per_b3),
                   pl.BlockSpec((1, 4, HEAD, HEAD), per_b4)],
        out_shape=[jax.ShapeDtypeStruct((b, seq, 256), BF16),
                   jax.ShapeDtypeStruct((b, 1, width), F32),
                   jax.ShapeDtypeStruct((b, 4, HEAD, HEAD), F32)],
        scratch_shapes=[pltpu.VMEM((tl + 8, width), F32), pltpu.VMEM((4, HEAD, HEAD), F32),
                        pltpu.VMEM((tl, 256), F32)],
        compiler_params=pltpu.CompilerParams(dimension_semantics=("parallel", "arbitrary"),
                                             vmem_limit_bytes=VMEM_LIMIT),
        name="rwkv7",
    )(c_raw, shift0, s0, *params)


def _attn_kernel(i_tab, j_tab, flag_tab, q_ref, k_ref, v_ref, lam_ref, bn_ref, o_ref,
                 m_ref, acc_ref, *, tq, tk, past, lk_true, lam_init, half_tiles):
    p = pl.program_id(1)
    i = i_tab[p]
    j = j_tab[p]

    @pl.when(j == 0)
    def _():
        m_ref[...] = jnp.full(m_ref.shape, MASK_VALUE, F32)
        acc_ref[...] = jnp.zeros(acc_ref.shape, F32)

    def step(masked, nkeys):
        nslab = nkeys // LANES
        if masked:
            q_pos = past + i * tq + _iota((tq, nkeys), 0)
            k_pos = j * tk + _iota((tq, nkeys), 1)
            visible = (jnp.right_shift(k_pos, 6) <= jnp.right_shift(q_pos, 6)) & (k_pos < lk_true)
        q = q_ref[0]
        k = k_ref[0, 0:nkeys, :]
        v = v_ref[0, 0:nkeys, :]

        def scores(hm):
            sl = slice(hm * HEAD, (hm + 1) * HEAD)
            return lax.dot_general(q[:, sl], k[:, sl], _NT, preferred_element_type=F32)

        ones = jnp.ones((nkeys, LANES), BF16)
        s_next = scores(0)
        for hm in range(8):
            h = hm // 2
            s = s_next
            if hm + 1 < 8:
                s_next = scores(hm + 1)
            if masked:
                s = jnp.where(visible, s, MASK_VALUE)
            slabs = [s[:, n * LANES:(n + 1) * LANES] for n in range(nslab)]
            m_cur = functools.reduce(jnp.maximum, slabs)
            m_prev = m_ref[hm]
            m_new = jnp.maximum(m_prev, jnp.max(m_cur, axis=1, keepdims=True))
            alpha = jnp.exp2(m_prev - m_new)
            e = jnp.concatenate([jnp.exp2(sb - m_new).astype(BF16) for sb in slabs], axis=1)
            v_ones = jnp.concatenate([v[:, h * LANES:(h + 1) * LANES], ones], axis=1)
            acc_ref[hm] = (jnp.concatenate([alpha, alpha], axis=1) * acc_ref[hm]
                           + jnp.dot(e, v_ones, preferred_element_type=F32))
            m_ref[hm] = m_new

    kind = jnp.right_shift(flag_tab[p], 1)

    @pl.when(kind == 0)
    def _():
        step(False, tk)

    @pl.when(kind == 1)
    def _():
        step(True, tk)

    if half_tiles:
        @pl.when(kind == 2)
        def _():
            step(True, tk // 2)

    @pl.when(jnp.bitwise_and(flag_tab[p], 1) != 0)
    def _():
        lp = lam_ref[...]
        lam = (jnp.exp(jnp.sum(lp[0:1] * lp[1:2], axis=1, keepdims=True))
               - jnp.exp(jnp.sum(lp[2:3] * lp[3:4], axis=1, keepdims=True)) + lam_init)
        for h in range(4):
            a1 = acc_ref[2 * h]
            a2 = acc_ref[2 * h + 1]
            o = a1[:, :LANES] / a1[:, LANES:] - lam * (a2[:, :LANES] / a2[:, LANES:])
            o = _rmsnorm_rows(o, bn_ref[...]) * (1.0 - lam_init)
            o_ref[0, :, h * LANES:(h + 1) * LANES] = o.astype(BF16)


def _diff_attention(q, k, v, lam_params, bnorm_row, past, lam_init, tq, tk):
    b, lq, width = q.shape
    lk_true = k.shape[1]
    nk = -(-lk_true // tk)
    if nk * tk != lk_true:
        pad = ((0, 0), (0, nk * tk - lk_true), (0, 0))
        k = jnp.pad(k, pad)
        v = jnp.pad(v, pad)
    nq = lq // tq
    half_tiles = (tk // 2) % LANES == 0
    i_list, j_list, flag_list = [], [], []
    for i in range(nq):
        first_pos = past + i * tq
        last_pos = first_pos + tq - 1
        visible_end = min((last_pos // CHUNK + 1) * CHUNK, lk_true)
        j_max = (visible_end - 1) // tk
        for j in range(j_max + 1):
            key_end = (j + 1) * tk
            if (key_end - 1) // CHUNK <= first_pos // CHUNK and key_end <= lk_true:
                kind = 0
            elif half_tiles and visible_end - j * tk <= tk // 2:
                kind = 2
            else:
                kind = 1
            i_list.append(i)
            j_list.append(j)
            flag_list.append((1 if j == j_max else 0) + 2 * kind)
    tabs = [jnp.asarray(np.asarray(t, np.int32)) for t in (i_list, j_list, flag_list)]
    grid_spec = pltpu.PrefetchScalarGridSpec(
        num_scalar_prefetch=3,
        grid=(b, len(i_list)),
        in_specs=[pl.BlockSpec((1, tq, width), lambda bi, p, it, jt, lt: (bi, it[p], 0)),
                  pl.BlockSpec((1, tk, width), lambda bi, p, it, jt, lt: (bi, jt[p], 0)),
                  pl.BlockSpec((1, tk, width), lambda bi, p, it, jt, lt: (bi, jt[p], 0)),
                  pl.BlockSpec((4, HEAD), lambda bi, p, it, jt, lt: (0, 0)),
                  pl.BlockSpec((1, LANES), lambda bi, p, it, jt, lt: (0, 0))],
        out_specs=pl.BlockSpec((1, tq, width), lambda bi, p, it, jt, lt: (bi, it[p], 0)),
        scratch_shapes=[pltpu.VMEM((8, tq, LANES), F32), pltpu.VMEM((8, tq, 2 * LANES), F32)],
    )
    return pl.pallas_call(
        functools.partial(_attn_kernel, tq=tq, tk=tk, past=past, lk_true=lk_true, lam_init=lam_init,
                          half_tiles=half_tiles),
        grid_spec=grid_spec,
        out_shape=jax.ShapeDtypeStruct((b, lq, width), BF16),
        compiler_params=pltpu.CompilerParams(dimension_semantics=("parallel", "arbitrary"),
                                             vmem_limit_bytes=VMEM_LIMIT),
        name="diff_attention",
    )(*tabs, q, k, v, lam_params, bnorm_row)


def _mlp_kernel(h_ref, ma_ref, mb_ref, mc_ref, wo_ref, nf_ref, w1_ref, w2_ref, np_ref, wg_ref,
                wp_ref, p_ref, nfin_ref, o_ref, *, tf, final_norm):
    mixed = (jnp.dot(ma_ref[...], wo_ref[0:256, :], preferred_element_type=F32)
             + jnp.dot(mb_ref[...], wo_ref[256:768, :], preferred_element_type=F32)
             + jnp.dot(mc_ref[...], wo_ref[768:1024, :], preferred_element_type=F32))
    h1 = h_ref[...] + mixed
    xn = _rmsnorm_rows(h1, nf_ref[...]).astype(BF16)
    h2 = h1
    for f0 in range(0, w1_ref.shape[1], tf):
        u = jnp.maximum(jnp.dot(xn, w1_ref[:, f0:f0 + tf], preferred_element_type=F32), 0.0)
        h2 = h2 + jnp.dot((u * u).astype(BF16), w2_ref[f0:f0 + tf, :], preferred_element_type=F32)
    gate = jax.nn.sigmoid(jnp.dot(_rmsnorm_rows(h2, np_ref[...]).astype(BF16), wg_ref[...],
                                  preferred_element_type=F32))
    h3 = h2 + gate * jnp.dot(p_ref[...].astype(BF16), wp_ref[...], preferred_element_type=F32)
    if final_norm:
        h3 = _rmsnorm_rows(h3, nfin_ref[...])
    o_ref[...] = h3


def _mix_mlp(h2d, mix_a, mix_b, mix_c, w_out, norm_ffn, w_ff1, w_ff2, norm_ple, w_gate, w_proj,
             p2d, norm_final, final_norm, tm, tf):
    t, d = h2d.shape
    dff = w_ff1.shape[1]
    row = lambda i: (i, 0)
    fixed = lambda i: (0, 0)
    resident = lambda shape: pl.BlockSpec(shape, fixed, pipeline_mode=pl.Buffered(1))
    return pl.pallas_call(
        functools.partial(_mlp_kernel, tf=tf, final_norm=final_norm),
        grid=(t // tm,),
        in_specs=[pl.BlockSpec((tm, d), row), pl.BlockSpec((tm, 256), row),
                  pl.BlockSpec((tm, 512), row), pl.BlockSpec((tm, 256), row),
                  resident((d, d)), resident((1, d)), resident((d, dff)), resident((dff, d)),
                  resident((1, d)), resident((d, d)), resident((p2d.shape[1], d)),
                  pl.BlockSpec((tm, p2d.shape[1]), row), resident((1, d))],
        out_specs=pl.BlockSpec((tm, d), row),
        out_shape=jax.ShapeDtypeStruct((t, d), F32),
        compiler_params=pltpu.CompilerParams(dimension_semantics=("parallel",),
                                             vmem_limit_bytes=VMEM_LIMIT),
        name="mix_mlp",
    )(h2d, mix_a, mix_b, mix_c, w_out, norm_ffn, w_ff1, w_ff2, norm_ple, w_gate, w_proj, p2d,
      norm_final)


def _rope_tables(past, seq_len):
    half = HEAD // 2
    inv = ROPE_THETA ** (-2.0 * jnp.arange(half, dtype=F32) / HEAD)
    pos = past + jnp.arange(seq_len, dtype=jnp.int32)
    ang = pos.astype(F32)[:, None] * inv[None, :]
    cos, sin, zero = jnp.cos(ang), jnp.sin(ang), jnp.zeros_like(ang)
    rep = LANES // HEAD
    cos_t = jnp.tile(jnp.concatenate([cos, cos], axis=1), (1, rep))
    sa_t = jnp.tile(jnp.concatenate([-sin, zero], axis=1), (1, rep))
    sb_t = jnp.tile(jnp.concatenate([zero, sin], axis=1), (1, rep))
    return cos_t, sa_t, sb_t


def _pad_lanes(v, width=LANES):
    return jnp.pad(v, (0, width - v.shape[0]))[None, :]


def _prep_layer(i, norm_mix, w_in, a_conv_w, a_A_log, a_dt_bias, a_norm,
                b_lam_q1, b_lam_k1, b_lam_q2, b_lam_k2, b_norm,
                c_mu, c_w0, c_w_up, c_a0, c_a_up, c_g_up, c_k_k, c_k_a, c_r_k, c_ln_w, c_ln_b,
                w_out, norm_ffn, w_ff1, w_ff2, norm_ple, w_ple_gate, w_ple_proj):
    w = w_in[i]
    d = w.shape[0]
    w_perm = jnp.concatenate(
        [w[:, 0:1024], w[:, 1032:3592], w[:, 1024:1032],
         jnp.zeros((d, _PROJ_WIDTH - 3592), w.dtype)], axis=1).astype(BF16)
    zeros_rank = jnp.zeros_like(c_w_up[i])
    rwkv = (c_mu[i][None, :], c_w0[i][None, :],
            jnp.concatenate([c_w_up[i], zeros_rank], axis=0).astype(BF16), c_a0[i][None, :],
            jnp.concatenate([jnp.zeros_like(c_a_up[i]), c_a_up[i]], axis=0).astype(BF16),
            c_g_up[i].astype(BF16), c_k_k[i][None, :], c_k_a[i][None, :],
            c_r_k[i].reshape(1, -1), c_ln_w[i][None, :], c_ln_b[i][None, :])
    return dict(
        norm_mix=norm_mix[i][None, :], w_in=w_perm, conv_w=a_conv_w[i],
        alog=_pad_lanes(a_A_log[i]), dtb=_pad_lanes(a_dt_bias[i]),
        anorm=jnp.tile(a_norm[i], 4)[None, :],
        lam=jnp.stack([b_lam_q1[i], b_lam_k1[i], b_lam_q2[i], b_lam_k2[i]], axis=0),
        bnorm=b_norm[i][None, :], rwkv=rwkv,
        w_out=w_out[i].astype(BF16), norm_ffn=norm_ffn[i][None, :],
        w_ff1=w_ff1[i].astype(BF16), w_ff2=w_ff2[i].astype(BF16),
        norm_ple=norm_ple[i][None, :], w_gate=w_ple_gate[i].astype(BF16),
        w_proj=w_ple_proj[i].astype(BF16))


def _pick_tile(n, target):
    t = min(n, target)
    while n % t:
        t //= 2
    return t


def _trunk(x, p, cache_k, cache_v, conv_buf, delta_s, shift_prev, wkv_s, layers, norm_final):
    b, seq, d = x.shape
    depth = len(layers)
    past = cache_k.shape[2]
    t = b * seq
    tm = _pick_tile(t, PROJ_TM)
    rope_tabs = _rope_tables(past, seq)
    h = x.reshape(t, d)
    states = []
    kv_stacked = None
    conv_in_projection = tm < seq and seq % tm == 0
    for i, lp in enumerate(layers):
        conv = (conv_buf[i], lp["conv_w"]) if conv_in_projection else None
        a_qkv, a_z, a_ab, q_b, k_f, v_f, k_b, v_b, c_raw, *conv_state = _input_projection(
            h, lp["norm_mix"], lp["w_in"], rope_tabs, seq, tm, i, depth, kv_stacked, conv)
        kv_stacked = (k_f, v_f)
        mix_a, conv_n, delta_n = _gated_delta(
            a_qkv.reshape(b, seq, -1), a_z.reshape(b, seq, -1), a_ab.reshape(b, seq, -1),
            conv_buf[i], delta_s[i], lp["conv_w"], lp["alog"], lp["dtb"], lp["anorm"],
            conv_done=conv_in_projection)
        if conv_in_projection:
            conv_n = conv_state[0]
        k_all = k_b.reshape(b, seq, -1)
        v_all = v_b.reshape(b, seq, -1)
        if past:
            k_all = jnp.concatenate([cache_k[i].reshape(b, past, -1).astype(BF16), k_all], axis=1)
            v_all = jnp.concatenate([cache_v[i].reshape(b, past, -1).astype(BF16), v_all], axis=1)
        lam_init = 0.8 - 0.6 * math.exp(-0.3 * i)
        mix_b = _diff_attention(q_b.reshape(b, seq, -1), k_all, v_all, lp["lam"], lp["bnorm"],
                                past, lam_init, _pick_tile(seq, ATTN_TQ),
                                ATTN_TK if past + seq >= 8 * ATTN_TK else ATTN_TK // 2)
        mix_c, shift_n, wkv_n = _rwkv7(c_raw.reshape(b, seq, -1), shift_prev[i][:, None, :],
                                       wkv_s[i], lp["rwkv"])
        h = _mix_mlp(h, mix_a.reshape(t, -1), mix_b.reshape(t, -1), mix_c.reshape(t, -1),
                     lp["w_out"], lp["norm_ffn"], lp["w_ff1"], lp["w_ff2"], lp["norm_ple"],
                     lp["w_gate"], lp["w_proj"], p[i].reshape(t, -1), norm_final[None, :],
                     i == depth - 1, _pick_tile(t, MLP_TM), MLP_TF)
        states.append((conv_n, delta_n, shift_n[:, 0, :], wkv_n))
    conv_all, delta_all, shift_all, wkv_all = (
        jnp.stack([st[j] for st in states], axis=0) for j in range(4))
    k_all_layers, v_all_layers = (a.reshape(depth, b, seq, 4, LANES) for a in kv_stacked)
    return h.reshape(b, seq, d), [conv_all, delta_all, k_all_layers, v_all_layers, shift_all, wkv_all]


def kernel(x_prompt, x_sample, cache_b_k, cache_b_v, state_a_conv, state_a_delta, state_c_shift, state_c_wkv, p_prompt, p_sample, norm_mix, w_in, a_conv_w, a_A_log, a_dt_bias, a_norm, b_lam_q1, b_lam_k1, b_lam_q2, b_lam_k2, b_norm, c_mu, c_w0, c_w_up, c_a0, c_a_up, c_g_up, c_k_k, c_k_a, c_r_k, c_ln_w, c_ln_b, w_out, norm_ffn, w_ff1, w_ff2, norm_ple, w_ple_gate, w_ple_proj, norm_final):
    depth = w_in.shape[0]
    layers = [_prep_layer(i, norm_mix, w_in, a_conv_w, a_A_log, a_dt_bias, a_norm,
                          b_lam_q1, b_lam_k1, b_lam_q2, b_lam_k2, b_norm,
                          c_mu, c_w0, c_w_up, c_a0, c_a_up, c_g_up, c_k_k, c_k_a, c_r_k, c_ln_w, c_ln_b,
                          w_out, norm_ffn, w_ff1, w_ff2, norm_ple, w_ple_gate, w_ple_proj)
              for i in range(depth)]
    bp = x_prompt.shape[0]
    dt = x_prompt.dtype
    zeros = lambda ref: jnp.zeros((depth, bp) + ref.shape[2:], dt)
    empty_k = jnp.zeros((depth, bp, 0) + cache_b_k.shape[3:], dt)
    empty_v = jnp.zeros((depth, bp, 0) + cache_b_v.shape[3:], dt)
    y_prompt, st_p = _trunk(x_prompt, p_prompt, empty_k, empty_v, zeros(state_a_conv),
                            zeros(state_a_delta), zeros(state_c_shift), zeros(state_c_wkv),
                            layers, norm_final)
    y_sample, st_s = _trunk(x_sample, p_sample, cache_b_k, cache_b_v, state_a_conv, state_a_delta,
                            state_c_shift, state_c_wkv, layers, norm_final)
    return (y_prompt, y_sample, *st_p, *st_s)
```

```python
import functools
import math

import numpy as np
import jax
import jax.numpy as jnp
from jax import lax
from jax.experimental import pallas as pl
from jax.experimental.pallas import tpu as pltpu

F32 = jnp.float32
BF16 = jnp.bfloat16

CHUNK = 64
ROPE_THETA = 10000.0
NORM_EPS = 1e-6
L2_EPS = 1e-6
C_LN_EPS = 64e-5
RWKV_DECAY_OFFSET = 0.5
MASK_VALUE = float(np.finfo(np.float32).min)

LANES = 128
HEAD = 64
VMEM_LIMIT = 56 * 1024 * 1024
PROJ_TM = 512
MLP_TM = 512
MLP_TF = 1024
CHUNKS_PER_STEP = 8
ATTN_TQ = 512
ATTN_TK = 1024
ATTN_Q_SCALE = (64 ** -0.5) * math.log2(math.e)


def _iota(shape, axis):
    return lax.broadcasted_iota(jnp.int32, shape, axis)


_NN = (((1,), (0,)), ((), ()))
_NT = (((1,), (1,)), ((), ()))
_TN = (((0,), (0,)), ((), ()))


def _mm(a, b, dims=_NN):
    return lax.dot_general(a.astype(BF16), b.astype(BF16), dims, preferred_element_type=F32)


def _split3(x):
    h = x.astype(BF16)
    r = x - h.astype(F32)
    m = r.astype(BF16)
    lo = (r - m.astype(F32)).astype(BF16)
    return h, m, lo


def _mm_exact_rhs(x, ones_bf16, dims=_NN):
    h, m, lo = _split3(x)
    d = functools.partial(lax.dot_general, dimension_numbers=dims, preferred_element_type=F32)
    return d(h, ones_bf16) + d(m, ones_bf16) + d(lo, ones_bf16)


def _head_sums(x, seg_bf16):
    h = x.astype(BF16)
    lo = (x - h.astype(F32)).astype(BF16)
    return (jnp.dot(h, seg_bf16, preferred_element_type=F32)
            + jnp.dot(lo, seg_bf16, preferred_element_type=F32))


def _mm_exact_lhs(ones_bf16, x, dims=_NN):
    h, m, lo = _split3(x)
    d = functools.partial(lax.dot_general, dimension_numbers=dims, preferred_element_type=F32)
    return d(ones_bf16, h) + d(ones_bf16, m) + d(ones_bf16, lo)


def _selector_matrices(tl, c):
    tok = np.arange(tl)
    same = (tok[:, None] // c) == (tok[None, :] // c)
    lane = np.arange(4 * HEAD)
    src = np.arange(LANES)
    mats = dict(
        seg=(lane[:, None] // HEAD) == (lane[None, :] // HEAD),
        tril=same & (tok[:, None] >= tok[None, :]),
        triu=same & (tok[:, None] <= tok[None, :]),
        last=tok[None, :] == (tok[:, None] // c) * c + c - 1,
        expg=src[:, None] == lane[None, :] // HEAD,
        expb=src[:, None] == lane[None, :] // HEAD + 4)
    return {name: jnp.asarray(m.astype(np.float32), BF16) for name, m in mats.items()}


def _chunking(seq):
    c = min(CHUNK, seq)
    assert c & (c - 1) == 0 and seq % c == 0, seq
    g = min(CHUNKS_PER_STEP, seq // c)
    while (seq // c) % g:
        g -= 1
    return c, g


def _rmsnorm_rows(x, g):
    return x * lax.rsqrt(jnp.mean(x * x, axis=-1, keepdims=True) + NORM_EPS) * g


def _unit_lower_inverses(a_list, c):
    row = _iota((c, c), 0)
    col = _iota((c, c), 1)

    def same_block(shift):
        return jnp.right_shift(row, shift) == jnp.right_shift(col, shift)

    eye = (row == col).astype(F32)
    leaf = same_block(3)
    ns = [jnp.where(leaf, -a, 0.0) for a in a_list]
    ts = [eye + n for n in ns]
    n2s = [_mm(n, n) for n in ns]
    ts = [t + _mm(t, n2) for t, n2 in zip(ts, n2s)]
    n4s = [_mm(n2, n2) for n2 in n2s]
    ts = [t + _mm(t, n4) for t, n4 in zip(ts, n4s)]
    shift = 3
    while (1 << shift) < c:
        off_mask = same_block(shift + 1) & jnp.logical_not(same_block(shift))
        tos = [_mm(t, jnp.where(off_mask, a, 0.0)) for t, a in zip(ts, a_list)]
        ts = [t - _mm(to, t) for t, to in zip(ts, tos)]
        shift += 1
    return ts


_PROJ_A_QKV = 0
_PROJ_A_Z = 768
_PROJ_B_Q = 1024
_PROJ_B_K = 1536
_PROJ_B_V = 2048
_PROJ_C = 2560
_PROJ_A_AB = 3584
_PROJ_WIDTH = 3712


def _proj_kernel(*refs, n_alias):
    x_ref, g_ref, w_ref, cos_ref, sa_ref, sb_ref = refs[:6]
    aqkv_ref, az_ref, aab_ref, q_ref, k_ref, v_ref, kb_ref, vb_ref, c_ref = refs[6 + n_alias:]
    xn = _rmsnorm_rows(x_ref[...], g_ref[...]).astype(BF16)

    def proj(c0, c1):
        return jnp.dot(xn, w_ref[:, c0:c1], preferred_element_type=F32)

    a_all = proj(_PROJ_A_QKV, _PROJ_B_Q)
    aqkv_ref[...] = a_all[:, :_PROJ_A_Z]
    az_ref[...] = a_all[:, _PROJ_A_Z:]
    cos = cos_ref[...]
    sa = sa_ref[...]
    sb = sb_ref[...]

    def rope(x):
        return x * cos + pltpu.roll(x, LANES - 32, 1) * sa + pltpu.roll(x, 32, 1) * sb

    q_all = proj(_PROJ_B_Q, _PROJ_B_K)
    k_all = proj(_PROJ_B_K, _PROJ_B_V)
    for h in range(4):
        lo, hi = h * LANES, (h + 1) * LANES
        q_ref[:, lo:hi] = (rope(q_all[:, lo:hi]) * ATTN_Q_SCALE).astype(BF16)
        k = rope(k_all[:, lo:hi])
        k_ref[:, h, :] = k
        kb_ref[:, lo:hi] = k.astype(BF16)
    v = proj(_PROJ_B_V, _PROJ_C)
    for h in range(4):
        v_ref[:, h, :] = v[:, h * LANES:(h + 1) * LANES]
    vb_ref[...] = v.astype(BF16)
    c_all = proj(_PROJ_C, _PROJ_WIDTH)
    c_ref[...] = c_all[:, :_PROJ_A_AB - _PROJ_C]
    aab_ref[...] = c_all[:, _PROJ_A_AB - _PROJ_C:]


def _input_projection(x2d, g_row, w_bf16, rope_tabs, seq_len, tm, layer, depth, kv_stacked):
    t, d = x2d.shape
    nt = t // tm
    cos_t, sa_t, sb_t = rope_tabs
    if tm >= seq_len:
        reps = tm // seq_len
        cos_t, sa_t, sb_t = (jnp.tile(a, (reps, 1)) for a in (cos_t, sa_t, sb_t))
        tab_map = lambda i: (0, 0)
    else:
        per_seq = seq_len // tm
        tab_map = lambda i: (i % per_seq, 0)
    row = lambda i: (i, 0)
    fixed = lambda i: (0, 0)
    tab_spec = pl.BlockSpec((tm, LANES), tab_map)
    widths = [(768, F32), (256, F32), (128, F32), (512, BF16), (512, F32), (512, F32),
              (512, BF16), (512, BF16), (1024, F32)]
    stacked = (4, 5)
    out_specs = [pl.BlockSpec((None, tm, 4, LANES), lambda i: (layer, i, 0, 0)) if n in stacked
                 else pl.BlockSpec((tm, w), row) for n, (w, _) in enumerate(widths)]
    out_shape = [jax.ShapeDtypeStruct((depth, t, 4, LANES) if n in stacked else (t, w), dt)
                 for n, (w, dt) in enumerate(widths)]
    in_specs = [pl.BlockSpec((tm, d), row), pl.BlockSpec((1, d), fixed),
                pl.BlockSpec((d, _PROJ_WIDTH), fixed), tab_spec, tab_spec, tab_spec]
    operands = [x2d, g_row, w_bf16, cos_t, sa_t, sb_t]
    aliases = {}
    if kv_stacked is not None:
        for out_idx, arr in zip(stacked, kv_stacked):
            aliases[len(operands)] = out_idx
            in_specs.append(pl.BlockSpec(memory_space=pl.ANY))
            operands.append(arr)
    return pl.pallas_call(
        functools.partial(_proj_kernel, n_alias=len(aliases)),
        grid=(nt,),
        in_specs=in_specs,
        out_specs=out_specs,
        out_shape=out_shape,
        input_output_aliases=aliases,
        compiler_params=pltpu.CompilerParams(dimension_semantics=("parallel",),
                                             vmem_limit_bytes=VMEM_LIMIT),
        name="input_projection",
    )(*operands)


def _delta_kernel(qkv_ref, z_ref, ab_ref, conv0_ref, s0_ref, convw_ref, alog_ref, dtb_ref, anorm_ref,
                  seg_ref, tril_ref, triu_ref, last_ref, expg_ref, expb_ref,
                  mix_ref, convn_ref, sn_ref, xp_ref, s_ref, o_ref, *, c, g, nl):
    l = pl.program_id(1)
    tl = c * g

    @pl.when(l == 0)
    def _():
        xp_ref[5:8, :] = conv0_ref[0]
        s_ref[...] = s0_ref[0]

    x = qkv_ref[0]
    xp_ref[8:8 + tl, :] = x
    w = convw_ref[...]
    y = xp_ref[5:5 + tl, :] * w[0:1]
    y = y + xp_ref[6:6 + tl, :] * w[1:2]
    y = y + xp_ref[7:7 + tl, :] * w[2:3]
    y = y + x * w[3:4]
    tail = xp_ref[5 + tl:8 + tl, :]
    xp_ref[5:8, :] = tail

    @pl.when(l == nl - 1)
    def _():
        convn_ref[0] = tail

    act = y * jax.nn.sigmoid(y)
    seg = seg_ref[...]

    def l2n(t):
        return t * lax.rsqrt(_head_sums(t * t, seg) + L2_EPS)

    q = l2n(act[:, 0:256]) * (HEAD ** -0.5)
    k = l2n(act[:, 256:512])
    v = act[:, 512:768]

    ab = ab_ref[0]
    gl = -jnp.exp(alog_ref[...]) * jax.nn.softplus(ab + dtb_ref[...])
    beta_w = _mm_exact_rhs(jax.nn.sigmoid(ab), expb_ref[...])
    g_w = _mm_exact_lhs(tril_ref[...], _mm_exact_rhs(gl, expg_ref[...]))
    g_rows = _mm_exact_rhs(gl, triu_ref[...], _TN)
    g_last_w = _mm_exact_lhs(last_ref[...], g_w)
    eg_w = jnp.exp(g_w)
    v_beta = v * beta_w
    k_beta_eg = k * (beta_w * eg_w)
    q_dec = q * eg_w
    k_tail_w = k * jnp.exp(g_last_w - g_w)
    decay_end_w = jnp.exp(g_last_w)

    row = _iota((c, c), 0)
    col = _iota((c, c), 1)
    lower = row >= col
    strict = row > col

    probs = [(slice(ci * c, (ci + 1) * c), h) for ci in range(g) for h in range(4)]
    heads = lambda h: slice(h * HEAD, (h + 1) * HEAD)
    decay = [jnp.where(lower, jnp.exp(jnp.where(lower, g_w[rows, heads(h)][:, :c] - g_rows[h:h + 1, rows], 0.0)),
                       0.0) for rows, h in probs]
    kh = [k[rows, heads(h)] for rows, h in probs]
    kq = [_mm(jnp.concatenate([kk_, q[rows, heads(h)]], axis=0), kk_, _NT)
          for kk_, (rows, h) in zip(kh, probs)]
    t_inv = _unit_lower_inverses(
        [jnp.where(strict, beta_w[rows, heads(h)][:, :c] * x[:c] * d, 0.0)
         for x, d, (rows, h) in zip(kq, decay, probs)], c)
    sol = [_mm(t, jnp.concatenate([v_beta[rows, heads(h)], k_beta_eg[rows, heads(h)]], axis=1))
           for t, (rows, h) in zip(t_inv, probs)]
    qk = [jnp.where(lower, x[c:] * d, 0.0) for x, d in zip(kq, decay)]
    kt_sol = [_mm(k_tail_w[rows, heads(h)], so, _TN) for so, (rows, h) in zip(sol, probs)]
    qk_sol = [_mm(x, so) for x, so in zip(qk, sol)]
    q_eff = [q_dec[rows, heads(h)] - x[:, HEAD:] for x, (rows, h) in zip(qk_sol, probs)]
    decay_end = [decay_end_w[rows.stop - 1:rows.stop, heads(h)] for rows, h in probs]

    states = [s_ref[h] for h in range(4)]
    for ci in range(g):
        rows = slice(ci * c, (ci + 1) * c)
        idx = [ci * 4 + h for h in range(4)]
        o_new = [_mm(q_eff[n], states[h]) + qk_sol[n][:, :HEAD] for h, n in enumerate(idx)]
        states = [states[h] * decay_end[n] + kt_sol[n][:, :HEAD] - _mm(kt_sol[n][:, HEAD:], states[h])
                  for h, n in enumerate(idx)]
        for h in range(4):
            o_ref[rows, heads(h)] = o_new[h]
    for h in range(4):
        s_ref[h] = states[h]

    o = o_ref[...]
    ms = _head_sums(o * o, seg) * (1.0 / HEAD)
    z = z_ref[0]
    ao = (o * lax.rsqrt(ms + NORM_EPS) * anorm_ref[...]) * (z * jax.nn.sigmoid(z))
    mix_ref[0] = ao.astype(BF16)

    @pl.when(l == nl - 1)
    def _():
        sn_ref[0] = s_ref[...]


def _gated_delta(a_qkv, a_z, a_ab, conv0, s0, conv_w, alog_row, dtb_row, anorm_row):
    b, seq, _ = a_qkv.shape
    c, g = _chunking(seq)
    tl = c * g
    nl = seq // tl
    tile = lambda i, l: (i, l, 0)
    per_b3 = lambda i, l: (i, 0, 0)
    per_b4 = lambda i, l: (i, 0, 0, 0)
    fixed = lambda i, l: (0, 0)
    sel = _selector_matrices(tl, c)
    consts = [sel[name] for name in ("seg", "tril", "triu", "last", "expg", "expb")]
    return pl.pallas_call(
        functools.partial(_delta_kernel, c=c, g=g, nl=nl),
        grid=(b, nl),
        in_specs=[pl.BlockSpec((1, tl, 768), tile), pl.BlockSpec((1, tl, 256), tile),
                  pl.BlockSpec((1, tl, LANES), tile), pl.BlockSpec((1, 3, 768), per_b3),
                  pl.BlockSpec((1, 4, HEAD, HEAD), per_b4), pl.BlockSpec((4, 768), fixed),
                  pl.BlockSpec((1, LANES), fixed), pl.BlockSpec((1, LANES), fixed),
                  pl.BlockSpec((1, 256), fixed)] + [pl.BlockSpec(m.shape, fixed) for m in consts],
        out_specs=[pl.BlockSpec((1, tl, 256), tile), pl.BlockSpec((1, 3, 768), per_b3),
                   pl.BlockSpec((1, 4, HEAD, HEAD), per_b4)],
        out_shape=[jax.ShapeDtypeStruct((b, seq, 256), BF16),
                   jax.ShapeDtypeStruct((b, 3, 768), F32),
                   jax.ShapeDtypeStruct((b, 4, HEAD, HEAD), F32)],
        scratch_shapes=[pltpu.VMEM((tl + 8, 768), F32), pltpu.VMEM((4, HEAD, HEAD), F32),
                        pltpu.VMEM((tl, 256), F32)],
        compiler_params=pltpu.CompilerParams(dimension_semantics=("parallel", "arbitrary"),
                                             vmem_limit_bytes=VMEM_LIMIT),
        name="gated_delta",
    )(a_qkv, a_z, a_ab, conv0, s0, conv_w, alog_row, dtb_row, anorm_row, *consts)


def _rwkv_kernel(c_ref, shift0_ref, s0_ref, mu_ref, w0_ref, wup_ref, a0_ref, aup_ref, gup_ref,
                 kk_ref, ka_ref, rk_ref, lnw_ref, lnb_ref, seg_ref, tril_ref,
                 mix_ref, shiftn_ref, sn_ref, xp_ref, s_ref, y_ref, *, c, g, nl):
    l = pl.program_id(1)
    tl = c * g

    @pl.when(l == 0)
    def _():
        xp_ref[7:8, :] = shift0_ref[0]
        s_ref[...] = s0_ref[0]

    raw = c_ref[0]
    xp_ref[8:8 + tl, :] = raw
    prev = xp_ref[7:7 + tl, :]
    last = raw[tl - 1:tl, :]
    xp_ref[7:8, :] = last

    @pl.when(l == nl - 1)
    def _():
        shiftn_ref[0] = last

    x = raw + (prev - raw) * mu_ref[...]
    cr, ck, cv = x[:, 0:256], x[:, 256:512], x[:, 512:768]
    c_wa = x[:, 768:896]
    c_g = x[:, 896:1024]
    w_log = -jnp.exp(-jax.nn.softplus(-(w0_ref[...] + _mm(jnp.tanh(c_wa), wup_ref[...])))
                     - RWKV_DECAY_OFFSET)
    ca = jax.nn.sigmoid(a0_ref[...] + _mm(c_wa, aup_ref[...]))
    cg = _mm(jax.nn.sigmoid(c_g), gup_ref[...])
    seg = seg_ref[...]
    kkv = ck * kk_ref[...]
    kk = kkv * lax.rsqrt(_head_sums(kkv * kkv, seg) + L2_EPS)
    ck = ck * (1.0 + (ca - 1.0) * ka_ref[...])

    g_cum = _mm_exact_lhs(tril_ref[...], w_log)
    e_pos = jnp.exp(g_cum)
    e_neg = jnp.exp(-g_cum)
    a_t = -kk * jnp.exp(g_cum - w_log)
    b_t = (kk * ca) * e_neg
    k_t = ck * e_neg
    r_t = cr * e_pos

    row = _iota((c, c), 0)
    col = _iota((c, c), 1)
    lower = row >= col
    strict = row > col

    probs = [(slice(ci * c, (ci + 1) * c), h) for ci in range(g) for h in range(4)]
    heads = lambda h: slice(h * HEAD, (h + 1) * HEAD)
    bh = [b_t[rows, heads(h)] for rows, h in probs]
    kh = [k_t[rows, heads(h)] for rows, h in probs]
    vh = [cv[rows, heads(h)] for rows, h in probs]
    ar = [jnp.concatenate([a_t[rows, heads(h)], r_t[rows, heads(h)]], axis=0) for rows, h in probs]
    pb = [_mm(x, y_, _NT) for x, y_ in zip(ar, bh)]
    pk = [_mm(x, y_, _NT) for x, y_ in zip(ar, kh)]
    t_inv = _unit_lower_inverses([-jnp.where(strict, x[:c], 0.0) for x in pb], c)
    n_rb = [jnp.where(lower, x[c:], 0.0) for x in pb]
    mn = [_mm(jnp.concatenate([jnp.where(strict, x[:c], 0.0), jnp.where(lower, x[c:], 0.0)], axis=0), vv)
          for x, vv in zip(pk, vh)]
    bk = [jnp.concatenate([x, y_], axis=0) for x, y_ in zip(bh, kh)]
    g_end = [e_pos[rows.stop - 1:rows.stop, heads(h)] for rows, h in probs]
    ta = [_mm(t, jnp.concatenate([x[:c], m_[:c]], axis=1)) for t, x, m_ in zip(t_inv, ar, mn)]
    p_mat = [_mm(x[:, :HEAD], y_, _TN) for x, y_ in zip(ta, bh)]
    c_mat = [_mm(jnp.concatenate([x[:, HEAD:], vv], axis=0), y_, _TN) for x, vv, y_ in zip(ta, vh, bk)]
    nr = [_mm(x, y_) for x, y_ in zip(n_rb, ta)]
    r_eff = [x[c:] + y_[:, :HEAD] for x, y_ in zip(ar, nr)]
    y_off = [x[:, HEAD:] + m_[c:] for x, m_ in zip(nr, mn)]

    states = [s_ref[h] for h in range(4)]
    for ci in range(g):
        rows = slice(ci * c, (ci + 1) * c)
        idx = [ci * 4 + h for h in range(4)]
        y_new = [_mm(r_eff[n], states[h], _NT) + y_off[n] for h, n in enumerate(idx)]
        states = [(states[h] + _mm(states[h], p_mat[n]) + c_mat[n]) * g_end[n] for h, n in enumerate(idx)]
        for h in range(4):
            y_ref[rows, heads(h)] = y_new[h]
    for h in range(4):
        s_ref[h] = states[h]

    y = y_ref[...]
    mean = _head_sums(y, seg) * (1.0 / HEAD)
    yc = y - mean
    var = _head_sums(yc * yc, seg) * (1.0 / HEAD)
    cy = (yc * lax.rsqrt(var + C_LN_EPS)) * lnw_ref[...] + lnb_ref[...]
    bonus = _head_sums(cr * ck * rk_ref[...], seg) * cv
    mix_ref[0] = ((cy + bonus) * cg).astype(BF16)

    @pl.when(l == nl - 1)
    def _():
        sn_ref[0] = s_ref[...]


def _rwkv7(c_raw, shift0, s0, params):
    b, seq, width = c_raw.shape
    c, g = _chunking(seq)
    tl = c * g
    nl = seq // tl
    tile = lambda i, l: (i, l, 0)
    per_b3 = lambda i, l: (i, 0, 0)
    per_b4 = lambda i, l: (i, 0, 0, 0)
    fixed = lambda i, l: (0, 0)
    sel = _selector_matrices(tl, c)
    params = tuple(params) + (sel["seg"], sel["tril"])
    param_specs = [pl.BlockSpec(p.shape, fixed) for p in params]
    return pl.pallas_call(
        functools.partial(_rwkv_kernel, c=c, g=g, nl=nl),
        grid=(b, nl),
        in_specs=[pl.BlockSpec((1, tl, width), tile), pl.BlockSpec((1, 1, width), per_b3),
                  pl.BlockSpec((1, 4, HEAD, HEAD), per_b4)] + param_specs,
        out_specs=[pl.BlockSpec((1, tl, 256), tile), pl.BlockSpec((1, 1, width), per_b3),
                   pl.BlockSpec((1, 4, HEAD, HEAD), per_b4)],
        out_shape=[jax.ShapeDtypeStruct((b, seq, 256), BF16),
                   jax.ShapeDtypeStruct((b, 1, width), F32),
                   jax.ShapeDtypeStruct((b, 4, HEAD, HEAD), F32)],
        scratch_shapes=[pltpu.VMEM((tl + 8, width), F32), pltpu.VMEM((4, HEAD, HEAD), F32),
                        pltpu.VMEM((tl, 256), F32)],
        compiler_params=pltpu.CompilerParams(dimension_semantics=("parallel", "arbitrary"),
                                             vmem_limit_bytes=VMEM_LIMIT),
        name="rwkv7",
    )(c_raw, shift0, s0, *params)


def _attn_kernel(i_tab, j_tab, flag_tab, q_ref, k_ref, v_ref, lam_ref, bn_ref, o_ref,
                 m_ref, acc_ref, *, tq, tk, past, lk_true, lam_init, half_tiles):
    p = pl.program_id(1)
    i = i_tab[p]
    j = j_tab[p]

    @pl.when(j == 0)
    def _():
        m_ref[...] = jnp.full(m_ref.shape, MASK_VALUE, F32)
        acc_ref[...] = jnp.zeros(acc_ref.shape, F32)

    def step(masked, nkeys, k0=0):
        nslab = nkeys // LANES
        if masked:
            q_pos = past + i * tq + _iota((tq, nkeys), 0)
            k_pos = j * tk + k0 + _iota((tq, nkeys), 1)
            visible = (jnp.right_shift(k_pos, 6) <= jnp.right_shift(q_pos, 6)) & (k_pos < lk_true)
        q = q_ref[0]
        k = k_ref[0, k0:k0 + nkeys, :]
        v = v_ref[0, k0:k0 + nkeys, :]

        def scores(hm):
            sl = slice(hm * HEAD, (hm + 1) * HEAD)
            return lax.dot_general(q[:, sl], k[:, sl], _NT, preferred_element_type=F32)

        ones = jnp.ones((nkeys, LANES), BF16)
        s_next = scores(0)
        for hm in range(8):
            h = hm // 2
            s = s_next
            if hm + 1 < 8:
                s_next = scores(hm + 1)
            if masked:
                s = jnp.where(visible, s, MASK_VALUE)
            slabs = [s[:, n * LANES:(n + 1) * LANES] for n in range(nslab)]
            m_cur = functools.reduce(jnp.maximum, slabs)
            m_prev = m_ref[hm]
            m_new = jnp.maximum(m_prev, jnp.max(m_cur, axis=1, keepdims=True))
            alpha = jnp.exp2(m_prev - m_new)
            e = jnp.concatenate([jnp.exp2(sb - m_new).astype(BF16) for sb in slabs], axis=1)
            v_ones = jnp.concatenate([v[:, h * LANES:(h + 1) * LANES], ones], axis=1)
            acc_ref[hm] = (jnp.concatenate([alpha, alpha], axis=1) * acc_ref[hm]
                           + jnp.dot(e, v_ones, preferred_element_type=F32))
            m_ref[hm] = m_new

    kind = jnp.right_shift(flag_tab[p], 1)

    @pl.when(kind == 0)
    def _():
        step(False, tk)

    @pl.when(kind == 1)
    def _():
        step(True, tk)

    if half_tiles:
        @pl.when(kind == 2)
        def _():
            step(True, tk // 2)

        @pl.when(kind == 3)
        def _():
            step(False, tk // 2)
            step(True, tk // 2, tk // 2)

    @pl.when(jnp.bitwise_and(flag_tab[p], 1) != 0)
    def _():
        lp = lam_ref[...]
        lam = (jnp.exp(jnp.sum(lp[0:1] * lp[1:2], axis=1, keepdims=True))
               - jnp.exp(jnp.sum(lp[2:3] * lp[3:4], axis=1, keepdims=True)) + lam_init)
        for h in range(4):
            a1 = acc_ref[2 * h]
            a2 = acc_ref[2 * h + 1]
            o = a1[:, :LANES] / a1[:, LANES:] - lam * (a2[:, :LANES] / a2[:, LANES:])
            o = _rmsnorm_rows(o, bn_ref[...]) * (1.0 - lam_init)
            o_ref[0, :, h * LANES:(h + 1) * LANES] = o.astype(BF16)


def _diff_attention(q, k, v, lam_params, bnorm_row, past, lam_init, tq, tk):
    b, lq, width = q.shape
    lk_true = k.shape[1]
    nk = -(-lk_true // tk)
    if nk * tk != lk_true:
        pad = ((0, 0), (0, nk * tk - lk_true), (0, 0))
        k = jnp.pad(k, pad)
        v = jnp.pad(v, pad)
    nq = lq // tq
    half_tiles = (tk // 2) % LANES == 0
    i_list, j_list, flag_list = [], [], []
    for i in range(nq):
        first_pos = past + i * tq
        last_pos = first_pos + tq - 1
        visible_end = min((last_pos // CHUNK + 1) * CHUNK, lk_true)
        j_max = (visible_end - 1) // tk
        for j in range(j_max + 1):
            key_end = (j + 1) * tk
            if (key_end - 1) // CHUNK <= first_pos // CHUNK and key_end <= lk_true:
                kind = 0
            elif half_tiles and visible_end - j * tk <= tk // 2:
                kind = 2
            elif (half_tiles and (j * tk + tk // 2 - 1) // CHUNK <= first_pos // CHUNK
                  and j * tk + tk // 2 <= lk_true):
                kind = 3
            else:
                kind = 1
            i_list.append(i)
            j_list.append(j)
            flag_list.append((1 if j == j_max else 0) + 2 * kind)
    tabs = [jnp.asarray(np.asarray(t, np.int32)) for t in (i_list, j_list, flag_list)]
    grid_spec = pltpu.PrefetchScalarGridSpec(
        num_scalar_prefetch=3,
        grid=(b, len(i_list)),
        in_specs=[pl.BlockSpec((1, tq, width), lambda bi, p, it, jt, lt: (bi, it[p], 0)),
                  pl.BlockSpec((1, tk, width), lambda bi, p, it, jt, lt: (bi, jt[p], 0)),
                  pl.BlockSpec((1, tk, width), lambda bi, p, it, jt, lt: (bi, jt[p], 0)),
                  pl.BlockSpec((4, HEAD), lambda bi, p, it, jt, lt: (0, 0)),
                  pl.BlockSpec((1, LANES), lambda bi, p, it, jt, lt: (0, 0))],
        out_specs=pl.BlockSpec((1, tq, width), lambda bi, p, it, jt, lt: (bi, it[p], 0)),
        scratch_shapes=[pltpu.VMEM((8, tq, LANES), F32), pltpu.VMEM((8, tq, 2 * LANES), F32)],
    )
    return pl.pallas_call(
        functools.partial(_attn_kernel, tq=tq, tk=tk, past=past, lk_true=lk_true, lam_init=lam_init,
                          half_tiles=half_tiles),
        grid_spec=grid_spec,
        out_shape=jax.ShapeDtypeStruct((b, lq, width), BF16),
        compiler_params=pltpu.CompilerParams(dimension_semantics=("parallel", "arbitrary"),
                                             vmem_limit_bytes=VMEM_LIMIT),
        name="diff_attention",
    )(*tabs, q, k, v, lam_params, bnorm_row)


def _mlp_kernel(h_ref, ma_ref, mb_ref, mc_ref, wo_ref, nf_ref, w1_ref, w2_ref, np_ref, wg_ref,
                wp_ref, p_ref, nfin_ref, o_ref, *, tf, final_norm):
    mixed = (jnp.dot(ma_ref[...], wo_ref[0:256, :], preferred_element_type=F32)
             + jnp.dot(mb_ref[...], wo_ref[256:768, :], preferred_element_type=F32)
             + jnp.dot(mc_ref[...], wo_ref[768:1024, :], preferred_element_type=F32))
    h1 = h_ref[...] + mixed
    xn = _rmsnorm_rows(h1, nf_ref[...]).astype(BF16)
    h2 = h1
    for f0 in range(0, w1_ref.shape[1], tf):
        u = jnp.maximum(jnp.dot(xn, w1_ref[:, f0:f0 + tf], preferred_element_type=F32), 0.0)
        h2 = h2 + jnp.dot((u * u).astype(BF16), w2_ref[f0:f0 + tf, :], preferred_element_type=F32)
    gate = jax.nn.sigmoid(jnp.dot(_rmsnorm_rows(h2, np_ref[...]).astype(BF16), wg_ref[...],
                                  preferred_element_type=F32))
    h3 = h2 + gate * jnp.dot(p_ref[...].astype(BF16), wp_ref[...], preferred_element_type=F32)
    if final_norm:
        h3 = _rmsnorm_rows(h3, nfin_ref[...])
    o_ref[...] = h3


def _mix_mlp(h2d, mix_a, mix_b, mix_c, w_out, norm_ffn, w_ff1, w_ff2, norm_ple, w_gate, w_proj,
             p2d, norm_final, final_norm, tm, tf):
    t, d = h2d.shape
    dff = w_ff1.shape[1]
    row = lambda i: (i, 0)
    fixed = lambda i: (0, 0)
    resident = lambda shape: pl.BlockSpec(shape, fixed, pipeline_mode=pl.Buffered(1))
    return pl.pallas_call(
        functools.partial(_mlp_kernel, tf=tf, final_norm=final_norm),
        grid=(t // tm,),
        in_specs=[pl.BlockSpec((tm, d), row), pl.BlockSpec((tm, 256), row),
                  pl.BlockSpec((tm, 512), row), pl.BlockSpec((tm, 256), row),
                  resident((d, d)), resident((1, d)), resident((d, dff)), resident((dff, d)),
                  resident((1, d)), resident((d, d)), resident((p2d.shape[1], d)),
                  pl.BlockSpec((tm, p2d.shape[1]), row), resident((1, d))],
        out_specs=pl.BlockSpec((tm, d), row),
        out_shape=jax.ShapeDtypeStruct((t, d), F32),
        compiler_params=pltpu.CompilerParams(dimension_semantics=("parallel",),
                                             vmem_limit_bytes=VMEM_LIMIT),
        name="mix_mlp",
    )(h2d, mix_a, mix_b, mix_c, w_out, norm_ffn, w_ff1, w_ff2, norm_ple, w_gate, w_proj, p2d,
      norm_final)


def _rope_tables(past, seq_len):
    half = HEAD // 2
    inv = ROPE_THETA ** (-2.0 * jnp.arange(half, dtype=F32) / HEAD)
    pos = past + jnp.arange(seq_len, dtype=jnp.int32)
    ang = pos.astype(F32)[:, None] * inv[None, :]
    cos, sin, zero = jnp.cos(ang), jnp.sin(ang), jnp.zeros_like(ang)
    rep = LANES // HEAD
    cos_t = jnp.tile(jnp.concatenate([cos, cos], axis=1), (1, rep))
    sa_t = jnp.tile(jnp.concatenate([-sin, zero], axis=1), (1, rep))
    sb_t = jnp.tile(jnp.concatenate([zero, sin], axis=1), (1, rep))
    return cos_t, sa_t, sb_t


def _pad_lanes(v, width=LANES):
    return jnp.pad(v, (0, width - v.shape[0]))[None, :]


def _prep_layer(i, norm_mix, w_in, a_conv_w, a_A_log, a_dt_bias, a_norm,
                b_lam_q1, b_lam_k1, b_lam_q2, b_lam_k2, b_norm,
                c_mu, c_w0, c_w_up, c_a0, c_a_up, c_g_up, c_k_k, c_k_a, c_r_k, c_ln_w, c_ln_b,
                w_out, norm_ffn, w_ff1, w_ff2, norm_ple, w_ple_gate, w_ple_proj):
    w = w_in[i]
    d = w.shape[0]
    w_perm = jnp.concatenate(
        [w[:, 0:1024], w[:, 1032:3592], w[:, 1024:1032],
         jnp.zeros((d, _PROJ_WIDTH - 3592), w.dtype)], axis=1).astype(BF16)
    zeros_rank = jnp.zeros_like(c_w_up[i])
    rwkv = (c_mu[i][None, :], c_w0[i][None, :],
            jnp.concatenate([c_w_up[i], zeros_rank], axis=0).astype(BF16), c_a0[i][None, :],
            jnp.concatenate([jnp.zeros_like(c_a_up[i]), c_a_up[i]], axis=0).astype(BF16),
            c_g_up[i].astype(BF16), c_k_k[i][None, :], c_k_a[i][None, :],
            c_r_k[i].reshape(1, -1), c_ln_w[i][None, :], c_ln_b[i][None, :])
    return dict(
        norm_mix=norm_mix[i][None, :], w_in=w_perm, conv_w=a_conv_w[i],
        alog=_pad_lanes(a_A_log[i]), dtb=_pad_lanes(a_dt_bias[i]),
        anorm=jnp.tile(a_norm[i], 4)[None, :],
        lam=jnp.stack([b_lam_q1[i], b_lam_k1[i], b_lam_q2[i], b_lam_k2[i]], axis=0),
        bnorm=b_norm[i][None, :], rwkv=rwkv,
        w_out=w_out[i].astype(BF16), norm_ffn=norm_ffn[i][None, :],
        w_ff1=w_ff1[i].astype(BF16), w_ff2=w_ff2[i].astype(BF16),
        norm_ple=norm_ple[i][None, :], w_gate=w_ple_gate[i].astype(BF16),
        w_proj=w_ple_proj[i].astype(BF16))


def _pick_tile(n, target):
    t = min(n, target)
    while n % t:
        t //= 2
    return t


def _trunk(x, p, cache_k, cache_v, conv_buf, delta_s, shift_prev, wkv_s, layers, norm_final):
    b, seq, d = x.shape
    depth = len(layers)
    past = cache_k.shape[2]
    t = b * seq
    tm = _pick_tile(t, PROJ_TM)
    rope_tabs = _rope_tables(past, seq)
    h = x.reshape(t, d)
    states = []
    kv_stacked = None
    for i, lp in enumerate(layers):
        a_qkv, a_z, a_ab, q_b, k_f, v_f, k_b, v_b, c_raw = _input_projection(
            h, lp["norm_mix"], lp["w_in"], rope_tabs, seq, tm, i, depth, kv_stacked)
        kv_stacked = (k_f, v_f)
        mix_a, conv_n, delta_n = _gated_delta(
            a_qkv.reshape(b, seq, -1), a_z.reshape(b, seq, -1), a_ab.reshape(b, seq, -1),
            conv_buf[i], delta_s[i], lp["conv_w"], lp["alog"], lp["dtb"], lp["anorm"])
        k_all = k_b.reshape(b, seq, -1)
        v_all = v_b.reshape(b, seq, -1)
        if past:
            k_all = jnp.concatenate([cache_k[i].reshape(b, past, -1).astype(BF16), k_all], axis=1)
            v_all = jnp.concatenate([cache_v[i].reshape(b, past, -1).astype(BF16), v_all], axis=1)
        lam_init = 0.8 - 0.6 * math.exp(-0.3 * i)
        mix_b = _diff_attention(q_b.reshape(b, seq, -1), k_all, v_all, lp["lam"], lp["bnorm"],
                                past, lam_init, _pick_tile(seq, ATTN_TQ),
                                ATTN_TK if past + seq >= 8 * ATTN_TK else ATTN_TK // 2)
        mix_c, shift_n, wkv_n = _rwkv7(c_raw.reshape(b, seq, -1), shift_prev[i][:, None, :],
                                       wkv_s[i], lp["rwkv"])
        h = _mix_mlp(h, mix_a.reshape(t, -1), mix_b.reshape(t, -1), mix_c.reshape(t, -1),
                     lp["w_out"], lp["norm_ffn"], lp["w_ff1"], lp["w_ff2"], lp["norm_ple"],
                     lp["w_gate"], lp["w_proj"], p[i].reshape(t, -1), norm_final[None, :],
                     i == depth - 1, _pick_tile(t, MLP_TM), MLP_TF)
        states.append((conv_n, delta_n, shift_n[:, 0, :], wkv_n))
    conv_all, delta_all, shift_all, wkv_all = (
        jnp.stack([st[j] for st in states], axis=0) for j in range(4))
    k_all_layers, v_all_layers = (a.reshape(depth, b, seq, 4, LANES) for a in kv_stacked)
    return h.reshape(b, seq, d), [conv_all, delta_all, k_all_layers, v_all_layers, shift_all, wkv_all]


def kernel(x_prompt, x_sample, cache_b_k, cache_b_v, state_a_conv, state_a_delta, state_c_shift, state_c_wkv, p_prompt, p_sample, norm_mix, w_in, a_conv_w, a_A_log, a_dt_bias, a_norm, b_lam_q1, b_lam_k1, b_lam_q2, b_lam_k2, b_norm, c_mu, c_w0, c_w_up, c_a0, c_a_up, c_g_up, c_k_k, c_k_a, c_r_k, c_ln_w, c_ln_b, w_out, norm_ffn, w_ff1, w_ff2, norm_ple, w_ple_gate, w_ple_proj, norm_final):
    depth = w_in.shape[0]
    layers = [_prep_layer(i, norm_mix, w_in, a_conv_w, a_A_log, a_dt_bias, a_norm,
                          b_lam_q1, b_lam_k1, b_lam_q2, b_lam_k2, b_norm,
                          c_mu, c_w0, c_w_up, c_a0, c_a_up, c_g_up, c_k_k, c_k_a, c_r_k, c_ln_w, c_ln_b,
                          w_out, norm_ffn, w_ff1, w_ff2, norm_ple, w_ple_gate, w_ple_proj)
              for i in range(depth)]
    bp = x_prompt.shape[0]
    dt = x_prompt.dtype
    zeros = lambda ref: jnp.zeros((depth, bp) + ref.shape[2:], dt)
    empty_k = jnp.zeros((depth, bp, 0) + cache_b_k.shape[3:], dt)
    empty_v = jnp.zeros((depth, bp, 0) + cache_b_v.shape[3:], dt)
    y_prompt, st_p = _trunk(x_prompt, p_prompt, empty_k, empty_v, zeros(state_a_conv),
                            zeros(state_a_delta), zeros(state_c_shift), zeros(state_c_wkv),
                            layers, norm_final)
    y_sample, st_s = _trunk(x_sample, p_sample, cache_b_k, cache_b_v, state_a_conv, state_a_delta,
                            state_c_shift, state_c_wkv, layers, norm_final)
    return (y_prompt, y_sample, *st_p, *st_s)
```
